```python
import jax, jax.numpy as jnp
from jax import lax
import numpy as np

D_MODEL = 1024
BATCH = 8
SEQ = 4096
DEPTH = 1
DEC_BATCH = 32
DEC_SEQ = 1
PAST_LEN = 16384
PAGE_SIZE = 128

HEAD_DIM = 64
GDN_HEADS = 8
FOX_HEADS = 8
GDN_WIDTH = GDN_HEADS * HEAD_DIM
FOX_WIDTH = FOX_HEADS * HEAD_DIM
MIX_WIDTH = GDN_WIDTH + FOX_WIDTH
CONV_WIDTH = 4
GDN_CHUNK = 64
FOX_BLOCK = 128
PROJ_SPLITS = (GDN_WIDTH, GDN_WIDTH, GDN_WIDTH, GDN_WIDTH, GDN_HEADS, GDN_HEADS, FOX_WIDTH, FOX_WIDTH, FOX_WIDTH, FOX_HEADS)
PROJ_WIDTH = 4 * GDN_WIDTH + 2 * GDN_HEADS + 3 * FOX_WIDTH + FOX_HEADS
PEER_HEADS = 8
N_KEYS = 128
N_EXPERTS = N_KEYS * N_KEYS
PEER_TOPK = 16
PEER_DK_HALF = 128
PEER_BLOCK = 256
N_MOD = 6
EPS = 1e-6

kernel_name = 'hymba_gdn_fox_peer_adaln_step'


def rms(x):
    xf = x.astype(jnp.float32)
    return xf * lax.rsqrt(jnp.mean(xf * xf, axis=-1, keepdims=True) + EPS)


def modulate(x, shift, scale):
    y = rms(x) * (1.0 + scale[:, None, :].astype(jnp.float32)) + shift[:, None, :].astype(jnp.float32)
    return y.astype(x.dtype)


def l2norm(x):
    return x * lax.rsqrt(jnp.sum(x * x, axis=-1, keepdims=True) + 1e-6)


def causal_conv(u, buf, w):
    xp = jnp.concatenate([buf.astype(u.dtype), u], axis=1)
    out = lax.conv_general_dilated(xp, w[:, None, :].astype(u.dtype), window_strides=(1,), padding='VALID',
                                   dimension_numbers=('NWC', 'WIO', 'NWC'), feature_group_count=u.shape[-1])
    return out, xp[:, xp.shape[1] - (CONV_WIDTH - 1):]


def gated_delta_chunked(q, k, v, g, beta, s0):
    B, L, H, _ = q.shape
    DV = v.shape[-1]
    C = GDN_CHUNK if L >= GDN_CHUNK else L
    Lp = -(-L // C) * C
    pad = Lp - L

    def to_chunks(t):
        t = jnp.pad(t, [(0, 0), (0, pad)] + [(0, 0)] * (t.ndim - 2))
        t = t.reshape((B, Lp // C, C) + t.shape[2:])
        return jnp.moveaxis(jnp.moveaxis(t, 1, 0), 3, 2)

    qc, kc, vc, gc, bc = to_chunks(q), to_chunks(k), to_chunks(v), to_chunks(g), to_chunks(beta)
    G = jnp.cumsum(gc, axis=-1)
    causal = jnp.tril(jnp.ones((C, C), dtype=bool))
    strict = jnp.tril(jnp.ones((C, C), dtype=bool), k=-1)
    decay = jnp.exp(jnp.where(causal, G[..., :, None] - G[..., None, :], -jnp.inf))
    A = jnp.where(strict, bc[..., :, None] * jnp.einsum('nbhcd,nbhed->nbhce', kc, kc) * decay, 0.0)
    rhs = jnp.concatenate([vc * bc[..., None], kc * (bc * jnp.exp(G))[..., None]], axis=-1)
    sol = lax.linalg.triangular_solve(A, rhs, left_side=True, lower=True, unit_diagonal=True)
    Uc, Wc = sol[..., :DV], sol[..., DV:]
    Aqk = jnp.einsum('nbhcd,nbhed->nbhce', qc, kc) * decay
    qG = qc * jnp.exp(G)[..., None]
    kG = kc * jnp.exp(G[..., -1:] - G)[..., None]
    gl = jnp.exp(G[..., -1])

    def step(S, xs):
        Ui, Wi, Ai, qi, ki, gi = xs
        vn = Ui - jnp.einsum('bhcd,bhde->bhce', Wi, S)
        o = jnp.einsum('bhcd,bhde->bhce', qi, S) + jnp.einsum('bhcs,bhse->bhce', Ai, vn)
        S = S * gi[..., None, None] + jnp.einsum('bhcd,bhce->bhde', ki, vn)
        return S, o

    S, o = lax.scan(step, s0, (Uc, Wc, Aqk, qG, kG, gl))
    o = jnp.moveaxis(jnp.moveaxis(o, 2, 3), 0, 1).reshape(B, Lp, H, DV)[:, :L]
    return o, S


def fox_attend(q, k, v, Fq, Fk, q_pos, k_pos):
    s = jnp.einsum('bqhd,bkhd->bhqk', q, k).astype(jnp.float32) * (HEAD_DIM ** -0.5)
    s = s + jnp.swapaxes(Fq, 1, 2)[..., :, None] - jnp.swapaxes(Fk, 1, 2)[..., None, :]
    s = jnp.where(k_pos[None, :] <= q_pos[:, None], s, -jnp.inf)
    p = jax.nn.softmax(s, axis=-1)
    return jnp.einsum('bhqk,bkhd->bqhd', p.astype(v.dtype), v)


def fox_prompt(q, k, v, F):
    B, L, H, Dh = q.shape
    nb = L // FOX_BLOCK
    qb = jnp.swapaxes(q.reshape(B, nb, FOX_BLOCK, H, Dh), 0, 1)
    Fb = jnp.swapaxes(F.reshape(B, nb, FOX_BLOCK, H), 0, 1)
    starts = jnp.arange(nb) * FOX_BLOCK
    k_pos = jnp.arange(L)

    def one(args):
        qi, Fi, st = args
        return fox_attend(qi, k, v, Fi, F, st + jnp.arange(FOX_BLOCK), k_pos)

    o = lax.map(one, (qb, Fb, starts))
    return jnp.swapaxes(o, 0, 1).reshape(B, L, H, Dh)


def token_mixers(h, w_in, conv_w, a_log, dt_bias, gdn_norm_w, fox_fb, w_out, conv_buf, s0, past):
    B, L, _ = h.shape
    proj = h @ w_in
    gq, gk, gv, gz, ga, gb, fq, fk, fv, ff = jnp.split(proj, np.cumsum(PROJ_SPLITS)[:-1].tolist(), axis=-1)
    qkv, new_buf = causal_conv(jnp.concatenate([gq, gk, gv], axis=-1), conv_buf, conv_w)
    cq, ck, cv = jnp.split(jax.nn.silu(qkv), 3, axis=-1)
    q = l2norm(cq.reshape(B, L, GDN_HEADS, HEAD_DIM).astype(jnp.float32)) * (HEAD_DIM ** -0.5)
    k = l2norm(ck.reshape(B, L, GDN_HEADS, HEAD_DIM).astype(jnp.float32))
    v = cv.reshape(B, L, GDN_HEADS, HEAD_DIM).astype(jnp.float32)
    beta = jax.nn.sigmoid(gb.astype(jnp.float32))
    g = -jnp.exp(a_log.astype(jnp.float32)) * jax.nn.softplus(ga.astype(jnp.float32) + dt_bias.astype(jnp.float32))
    o, S = gated_delta_chunked(q, k, v, g, beta, s0.astype(jnp.float32))
    o = rms(o) * gdn_norm_w.astype(jnp.float32) * jax.nn.silu(gz.reshape(B, L, GDN_HEADS, HEAD_DIM).astype(jnp.float32))
    o_gdn = o.reshape(B, L, GDN_WIDTH).astype(h.dtype)
    fq = fq.reshape(B, L, FOX_HEADS, HEAD_DIM)
    fk = fk.reshape(B, L, FOX_HEADS, HEAD_DIM)
    fv = fv.reshape(B, L, FOX_HEADS, HEAD_DIM)
    logf = jax.nn.log_sigmoid((ff + fox_fb).astype(jnp.float32))
    if past is None:
        o_fox = fox_prompt(fq, fk, fv, jnp.cumsum(logf, axis=1))
    else:
        k_past, v_past, logf_past = past
        P = k_past.shape[1]
        k_all = jnp.concatenate([k_past.astype(fk.dtype), fk], axis=1)
        v_all = jnp.concatenate([v_past.astype(fv.dtype), fv], axis=1)
        F = jnp.cumsum(jnp.concatenate([logf_past.astype(jnp.float32), logf], axis=1), axis=1)
        o_fox = fox_attend(fq, k_all, v_all, F[:, P:], F, P + jnp.arange(L), jnp.arange(P + L))
    y = jnp.concatenate([o_gdn, o_fox.reshape(B, L, FOX_WIDTH).astype(h.dtype)], axis=-1) @ w_out
    dt = h.dtype
    return y, (fk.astype(dt), fv.astype(dt), logf.astype(dt), new_buf.astype(dt), S.astype(dt))


def peer(h, wq, subkeys, pu, pv):
    B, L, D = h.shape
    n = B * L
    blk = min(PEER_BLOCK, n)
    nb = -(-n // blk)
    xs = jnp.pad(h.reshape(n, D), ((0, nb * blk - n), (0, 0))).reshape(nb, blk, D)

    def one(xb):
        q = (xb @ wq).reshape(blk, PEER_HEADS, 2, PEER_DK_HALF)
        s = jnp.einsum('thpc,hpnc->thpn', q, subkeys).astype(jnp.float32)
        sv, si = lax.top_k(s, PEER_TOPK)
        cand = (sv[:, :, 0, :, None] + sv[:, :, 1, None, :]).reshape(blk, PEER_HEADS, PEER_TOPK * PEER_TOPK)
        cidx = (si[:, :, 0, :, None] * N_KEYS + si[:, :, 1, None, :]).reshape(blk, PEER_HEADS, PEER_TOPK * PEER_TOPK)
        fv, fi = lax.top_k(cand, PEER_TOPK)
        eidx = jnp.take_along_axis(cidx, fi, axis=-1)
        gate = jax.nn.softmax(fv, axis=-1)
        act = jax.nn.gelu(jnp.einsum('thkd,td->thk', pu[eidx], xb).astype(jnp.float32), approximate=False) * gate
        return jnp.einsum('thk,thkd->td', act.astype(xb.dtype), pv[eidx])

    out = lax.map(one, xs)
    return out.reshape(nb * blk, D)[:n].reshape(B, L, D)


def layer(x, c, w_mod, b_mod, w_in, conv_w, a_log, dt_bias, gdn_norm_w, fox_fb, w_out,
          peer_wq, peer_subkeys, peer_u, peer_v, conv_buf, s0, past):
    mod = jax.nn.silu(c) @ w_mod + b_mod
    sh1, sc1, g1, sh2, sc2, g2 = jnp.split(mod, N_MOD, axis=-1)
    m, st = token_mixers(modulate(x, sh1, sc1), w_in, conv_w, a_log, dt_bias, gdn_norm_w, fox_fb, w_out,
                         conv_buf, s0, past)
    x = x + g1[:, None, :] * m
    x = x + g2[:, None, :] * peer(modulate(x, sh2, sc2), peer_wq, peer_subkeys, peer_u, peer_v)
    return x, st


def setup_inputs(seed: int = 0) -> dict:
    key = jax.random.key(seed)
    ks = jax.random.split(key, 24)
    n_pages = PAST_LEN // PAGE_SIZE
    n_used = DEC_BATCH * n_pages
    n_pool = (n_used * 5) // 4
    nrm = jax.random.normal
    page_table = jax.random.permutation(ks[0], n_pool)[:n_used].reshape(DEC_BATCH, n_pages).astype(jnp.int32)
    dt = jnp.exp(jax.random.uniform(ks[1], (DEPTH, GDN_HEADS), minval=np.log(1e-3), maxval=np.log(1e-1)))
    return {
        'x_prompt': nrm(ks[2], (BATCH, SEQ, D_MODEL), jnp.float32),
        'x_sample': nrm(ks[3], (DEC_BATCH, DEC_SEQ, D_MODEL), jnp.float32),
        'cache_k': nrm(ks[4], (DEPTH, n_pool, PAGE_SIZE, FOX_HEADS, HEAD_DIM), jnp.float32),
        'cache_v': nrm(ks[5], (DEPTH, n_pool, PAGE_SIZE, FOX_HEADS, HEAD_DIM), jnp.float32),
        'cache_logf': jax.nn.log_sigmoid(3.0 + 0.5 * nrm(ks[6], (DEPTH, n_pool, PAGE_SIZE, FOX_HEADS), jnp.float32)),
        'state_conv': nrm(ks[7], (DEPTH, DEC_BATCH, CONV_WIDTH - 1, 3 * GDN_WIDTH), jnp.float32),
        'state_delta': 0.1 * nrm(ks[8], (DEPTH, DEC_BATCH, GDN_HEADS, HEAD_DIM, HEAD_DIM), jnp.float32),
        'page_table': page_table,
        'c_prompt': nrm(ks[9], (BATCH, D_MODEL), jnp.float32),
        'c_sample': nrm(ks[10], (DEC_BATCH, D_MODEL), jnp.float32),
        'w_mod': nrm(ks[11], (DEPTH, D_MODEL, N_MOD * D_MODEL), jnp.float32) * D_MODEL ** -0.5,
        'b_mod': 0.02 * nrm(ks[12], (DEPTH, N_MOD * D_MODEL), jnp.float32),
        'w_in': nrm(ks[13], (DEPTH, D_MODEL, PROJ_WIDTH), jnp.float32) * D_MODEL ** -0.5,
        'conv_w': 0.5 * nrm(ks[14], (DEPTH, CONV_WIDTH, 3 * GDN_WIDTH), jnp.float32),
        'a_log': jnp.log(jax.random.uniform(ks[15], (DEPTH, GDN_HEADS), minval=1.0, maxval=16.0)),
        'dt_bias': dt + jnp.log(-jnp.expm1(-dt)),
        'gdn_norm_w': 1.0 + 0.02 * nrm(ks[16], (DEPTH, HEAD_DIM), jnp.float32),
        'fox_fb': 2.0 + 0.5 * nrm(ks[17], (DEPTH, FOX_HEADS), jnp.float32),
        'w_out': nrm(ks[18], (DEPTH, MIX_WIDTH, D_MODEL), jnp.float32) * MIX_WIDTH ** -0.5,
        'peer_wq': nrm(ks[19], (DEPTH, D_MODEL, PEER_HEADS * 2 * PEER_DK_HALF), jnp.float32) * D_MODEL ** -0.5,
        'peer_subkeys': nrm(ks[20], (DEPTH, PEER_HEADS, 2, N_KEYS, PEER_DK_HALF), jnp.float32) * PEER_DK_HALF ** -0.5,
        'peer_u': nrm(ks[21], (DEPTH, N_EXPERTS, D_MODEL), jnp.float32) * D_MODEL ** -0.5,
        'peer_v': nrm(ks[22], (DEPTH, N_EXPERTS, D_MODEL), jnp.float32) * PEER_HEADS ** -0.5,
        'final_norm_w': 1.0 + 0.02 * nrm(ks[23], (D_MODEL,), jnp.float32),
    }


def reference(x_prompt, x_sample, cache_k, cache_v, cache_logf, state_conv, state_delta, page_table,
              c_prompt, c_sample, w_mod, b_mod, w_in, conv_w, a_log, dt_bias, gdn_norm_w, fox_fb, w_out,
              peer_wq, peer_subkeys, peer_u, peer_v, final_norm_w):
    Bp = x_prompt.shape[0]
    Bs = x_sample.shape[0]
    n_pages = page_table.shape[1]
    xp, xs = x_prompt, x_sample
    st_p, st_s = [], []
    for l in range(DEPTH):
        lw = (w_mod[l], b_mod[l], w_in[l], conv_w[l], a_log[l], dt_bias[l], gdn_norm_w[l], fox_fb[l], w_out[l],
              peer_wq[l], peer_subkeys[l], peer_u[l], peer_v[l])
        buf0 = jnp.zeros((Bp, CONV_WIDTH - 1, 3 * GDN_WIDTH), x_prompt.dtype)
        s00 = jnp.zeros((Bp, GDN_HEADS, HEAD_DIM, HEAD_DIM), jnp.float32)
        xp, sp = layer(xp, c_prompt, *lw, buf0, s00, None)
        st_p.append(sp)
        k_past = cache_k[l][page_table].reshape(Bs, n_pages * PAGE_SIZE, FOX_HEADS, HEAD_DIM)
        v_past = cache_v[l][page_table].reshape(Bs, n_pages * PAGE_SIZE, FOX_HEADS, HEAD_DIM)
        f_past = cache_logf[l][page_table].reshape(Bs, n_pages * PAGE_SIZE, FOX_HEADS)
        xs, ss = layer(xs, c_sample, *lw, state_conv[l], state_delta[l], (k_past, v_past, f_past))
        st_s.append(ss)
    fw = final_norm_w.astype(jnp.float32)
    y_prompt = (rms(xp) * fw).astype(x_prompt.dtype)
    y_sample = (rms(xs) * fw).astype(x_sample.dtype)
    k_prompt = jnp.stack([s[0] for s in st_p])
    v_prompt = jnp.stack([s[1] for s in st_p])
    logf_prompt = jnp.stack([s[2] for s in st_p])
    conv_prompt = jnp.stack([s[3] for s in st_p])
    delta_prompt = jnp.stack([s[4] for s in st_p])
    k_sample = jnp.stack([s[0] for s in st_s])
    v_sample = jnp.stack([s[1] for s in st_s])
    logf_sample = jnp.stack([s[2] for s in st_s])
    conv_sample = jnp.stack([s[3] for s in st_s])
    delta_sample = jnp.stack([s[4] for s in st_s])
    return (y_prompt, y_sample, k_prompt, v_prompt, logf_prompt, conv_prompt, delta_prompt,
            k_sample, v_sample, logf_sample, conv_sample, delta_sample)
```

```python
import functools

import numpy as np
import jax
import jax.numpy as jnp
from jax import lax
from jax.experimental import pallas as pl
from jax.experimental.pallas import tpu as pltpu

F32 = jnp.float32
BF16 = jnp.bfloat16
I32 = jnp.int32

HEAD_DIM = 64
GDN_HEADS = 8
FOX_HEADS = 8
GDN_WIDTH = GDN_HEADS * HEAD_DIM
FOX_WIDTH = FOX_HEADS * HEAD_DIM
CONV_WIDTH = 4
PEER_HEADS = 8
N_KEYS = 128
PEER_TOPK = 16
PEER_DK_HALF = 128
N_MOD = 6
EPS = 1e-6
LANES = 128
SUBLANES = 8
GDN_CHUNK = 128
SMALL_W = LANES
HIGHEST = lax.Precision.HIGHEST
NEG_INF = float("-inf")


def _dot(a, b, precision=None):
    return jnp.dot(a, b, preferred_element_type=F32, precision=precision)


def _dot_nt(a, b, precision=None):
    return lax.dot_general(a, b, (((1,), (1,)), ((), ())), preferred_element_type=F32, precision=precision)


def _bdot(a, b):
    return _dot(a.astype(BF16), b.astype(BF16))


def _bdot_nt(a, b):
    return _dot_nt(a.astype(BF16), b.astype(BF16))


def _split3(x):
    hi = x.astype(BF16)
    r1 = x - hi.astype(F32)
    mid = r1.astype(BF16)
    lo = (r1 - mid.astype(F32)).astype(BF16)
    return hi, mid, lo


def _dot3(a, b):
    ah = a.astype(BF16)
    al = (a - ah.astype(F32)).astype(BF16)
    bh = b.astype(BF16)
    bl = (b - bh.astype(F32)).astype(BF16)
    return _dot(ah, bh) + (_dot(ah, bl) + _dot(al, bh))


def _params(sem, vmem_mb=None):
    kw = dict(dimension_semantics=sem)
    if vmem_mb is not None:
        kw["vmem_limit_bytes"] = vmem_mb * 1024 * 1024
    return pltpu.CompilerParams(**kw)


def _rms(x):
    return x * lax.rsqrt(jnp.mean(x * x, axis=-1, keepdims=True) + EPS)


def _silu(x):
    return x * jax.nn.sigmoid(x)


def _mod_kernel(c_ref, w_ref, b_ref, o_ref):
    s = _silu(c_ref[...])
    o_ref[...] = _bdot(s, w_ref[...]) + b_ref[...]


def _mod(c, w, b):
    n, d = c.shape
    nout = w.shape[1]
    tn = 1024
    return pl.pallas_call(
        _mod_kernel,
        grid=(nout // tn,),
        in_specs=[pl.BlockSpec((n, d), lambda j: (0, 0)),
                  pl.BlockSpec((d, tn), lambda j: (0, j)),
                  pl.BlockSpec((1, tn), lambda j: (0, j))],
        out_specs=pl.BlockSpec((n, tn), lambda j: (0, j)),
        out_shape=jax.ShapeDtypeStruct((n, nout), F32),
        compiler_params=_params(("arbitrary",)),
        name="mod",
    )(c, w, b.reshape(1, nout))


def _spread_heads(a):
    tm = a.shape[0]
    lane = lax.broadcasted_iota(I32, (tm, LANES), 1)
    low = lane < HEAD_DIM
    out = []
    for j in range(a.shape[1] // LANES):
        blk = a[:, j * LANES:(j + 1) * LANES]
        out.append(jnp.where(low, blk, 0.0))
        out.append(jnp.where(low, pltpu.roll(blk, HEAD_DIM, axis=1), 0.0))
    return jnp.concatenate(out, axis=1)


def _inproj_kernel(tiles_per_seq, x_ref, sh_ref, sc_ref, w_ref, padd_ref, alog_ref, selq_ref, selk_ref, cq_ref,
                   ck_ref, cv_ref, gqkv_ref, z_ref, small_ref, fk_ref, fv_ref, qa_ref, ka_ref, va_ref, *rest):
    carry_ref = rest[-1]
    i = pl.program_id(0)
    tm = x_ref.shape[0]
    h = _rms(x_ref[...]) * (1.0 + sc_ref[0]) + sh_ref[0]
    hb = h.astype(BF16)
    o = 0
    gqkv_ref[...] = _dot(hb, w_ref[:, o:o + 3 * GDN_WIDTH]); o += 3 * GDN_WIDTH
    z_ref[...] = _dot(hb, w_ref[:, o:o + GDN_WIDTH]); o += GDN_WIDTH
    sm = _dot(hb, w_ref[:, o:o + SMALL_W]); o += SMALL_W
    fq = _dot(hb, w_ref[:, o:o + FOX_WIDTH]); o += FOX_WIDTH
    fk = _dot(hb, w_ref[:, o:o + FOX_WIDTH]); o += FOX_WIDTH
    fv = _dot(hb, w_ref[:, o:o + FOX_WIDTH])
    fk_ref[...] = fk
    fv_ref[...] = fv
    if len(rest) == 2:
        rest[0][...] = fq

    lane = lax.broadcasted_iota(I32, (tm, SMALL_W), 1)
    y = sm + padd_ref[...]
    t = jnp.log1p(jnp.exp(-jnp.abs(y)))
    softplus = jnp.maximum(y, 0.0) + t
    logsig = jnp.minimum(y, 0.0) - t
    small = jnp.where(lane < GDN_HEADS, -jnp.exp(alog_ref[...]) * softplus,
                      jnp.where(lane < 2 * GDN_HEADS, jax.nn.sigmoid(sm),
                                jnp.where(lane < 2 * GDN_HEADS + FOX_HEADS, logsig, 0.0)))
    small_ref[...] = small

    @pl.when(i % tiles_per_seq == 0)
    def _():
        carry_ref[...] = jnp.zeros_like(carry_ref)

    r = lax.broadcasted_iota(I32, (tm, tm), 0)
    c = lax.broadcasted_iota(I32, (tm, tm), 1)
    ltri = (c <= r).astype(F32)
    cum = _dot(ltri, small, precision=HIGHEST) + carry_ref[...]
    carry_ref[...] = cum[tm - 1:tm, :]
    hi, mid, lo = _split3(cum)
    parts = jnp.concatenate([hi, mid, lo], axis=1)
    qa_ref[...] = (_spread_heads(fq * (HEAD_DIM ** -0.5)) + _dot(parts, selq_ref[...]) + cq_ref[...]).astype(BF16)
    ka_ref[...] = (_spread_heads(fk) + _dot(parts, selk_ref[...]) + ck_ref[...]).astype(BF16)
    va_ref[...] = (_spread_heads(fv) + cv_ref[...]).astype(BF16)


def _inproj_consts():
    selq = np.zeros((3 * SMALL_W, FOX_HEADS * LANES), np.float32)
    selk = np.zeros((3 * SMALL_W, FOX_HEADS * LANES), np.float32)
    cq = np.zeros((1, FOX_HEADS * LANES), np.float32)
    ck = np.zeros((1, FOX_HEADS * LANES), np.float32)
    cv = np.zeros((1, FOX_HEADS * LANES), np.float32)
    for h in range(FOX_HEADS):
        base = h * LANES + HEAD_DIM
        for p in range(3):
            src = p * SMALL_W + 2 * GDN_HEADS + h
            selq[src, base + p] = 1.0
            ck[0, base + p] = 1.0
            selk[src, base + 3 + p] = -1.0
            cq[0, base + 3 + p] = 1.0
        cv[0, base] = 1.0
    return (jnp.asarray(selq, BF16), jnp.asarray(selk, BF16), jnp.asarray(cq), jnp.asarray(ck), jnp.asarray(cv))


def _prep_inproj(w_in, a_log, dt_bias, fox_fb):
    o = np.cumsum((0, GDN_WIDTH, GDN_WIDTH, GDN_WIDTH, GDN_WIDTH, GDN_HEADS, GDN_HEADS, FOX_WIDTH, FOX_WIDTH,
                   FOX_WIDTH, FOX_HEADS)).tolist()
    d = w_in.shape[0]
    nsmall = 2 * GDN_HEADS + FOX_HEADS
    small = jnp.concatenate([w_in[:, o[4]:o[6]], w_in[:, o[9]:o[10]], jnp.zeros((d, SMALL_W - nsmall), w_in.dtype)], 1)
    w_cat = jnp.concatenate([w_in[:, :o[4]], small, w_in[:, o[6]:o[9]]], axis=1).astype(BF16)
    zero = jnp.zeros((GDN_HEADS,), F32)
    tail = jnp.zeros((SMALL_W - nsmall,), F32)
    padd = jnp.concatenate([dt_bias.astype(F32), zero, fox_fb.astype(F32), tail]).reshape(1, SMALL_W)
    alog = jnp.concatenate([a_log.astype(F32), zero, zero, tail]).reshape(1, SMALL_W)
    return w_cat, padd, alog


def _inproj(x, sh, sc, w_cat, padd, pmul, tm, tiles_per_seq, with_fq=False):
    T, D = x.shape
    rm = sh.shape[1]
    nw = w_cat.shape[1]
    selq, selk, cq, ck, cv = _inproj_consts()
    aw = FOX_HEADS * LANES
    row = lambda w: pl.BlockSpec((tm, w), lambda i: (i, 0))
    const = lambda a: pl.BlockSpec(a.shape, lambda i: (0,) * a.ndim)
    seq = pl.BlockSpec((1, rm, D), lambda i: (i // tiles_per_seq, 0, 0))
    outs = [(3 * GDN_WIDTH, F32), (GDN_WIDTH, F32), (SMALL_W, F32), (FOX_WIDTH, F32), (FOX_WIDTH, F32),
            (aw, BF16), (aw, BF16), (aw, BF16)]
    if with_fq:
        outs.append((FOX_WIDTH, F32))
    return pl.pallas_call(
        functools.partial(_inproj_kernel, tiles_per_seq),
        grid=(T // tm,),
        in_specs=[row(D), seq, seq, const(w_cat), const(padd), const(pmul), const(selq), const(selk), const(cq),
                  const(ck), const(cv)],
        out_specs=[row(w) for w, _ in outs],
        out_shape=[jax.ShapeDtypeStruct((T, w), dt) for w, dt in outs],
        scratch_shapes=[pltpu.VMEM((1, SMALL_W), F32)],
        compiler_params=_params(("arbitrary",), 48),
        name="inproj",
    )(x, sh, sc, w_cat, padd, pmul, selq, selk, cq, ck, cv)


def _head_block(a, base, h, low):
    j, odd = divmod(h, 2)
    blk = a[:, base + j * LANES:base + (j + 1) * LANES]
    if odd:
        blk = pltpu.roll(blk, HEAD_DIM, axis=1)
    return jnp.where(low, blk, 0.0)


def _join_heads(heads):
    return jnp.concatenate([heads[2 * j] + pltpu.roll(heads[2 * j + 1], HEAD_DIM, axis=1)
                            for j in range(len(heads) // 2)], axis=1)


def _unit_lower_inverse(a_mat, r, cc):
    n = a_mat.shape[0]
    s = SUBLANES
    same = (r // s) == (cc // s)
    b = jnp.where(same, -a_mat, 0.0)
    t = jnp.where(r == cc, 1.0, 0.0) + b
    b2 = _dot3(b, b)
    t = t + _dot3(t, b2)
    t = t + _dot3(t, _dot3(b2, b2))
    while s < n:
        same2 = (r // (2 * s)) == (cc // (2 * s))
        off = jnp.where(same2 & jnp.logical_not(same), a_mat, 0.0)
        t = t - _dot3(_dot3(t, off), t)
        same = same2
        s *= 2
    return t


def _gdn_kernel(u_ref, z_ref, sm_ref, cw_ref, nw_ref, o_ref, s_out_ref, ubuf, s_ref):
    c = pl.program_id(1)
    C = u_ref.shape[0]
    W3 = 3 * GDN_WIDTH

    @pl.when(c == 0)
    def _():
        ubuf[0:SUBLANES, :] = jnp.zeros((SUBLANES, W3), F32)
        s_ref[...] = jnp.zeros_like(s_ref)

    ubuf[SUBLANES:SUBLANES + C, :] = u_ref[...]
    cw = cw_ref[...]
    conv = cw[CONV_WIDTH - 1:CONV_WIDTH] * ubuf[SUBLANES:SUBLANES + C, :]
    for j in range(1, CONV_WIDTH):
        conv = conv + cw[CONV_WIDTH - 1 - j:CONV_WIDTH - j] * ubuf[SUBLANES - j:SUBLANES - j + C, :]
    ubuf[0:SUBLANES, :] = ubuf[C:C + SUBLANES, :]
    a = _silu(conv)
    z = z_ref[...]
    small = sm_ref[...]

    lane = lax.broadcasted_iota(I32, (C, LANES), 1)
    low = lane < HEAD_DIM
    r = lax.broadcasted_iota(I32, (C, C), 0)
    cc = lax.broadcasted_iota(I32, (C, C), 1)
    causal = cc <= r
    strict = cc < r
    g_cum = _dot(causal.astype(F32), small, precision=HIGHEST)
    g_last = g_cum[C - 1:C, :]
    e_g = jnp.exp(g_cum)
    e_gl = jnp.exp(g_last - g_cum)
    e_last = jnp.exp(g_last)
    neg_g_hi = pltpu.roll(-g_cum, HEAD_DIM, axis=1)
    nw = nw_ref[...]
    nsteps = int(np.log2(C))
    outs = []
    for h in range(GDN_HEADS):
        cq = _head_block(a, 0, h, low)
        ck = _head_block(a, GDN_WIDTH, h, low)
        v = _head_block(a, 2 * GDN_WIDTH, h, low)
        q = cq * lax.rsqrt(jnp.sum(cq * cq, axis=1, keepdims=True) + 1e-6) * (HEAD_DIM ** -0.5)
        k = ck * lax.rsqrt(jnp.sum(ck * ck, axis=1, keepdims=True) + 1e-6)
        beta = small[:, GDN_HEADS + h:GDN_HEADS + h + 1]
        p_mat = jnp.where(lane == h, g_cum, jnp.where(lane == h + HEAD_DIM, 1.0, 0.0))
        q_mat = jnp.where(lane == h, 1.0, jnp.where(lane == h + HEAD_DIM, neg_g_hi, 0.0))
        g_diff = _dot_nt(p_mat, q_mat, precision=HIGHEST)
        decay = jnp.exp(jnp.where(causal, g_diff, NEG_INF))
        a_mat = jnp.where(strict, beta * _bdot_nt(k, k) * decay, 0.0)
        rhs = v * beta + pltpu.roll(k * (beta * e_g[:, h:h + 1]), HEAD_DIM, axis=1)
        x = _dot3(_unit_lower_inverse(a_mat, r, cc), rhs)
        u_mat = jnp.where(low, x, 0.0)
        w_mat = jnp.where(low, pltpu.roll(x, HEAD_DIM, axis=1), 0.0)
        a_qk = _bdot_nt(q, k) * decay
        q_g = q * e_g[:, h:h + 1]
        k_g = k * e_gl[:, h:h + 1]
        s_h = s_ref[h]
        vn = u_mat - _bdot(w_mat, s_h)
        o = _bdot(q_g, s_h) + _bdot(a_qk, vn)
        s_ref[h] = s_h * e_last[:, h:h + 1] + _bdot(k_g.T, vn)
        zh = _head_block(z, 0, h, low)
        ms = jnp.sum(o * o, axis=1, keepdims=True) * (1.0 / HEAD_DIM)
        outs.append(o * lax.rsqrt(ms + EPS) * nw * _silu(zh))
    o_ref[...] = _join_heads(outs).astype(o_ref.dtype)

    @pl.when(c == pl.num_programs(1) - 1)
    def _():
        s_out_ref[0] = s_ref[:, 0:HEAD_DIM, 0:HEAD_DIM]


def _gdn_prompt(gqkv, z, small, conv_w, norm_w, nb, L):
    C = GDN_CHUNK
    nch = L // C
    W3 = 3 * GDN_WIDTH
    nw = jnp.concatenate([norm_w.astype(F32), jnp.zeros((LANES - HEAD_DIM,), F32)]).reshape(1, LANES)
    row = lambda w: pl.BlockSpec((C, w), lambda b, c: (b * nch + c, 0))
    return pl.pallas_call(
        _gdn_kernel,
        grid=(nb, nch),
        in_specs=[row(W3), row(GDN_WIDTH), row(SMALL_W),
                  pl.BlockSpec((CONV_WIDTH, W3), lambda b, c: (0, 0)),
                  pl.BlockSpec((1, LANES), lambda b, c: (0, 0))],
        out_specs=[row(GDN_WIDTH),
                   pl.BlockSpec((1, GDN_HEADS, HEAD_DIM, HEAD_DIM), lambda b, c: (b, 0, 0, 0))],
        out_shape=[jax.ShapeDtypeStruct((nb * L, GDN_WIDTH), BF16),
                   jax.ShapeDtypeStruct((nb, GDN_HEADS, HEAD_DIM, HEAD_DIM), F32)],
        scratch_shapes=[pltpu.VMEM((SUBLANES + C, W3), F32), pltpu.VMEM((GDN_HEADS, LANES, LANES), F32)],
        compiler_params=_params(("arbitrary", "arbitrary")),
        name="gdn_prompt",
    )(gqkv, z, small, conv_w, nw)


def _fox_kernel(qt_ref, kt_ref, q_ref, k_ref, v_ref, o_ref, m_ref, acc_ref):
    p = pl.program_id(2)
    qi = qt_ref[p]
    ki = kt_ref[p]
    tq = q_ref.shape[0]
    tk = k_ref.shape[0]

    @pl.when(ki == 0)
    def _():
        m_ref[...] = jnp.full_like(m_ref, NEG_INF)
        acc_ref[...] = jnp.zeros_like(acc_ref)

    s = _dot_nt(q_ref[...], k_ref[...])
    row = lax.broadcasted_iota(I32, (tq, tk), 0) + qi * tq
    col = lax.broadcasted_iota(I32, (tq, tk), 1) + ki * tk
    s = jnp.where(col <= row, s, NEG_INF)
    m_old = m_ref[...]
    m_new = jnp.maximum(m_old, jnp.max(s, axis=1, keepdims=True))
    alpha = jnp.exp(m_old - m_new)
    pmat = jnp.exp(s - m_new)
    acc_ref[...] = acc_ref[...] * alpha + _dot(pmat.astype(BF16), v_ref[...])
    m_ref[...] = m_new

    @pl.when(ki == qi)
    def _():
        acc = acc_ref[...]
        o_ref[...] = (acc / acc[:, HEAD_DIM:HEAD_DIM + 1]).astype(o_ref.dtype)


def _fox_prompt(qa, ka, va, nb, L, tq):
    nq = L // tq
    pairs = [(i, j) for i in range(nq) for j in range(i + 1)]
    qt = jnp.asarray([p[0] for p in pairs], I32)
    kt = jnp.asarray([p[1] for p in pairs], I32)
    qspec = pl.BlockSpec((tq, LANES), lambda b, h, p, qt, kt: (b * nq + qt[p], h))
    kspec = pl.BlockSpec((tq, LANES), lambda b, h, p, qt, kt: (b * nq + kt[p], h))
    return pl.pallas_call(
        _fox_kernel,
        grid_spec=pltpu.PrefetchScalarGridSpec(
            num_scalar_prefetch=2,
            grid=(nb, FOX_HEADS, len(pairs)),
            in_specs=[qspec, kspec, kspec],
            out_specs=qspec,
            scratch_shapes=[pltpu.VMEM((tq, 1), F32), pltpu.VMEM((tq, LANES), F32)]),
        out_shape=jax.ShapeDtypeStruct(qa.shape, BF16),
        compiler_params=_params(("arbitrary", "arbitrary", "arbitrary")),
        name="fox_prompt",
    )(qt, kt, qa, ka, va)


def _outproj_kernel(og_ref, of_ref, x_ref, g1_ref, sh_ref, sc_ref, wg_ref, wf_ref, wq_ref, x1_ref, h2_ref, qp_ref):
    m = _dot(og_ref[...], wg_ref[...]) + _dot(of_ref[...], wf_ref[...])
    x1 = x_ref[...] + g1_ref[0] * m
    x1_ref[...] = x1
    h2 = _rms(x1) * (1.0 + sc_ref[0]) + sh_ref[0]
    h2_ref[...] = h2
    qp_ref[...] = _dot(h2.astype(BF16), wq_ref[...]).astype(BF16)


def _prep_outproj(w_out, peer_wq):
    wg = w_out[:GDN_WIDTH].astype(BF16)
    wf = w_out[GDN_WIDTH:].reshape(FOX_HEADS, HEAD_DIM, -1)
    wf = jnp.pad(wf, ((0, 0), (0, LANES - HEAD_DIM), (0, 0))).reshape(FOX_HEADS * LANES, -1).astype(BF16)
    return wg, wf, peer_wq.astype(BF16)


def _outproj(og, of, x, g1, sh2, sc2, wg, wf, wq, tm, tiles_per_seq):
    T, D = x.shape
    rm = g1.shape[1]
    nq = wq.shape[1]
    row = lambda w: pl.BlockSpec((tm, w), lambda i: (i, 0))
    const = lambda a: pl.BlockSpec(a.shape, lambda i: (0,) * a.ndim)
    seq = pl.BlockSpec((1, rm, D), lambda i: (i // tiles_per_seq, 0, 0))
    return pl.pallas_call(
        _outproj_kernel,
        grid=(T // tm,),
        in_specs=[row(og.shape[1]), row(of.shape[1]), row(D), seq, seq, seq, const(wg), const(wf), const(wq)],
        out_specs=[row(D), row(D), row(nq)],
        out_shape=[jax.ShapeDtypeStruct((T, D), F32), jax.ShapeDtypeStruct((T, D), F32),
                   jax.ShapeDtypeStruct((T, nq), BF16)],
        compiler_params=_params(("arbitrary",), 48),
        name="outproj",
    )(og, of, x, g1, sh2, sc2, wg, wf, wq)


def _topk_rows(s, k):
    n = s.shape[0]
    iota_n = lax.broadcasted_iota(I32, s.shape, 0)
    vals, idxs = [], []
    for _ in range(k):
        m = jnp.max(s, axis=0, keepdims=True)
        idx = jnp.min(jnp.where(s == m, iota_n, n), axis=0, keepdims=True)
        vals.append(m)
        idxs.append(idx)
        s = jnp.where(iota_n == idx, NEG_INF, s)
    return jnp.concatenate(vals, axis=0), jnp.concatenate(idxs, axis=0)


def _pair_rows(a0, a1, op):
    k = PEER_TOPK
    rows = [op(a0[0:1], a1[0:SUBLANES]), op(a0[0:1], a1[SUBLANES:k])]
    rows += [op(a0[a:a + 1], a1[0:SUBLANES]) for a in range(1, SUBLANES)]
    rows.append(op(a0[SUBLANES:k], a1[0:1]))
    return jnp.concatenate(rows, axis=0)


def _topk_kernel(q_ref, sk_ref, e_ref, g_ref, sv_ref, si_ref):
    hp = pl.program_id(1)
    tl = q_ref.shape[0]
    k = PEER_TOPK
    s = _dot_nt(sk_ref[0], q_ref[...])
    vals, idxs = _topk_rows(s, k)
    sv_ref[hp] = vals
    si_ref[hp] = idxs

    @pl.when(hp == pl.num_programs(1) - 1)
    def _():
        nrow = 10 * SUBLANES
        ridx = lax.broadcasted_iota(I32, (nrow, tl), 0)
        blk = ridx // SUBLANES
        w = ridx % SUBLANES
        flat = jnp.where(blk == 0, w, jnp.where(blk == 1, SUBLANES + w,
                         jnp.where(blk <= SUBLANES, k * (blk - 1) + w, k * (SUBLANES + w))))
        e_all, g_all = [], []
        for h in range(PEER_HEADS):
            cand = _pair_rows(sv_ref[2 * h], sv_ref[2 * h + 1], lambda a, b: a + b)
            ecand = _pair_rows(si_ref[2 * h], si_ref[2 * h + 1], lambda a, b: a * N_KEYS + b)
            fvs, es = [], []
            for _ in range(k):
                m = jnp.max(cand, axis=0, keepdims=True)
                f = jnp.min(jnp.where(cand == m, flat, k * k), axis=0, keepdims=True)
                sel = flat == f
                es.append(jnp.max(jnp.where(sel, ecand, -1), axis=0, keepdims=True))
                fvs.append(m)
                cand = jnp.where(sel, NEG_INF, cand)
            fv = jnp.concatenate(fvs, axis=0)
            ex = jnp.exp(fv - fv[0:1])
            g_all.append(ex / jnp.sum(ex, axis=0, keepdims=True))
            e_all.append(jnp.concatenate(es, axis=0))
        e_mat = jnp.concatenate(e_all, axis=0).astype(F32)
        g_mat = jnp.concatenate(g_all, axis=0)
        for j in range(tl // LANES):
            e_ref[j * LANES:(j + 1) * LANES, :] = e_mat[:, j * LANES:(j + 1) * LANES].T.astype(I32)
            g_ref[j * LANES:(j + 1) * LANES, :] = g_mat[:, j * LANES:(j + 1) * LANES].T


def _topk(qp, subkeys, tl):
    T = qp.shape[0]
    nhp = subkeys.shape[0]
    ne = PEER_HEADS * PEER_TOPK
    out = pl.BlockSpec((tl, ne), lambda i, hp: (i, 0))
    return pl.pallas_call(
        _topk_kernel,
        grid=(T // tl, nhp),
        in_specs=[pl.BlockSpec((tl, PEER_DK_HALF), lambda i, hp: (i, hp)),
                  pl.BlockSpec((1, N_KEYS, PEER_DK_HALF), lambda i, hp: (hp, 0, 0))],
        out_specs=[out, out],
        out_shape=[jax.ShapeDtypeStruct((T, ne), I32), jax.ShapeDtypeStruct((T, ne), F32)],
        scratch_shapes=[pltpu.VMEM((nhp, PEER_TOPK, tl), F32), pltpu.VMEM((nhp, PEER_TOPK, tl), I32)],
        compiler_params=_params(("arbitrary", "arbitrary")),
        name="peer_topk",
    )(qp, subkeys)


HALF_EXPERTS_SHIFT = 13


def _pack_table(w):
    E, D = w.shape
    bits = lax.bitcast_convert_type(w.astype(BF16), jnp.uint16).astype(jnp.uint32)
    packed = (bits[:E // 2] << 16) | bits[E // 2:]
    return packed.reshape(E // 2, D // LANES, LANES)


def _table_row(tab_ref, e):
    w = tab_ref[e & ((1 << HALF_EXPERTS_SHIFT) - 1)]
    sh = ((e >> HALF_EXPERTS_SHIFT) * 16).astype(jnp.uint32)
    return pltpu.bitcast((w << sh) & jnp.uint32(0xFFFF0000), F32)


def _fold(x, y, d, low):
    return jnp.where(low, x, y) + pltpu.roll(jnp.where(low, y, x), SUBLANES - d, axis=0)


def _sublane_sums(vs, lows):
    for d, low in zip((1, 2, 4), lows):
        vs = [_fold(vs[2 * i], vs[2 * i + 1], d, low) for i in range(len(vs) // 2)]
    return vs[0]


def _peer_u_kernel(e_ref, x_ref, g_ref, tab_ref, act_ref):
    tb = x_ref.shape[0]
    ne = e_ref.shape[1]
    sub = lax.broadcasted_iota(I32, (SUBLANES, LANES), 0)
    lows = [(sub & d) == 0 for d in (1, 2, 4)]
    lane = lax.broadcasted_iota(I32, (ne, LANES), 1)

    def group(gi, carry):
        base = pl.multiple_of(gi * SUBLANES, SUBLANES)
        cols = jnp.zeros((ne, LANES), F32)
        for tt in range(SUBLANES):
            t = base + tt
            xv = x_ref[t]
            rows = []
            for j in range(ne // SUBLANES):
                prods = [_table_row(tab_ref, e_ref[t, j * SUBLANES + kk]) * xv for kk in range(SUBLANES)]
                rows.append(_sublane_sums(prods, lows))
            dots = jnp.sum(jnp.concatenate(rows, axis=0), axis=1, keepdims=True)
            cols = jnp.where(lane == tt, dots, cols)
        d8 = cols.T[0:SUBLANES, :]
        gelu = 0.5 * d8 * (1.0 + lax.erf(d8 * (2.0 ** -0.5)))
        act_ref[pl.ds(base, SUBLANES), :] = gelu * g_ref[pl.ds(base, SUBLANES), :]
        return carry

    lax.fori_loop(0, tb // SUBLANES, group, 0)


def _peer_u(eidx, x3, gate, tab, tb):
    T, ne = eidx.shape
    return pl.pallas_call(
        _peer_u_kernel,
        grid=(T // tb,),
        in_specs=[pl.BlockSpec((tb, ne), lambda i: (i, 0), memory_space=pltpu.SMEM),
                  pl.BlockSpec((tb,) + x3.shape[1:], lambda i: (i, 0, 0)),
                  pl.BlockSpec((tb, ne), lambda i: (i, 0)),
                  pl.BlockSpec(tab.shape, lambda i: (0, 0, 0), pipeline_mode=pl.Buffered(1))],
        out_specs=pl.BlockSpec((tb, ne), lambda i: (i, 0)),
        out_shape=jax.ShapeDtypeStruct((T, ne), F32),
        compiler_params=_params(("arbitrary",), 48),
        name="peer_u",
    )(eidx, x3, gate, tab)


def _peer_v_kernel(e_ref, a_ref, tab_ref, o_ref):
    tb = o_ref.shape[0]
    ne = e_ref.shape[1]
    nacc = 4

    def tok(t, carry):
        accs = [jnp.zeros(o_ref.shape[1:], F32) for _ in range(nacc)]
        for kk in range(ne):
            accs[kk % nacc] = accs[kk % nacc] + _table_row(tab_ref, e_ref[t, kk]) * a_ref[t, kk]
        o_ref[t] = (accs[0] + accs[1]) + (accs[2] + accs[3])
        return carry

    lax.fori_loop(0, tb, tok, 0)


def _peer_v(eidx, act, tab, tb):
    T, ne = eidx.shape
    smem = pl.BlockSpec((tb, ne), lambda i: (i, 0), memory_space=pltpu.SMEM)
    return pl.pallas_call(
        _peer_v_kernel,
        grid=(T // tb,),
        in_specs=[smem, smem, pl.BlockSpec(tab.shape, lambda i: (0, 0, 0), pipeline_mode=pl.Buffered(1))],
        out_specs=pl.BlockSpec((tb,) + tab.shape[1:], lambda i: (i, 0, 0)),
        out_shape=jax.ShapeDtypeStruct((T,) + tab.shape[1:], F32),
        compiler_params=_params(("arbitrary",), 48),
        name="peer_v",
    )(eidx, act, tab)


def _final_kernel(x1_ref, p_ref, g2_ref, fw_ref, y_ref):
    x2 = x1_ref[...] + g2_ref[0] * p_ref[...]
    y_ref[...] = _rms(x2) * fw_ref[...]


def _final(x1, peer, g2, fw, tm, tiles_per_seq):
    T, D = x1.shape
    rm = g2.shape[1]
    row = pl.BlockSpec((tm, D), lambda i: (i, 0))
    return pl.pallas_call(
        _final_kernel,
        grid=(T // tm,),
        in_specs=[row, row, pl.BlockSpec((1, rm, D), lambda i: (i // tiles_per_seq, 0, 0)),
                  pl.BlockSpec((1, D), lambda i: (0, 0))],
        out_specs=row,
        out_shape=jax.ShapeDtypeStruct((T, D), F32),
        compiler_params=_params(("arbitrary",)),
        name="final",
    )(x1, peer, g2, fw.reshape(1, D).astype(F32))


TILES = (256, 512, 256, 256, 64, 512)


def _post_mixers(og, of, x, g1, sh2, sc2, g2, wts, tm, rows_per_seq, tl, tb, tm_fin):
    wg, wf, wq, subkeys, tab_u, tab_v, fw = wts
    T, D = x.shape
    x1, h2, qp = _outproj(og, of, x, g1, sh2, sc2, wg, wf, wq, tm, rows_per_seq // tm)
    eidx, gate = _topk(qp, subkeys, tl)
    act = _peer_u(eidx, h2.reshape(T, D // LANES, LANES), gate, tab_u, tb)
    peer = _peer_v(eidx, act, tab_v, tb).reshape(T, D)
    return _final(x1, peer, g2, fw, tm_fin, rows_per_seq // tm_fin)


def _row_to_col(row, n):
    r = lax.broadcasted_iota(I32, (n, n), 0)
    c = lax.broadcasted_iota(I32, (n, n), 1)
    return jnp.sum(jnp.where(r == c, row, 0.0), axis=1, keepdims=True)


def _gdn_step_kernel(u_ref, z_ref, sm_ref, cs_ref, s0_ref, cw_ref, nw_ref, o_ref, cs_out_ref, s_out_ref):
    u = u_ref[0]
    st = cs_ref[0]
    cw = cw_ref[...]
    conv = cw[CONV_WIDTH - 1:CONV_WIDTH] * u
    for j in range(CONV_WIDTH - 1):
        conv = conv + cw[j:j + 1] * st[j:j + 1]
    cs_out_ref[0] = jnp.concatenate([st[1:CONV_WIDTH - 1], u], axis=0)
    a = _silu(conv)
    z = z_ref[0]
    small = sm_ref[0]
    low = lax.broadcasted_iota(I32, (1, LANES), 1) < HEAD_DIM
    nw = nw_ref[...][:, :HEAD_DIM]
    outs = []
    for h in range(GDN_HEADS):
        cq = _head_block(a, 0, h, low)[:, :HEAD_DIM]
        ck = _head_block(a, GDN_WIDTH, h, low)[:, :HEAD_DIM]
        v = _head_block(a, 2 * GDN_WIDTH, h, low)[:, :HEAD_DIM]
        q = cq * lax.rsqrt(jnp.sum(cq * cq, axis=1, keepdims=True) + 1e-6) * (HEAD_DIM ** -0.5)
        k = ck * lax.rsqrt(jnp.sum(ck * ck, axis=1, keepdims=True) + 1e-6)
        g = small[:, h:h + 1]
        beta = small[:, GDN_HEADS + h:GDN_HEADS + h + 1]
        eg = jnp.exp(g)
        s0 = s0_ref[0, h]
        w_col = _row_to_col(k * (beta * eg), HEAD_DIM)
        vn = v * beta - jnp.sum(w_col * s0, axis=0, keepdims=True)
        q_col = _row_to_col(q * eg, HEAD_DIM)
        o = jnp.sum(q_col * s0, axis=0, keepdims=True) + jnp.sum(q * k, axis=1, keepdims=True) * vn
        s_out_ref[0, h] = s0 * eg + _row_to_col(k, HEAD_DIM) * vn
        zh = _head_block(z, 0, h, low)[:, :HEAD_DIM]
        outs.append(_rms(o) * nw * _silu(zh))
    o_ref[0] = jnp.concatenate(outs, axis=1).astype(o_ref.dtype)


def _gdn_step(gqkv, z, small, state_conv, state_delta, conv_w, norm_w):
    nb = gqkv.shape[0]
    W3 = 3 * GDN_WIDTH
    nw = jnp.concatenate([norm_w.astype(F32), jnp.zeros((LANES - HEAD_DIM,), F32)]).reshape(1, LANES)
    per_b = lambda *shape: pl.BlockSpec((1,) + shape, lambda b: (b,) + (0,) * len(shape))
    return pl.pallas_call(
        _gdn_step_kernel,
        grid=(nb,),
        in_specs=[per_b(1, W3), per_b(1, GDN_WIDTH), per_b(1, SMALL_W), per_b(CONV_WIDTH - 1, W3),
                  per_b(GDN_HEADS, HEAD_DIM, HEAD_DIM),
                  pl.BlockSpec((CONV_WIDTH, W3), lambda b: (0, 0)), pl.BlockSpec((1, LANES), lambda b: (0, 0))],
        out_specs=[per_b(1, GDN_WIDTH), per_b(CONV_WIDTH - 1, W3), per_b(GDN_HEADS, HEAD_DIM, HEAD_DIM)],
        out_shape=[jax.ShapeDtypeStruct((nb, 1, GDN_WIDTH), BF16),
                   jax.ShapeDtypeStruct((nb, CONV_WIDTH - 1, W3), F32),
                   jax.ShapeDtypeStruct((nb, GDN_HEADS, HEAD_DIM, HEAD_DIM), F32)],
        compiler_params=_params(("arbitrary",)),
        name="gdn_step",
    )(gqkv.reshape(nb, 1, W3), z.reshape(nb, 1, GDN_WIDTH), small.reshape(nb, 1, SMALL_W), state_conv, state_delta,
      conv_w, nw)


def _fox_decode_kernel(pt_ref, q_ref, kn_ref, vn_ref, sm_ref, k_ref, v_ref, lf_ref, o_ref,
                       qbd_ref, m_ref, l_ref, acc_ref, carry_ref):
    p = pl.program_id(1)
    H = FOX_HEADS
    W = FOX_WIDTH
    head_of_lane = lax.broadcasted_iota(I32, (H, W), 1) // HEAD_DIM
    diag = head_of_lane == lax.broadcasted_iota(I32, (H, W), 0)

    @pl.when(p == 0)
    def _():
        qbd = jnp.where(diag, q_ref[0] * (HEAD_DIM ** -0.5), 0.0)
        qbd_ref[...] = qbd.astype(BF16)
        kn = kn_ref[0].astype(BF16).astype(F32)
        m_ref[...] = jnp.sum(qbd_ref[...].astype(F32) * kn, axis=1, keepdims=True)
        l_ref[...] = jnp.ones_like(l_ref)
        acc_ref[...] = jnp.broadcast_to(vn_ref[0].astype(BF16).astype(F32), (H, W))
        lane = lax.broadcasted_iota(I32, (H, SMALL_W), 1)
        row = lax.broadcasted_iota(I32, (H, SMALL_W), 0)
        carry_ref[...] = jnp.sum(jnp.where(lane == row + 2 * GDN_HEADS, sm_ref[0], 0.0), axis=1, keepdims=True)

    ps = k_ref.shape[1]
    lf = lf_ref[0]
    lf_t = jnp.concatenate([lf, jnp.zeros((ps, LANES - H), F32)], axis=1).T[0:H, :]
    j = lax.broadcasted_iota(I32, (ps, ps), 0)
    pos = lax.broadcasted_iota(I32, (ps, ps), 1)
    bias = _dot(lf_t, (j > pos).astype(F32), precision=HIGHEST) + carry_ref[...]
    carry_ref[...] = carry_ref[...] + jnp.sum(lf_t, axis=1, keepdims=True)
    s = _dot_nt(qbd_ref[...], k_ref[0].astype(BF16)) + bias
    m_old = m_ref[...]
    m_new = jnp.maximum(m_old, jnp.max(s, axis=1, keepdims=True))
    alpha = jnp.exp(m_old - m_new)
    pm = jnp.exp(s - m_new)
    l_ref[...] = l_ref[...] * alpha + jnp.sum(pm, axis=1, keepdims=True)
    acc_ref[...] = acc_ref[...] * alpha + _dot(pm.astype(BF16), v_ref[0].astype(BF16))
    m_ref[...] = m_new

    @pl.when(p == pl.num_programs(1) - 1)
    def _():
        out = jnp.where(diag, acc_ref[...] / l_ref[...], 0.0)
        o_ref[0] = _spread_heads(jnp.sum(out, axis=0, keepdims=True)).astype(o_ref.dtype)


def _fox_decode(page_table, fq, fk, fv, small, cache_k, cache_v, cache_logf):
    nb, npg = page_table.shape
    npool, ps = cache_k.shape[:2]
    W = FOX_WIDTH
    per_b = lambda w: pl.BlockSpec((1, 1, w), lambda b, p, pt: (b, 0, 0))
    page = lambda w: pl.BlockSpec((1, ps, w), lambda b, p, pt: (pt[b, npg - 1 - p], 0, 0))
    return pl.pallas_call(
        _fox_decode_kernel,
        grid_spec=pltpu.PrefetchScalarGridSpec(
            num_scalar_prefetch=1,
            grid=(nb, npg),
            in_specs=[per_b(W), per_b(W), per_b(W), per_b(SMALL_W), page(W), page(W), page(FOX_HEADS)],
            out_specs=per_b(FOX_HEADS * LANES),
            scratch_shapes=[pltpu.VMEM((FOX_HEADS, W), BF16), pltpu.VMEM((FOX_HEADS, 1), F32),
                            pltpu.VMEM((FOX_HEADS, 1), F32), pltpu.VMEM((FOX_HEADS, W), F32),
                            pltpu.VMEM((FOX_HEADS, 1), F32)]),
        out_shape=jax.ShapeDtypeStruct((nb, 1, FOX_HEADS * LANES), BF16),
        compiler_params=_params(("arbitrary", "arbitrary")),
        name="fox_decode",
    )(page_table, fq.reshape(nb, 1, W), fk.reshape(nb, 1, W), fv.reshape(nb, 1, W), small.reshape(nb, 1, SMALL_W),
      cache_k.reshape(npool, ps, W), cache_v.reshape(npool, ps, W), cache_logf)


def kernel(x_prompt, x_sample, cache_k, cache_v, cache_logf, state_conv, state_delta, page_table, c_prompt, c_sample, w_mod, b_mod, w_in, conv_w, a_log, dt_bias, gdn_norm_w, fox_fb, w_out, peer_wq, peer_subkeys, peer_u, peer_v, final_norm_w):
    assert w_mod.shape[0] == 1, "one layer"
    D = x_prompt.shape[-1]
    Bp, L = x_prompt.shape[:2]
    Bs = x_sample.shape[0]
    assert x_sample.shape[1] == 1 and L % GDN_CHUNK == 0 and Bs % SUBLANES == 0
    tm_in, tq, tm_out, tl, tb, tm_fin = (min(t, L) for t in TILES)

    c = jnp.concatenate([c_prompt, c_sample], axis=0)
    mod = _mod(c, w_mod[0], b_mod[0])
    mod_p = [mod[:Bp, None, j * D:(j + 1) * D] for j in range(N_MOD)]
    pad_s = (-Bs) % LANES
    mod_s = [mod[None, Bp:, j * D:(j + 1) * D] for j in range(N_MOD)]
    mod_s_pad = [jnp.pad(m, ((0, 0), (0, pad_s), (0, 0))) for m in mod_s]

    w_cat, padd, alog = _prep_inproj(w_in[0], a_log[0], dt_bias[0], fox_fb[0])
    wg, wf, wq = _prep_outproj(w_out[0], peer_wq[0])
    subkeys = peer_subkeys[0].reshape(PEER_HEADS * 2, N_KEYS, PEER_DK_HALF).astype(BF16)
    wts = (wg, wf, wq, subkeys, _pack_table(peer_u[0]), _pack_table(peer_v[0]), final_norm_w)

    xp = x_prompt.reshape(Bp * L, D)
    gqkv, z, small, fk, fv, qa, ka, va = _inproj(xp, mod_p[0], mod_p[1], w_cat, padd, alog, tm_in, L // tm_in)
    og, delta_p = _gdn_prompt(gqkv, z, small, conv_w[0], gdn_norm_w[0], Bp, L)
    of = _fox_prompt(qa, ka, va, Bp, L, tq)
    y_p = _post_mixers(og, of, xp, mod_p[2], mod_p[3], mod_p[4], mod_p[5], wts, tm_out, L, tl, tb, tm_fin)

    xs = x_sample.reshape(Bs, D)
    gqkv_s, z_s, small_s, fk_s, fv_s, _, _, _, fq_s = _inproj(xs, mod_s[0], mod_s[1], w_cat, padd, alog, Bs, 1,
                                                              with_fq=True)
    og_s, conv_s, delta_s = _gdn_step(gqkv_s, z_s, small_s, state_conv[0], state_delta[0], conv_w[0], gdn_norm_w[0])
    of_s = _fox_decode(page_table, fq_s, fk_s, fv_s, small_s, cache_k[0], cache_v[0], cache_logf[0])
    rows = lambda a: jnp.pad(a.reshape(Bs, -1), ((0, pad_s), (0, 0)))
    ts = Bs + pad_s
    y_s = _post_mixers(rows(og_s), rows(of_s), rows(xs), mod_s_pad[2], mod_s_pad[3], mod_s_pad[4], mod_s_pad[5], wts,
                       ts, ts, ts, min(tb, ts), ts)[:Bs]

    nf = 2 * GDN_HEADS
    return (y_p.reshape(Bp, L, D), y_s.reshape(Bs, 1, D),
            fk.reshape(1, Bp, L, FOX_HEADS, HEAD_DIM), fv.reshape(1, Bp, L, FOX_HEADS, HEAD_DIM),
            small[:, nf:nf + FOX_HEADS].reshape(1, Bp, L, FOX_HEADS),
            gqkv.reshape(Bp, L, -1)[None, :, L - (CONV_WIDTH - 1):], delta_p[None],
            fk_s.reshape(1, Bs, 1, FOX_HEADS, HEAD_DIM), fv_s.reshape(1, Bs, 1, FOX_HEADS, HEAD_DIM),
            small_s[:, nf:nf + FOX_HEADS].reshape(1, Bs, 1, FOX_HEADS), conv_s[None], delta_s[None])
```

```python
import functools

import numpy as np
import jax
import jax.numpy as jnp
from jax import lax
from jax.experimental import pallas as pl
from jax.experimental.pallas import tpu as pltpu

F32 = jnp.float32
BF16 = jnp.bfloat16
I32 = jnp.int32

HEAD_DIM = 64
GDN_HEADS = 8
FOX_HEADS = 8
GDN_WIDTH = GDN_HEADS * HEAD_DIM
FOX_WIDTH = FOX_HEADS * HEAD_DIM
CONV_WIDTH = 4
PEER_HEADS = 8
N_KEYS = 128
PEER_TOPK = 16
PEER_DK_HALF = 128
N_MOD = 6
EPS = 1e-6
LANES = 128
SUBLANES = 8
GDN_CHUNK = 128
SMALL_W = LANES
BF16_TILE_ROWS = 16
TABLE_ROWS_LOG2 = 13
TABLE_ROWS = 1 << TABLE_ROWS_LOG2
TILE_ROWS = BF16_TILE_ROWS
EXPERTS_PER_DOT = 16
U_GROUPS_PER_ITER = 2
PAGES_PER_STEP = 8
TOPK_SETS_PER_STEP = 4
FOX_ROW_SPLIT = 4
HIGHEST = lax.Precision.HIGHEST
NEG_INF = float("-inf")


def _dot(a, b, precision=None):
    return jnp.dot(a, b, preferred_element_type=F32, precision=precision)


def _dot_nt(a, b, precision=None):
    return lax.dot_general(a, b, (((1,), (1,)), ((), ())), preferred_element_type=F32, precision=precision)


def _bdot(a, b):
    return _dot(a.astype(BF16), b.astype(BF16))


def _bdot_nt(a, b):
    return _dot_nt(a.astype(BF16), b.astype(BF16))


def _split3(x):
    hi = x.astype(BF16)
    r1 = x - hi.astype(F32)
    mid = r1.astype(BF16)
    lo = (r1 - mid.astype(F32)).astype(BF16)
    return hi, mid, lo


def _dot3(a, b):
    ah = a.astype(BF16)
    al = (a - ah.astype(F32)).astype(BF16)
    bh = b.astype(BF16)
    bl = (b - bh.astype(F32)).astype(BF16)
    return _dot(ah, bh) + (_dot(ah, bl) + _dot(al, bh))


def _params(sem, vmem_mb=None):
    kw = dict(dimension_semantics=sem)
    if vmem_mb is not None:
        kw["vmem_limit_bytes"] = vmem_mb * 1024 * 1024
    return pltpu.CompilerParams(**kw)


def _rms(x):
    return x * lax.rsqrt(jnp.mean(x * x, axis=-1, keepdims=True) + EPS)


def _silu(x):
    return x * jax.nn.sigmoid(x)


def _mod_kernel(c_ref, w_ref, b_ref, o_ref):
    s = _silu(c_ref[...])
    o_ref[...] = _bdot(s, w_ref[...]) + b_ref[...]


def _mod(c, w, b):
    n, d = c.shape
    nout = w.shape[1]
    tn = 1024
    return pl.pallas_call(
        _mod_kernel,
        grid=(nout // tn,),
        in_specs=[pl.BlockSpec((n, d), lambda j: (0, 0)),
                  pl.BlockSpec((d, tn), lambda j: (0, j)),
                  pl.BlockSpec((1, tn), lambda j: (0, j))],
        out_specs=pl.BlockSpec((n, tn), lambda j: (0, j)),
        out_shape=jax.ShapeDtypeStruct((n, nout), F32),
        compiler_params=_params(("arbitrary",)),
        name="mod",
    )(c, w, b.reshape(1, nout))


def _spread_heads(a):
    tm = a.shape[0]
    lane = lax.broadcasted_iota(I32, (tm, LANES), 1)
    low = lane < HEAD_DIM
    out = []
    for j in range(a.shape[1] // LANES):
        blk = a[:, j * LANES:(j + 1) * LANES]
        out.append(jnp.where(low, blk, 0.0))
        out.append(jnp.where(low, pltpu.roll(blk, HEAD_DIM, axis=1), 0.0))
    return jnp.concatenate(out, axis=1)


def _inproj_kernel(tiles_per_seq, x_ref, sh_ref, sc_ref, w_ref, padd_ref, alog_ref, selq_ref, selk_ref, cq_ref,
                   ck_ref, cv_ref, gqkv_ref, z_ref, small_ref, fk_ref, fv_ref, qa_ref, ka_ref, va_ref, *rest):
    carry_ref = rest[-1]
    i = pl.program_id(0)
    tm = x_ref.shape[0]
    h = _rms(x_ref[...]) * (1.0 + sc_ref[0]) + sh_ref[0]
    hb = h.astype(BF16)
    o = 0
    gqkv_ref[...] = _dot(hb, w_ref[:, o:o + 3 * GDN_WIDTH]); o += 3 * GDN_WIDTH
    z_ref[...] = _dot(hb, w_ref[:, o:o + GDN_WIDTH]); o += GDN_WIDTH
    sm = _dot(hb, w_ref[:, o:o + SMALL_W]); o += SMALL_W
    fq = _dot(hb, w_ref[:, o:o + FOX_WIDTH]); o += FOX_WIDTH
    fk = _dot(hb, w_ref[:, o:o + FOX_WIDTH]); o += FOX_WIDTH
    fv = _dot(hb, w_ref[:, o:o + FOX_WIDTH])
    fk_ref[...] = fk
    fv_ref[...] = fv
    if len(rest) == 2:
        rest[0][...] = fq

    lane = lax.broadcasted_iota(I32, (tm, SMALL_W), 1)
    y = sm + padd_ref[...]
    t = jnp.log1p(jnp.exp(-jnp.abs(y)))
    softplus = jnp.maximum(y, 0.0) + t
    logsig = jnp.minimum(y, 0.0) - t
    small = jnp.where(lane < GDN_HEADS, -jnp.exp(alog_ref[...]) * softplus,
                      jnp.where(lane < 2 * GDN_HEADS, jax.nn.sigmoid(sm),
                                jnp.where(lane < 2 * GDN_HEADS + FOX_HEADS, logsig, 0.0)))
    small_ref[...] = small

    @pl.when(i % tiles_per_seq == 0)
    def _():
        carry_ref[...] = jnp.zeros_like(carry_ref)

    r = lax.broadcasted_iota(I32, (tm, tm), 0)
    c = lax.broadcasted_iota(I32, (tm, tm), 1)
    ltri = (c <= r).astype(F32)
    cum = _dot(ltri, small, precision=HIGHEST) + carry_ref[...]
    carry_ref[...] = cum[tm - 1:tm, :]
    hi, mid, lo = _split3(cum)
    parts = jnp.concatenate([hi, mid, lo], axis=1)
    qa_ref[...] = (_spread_heads(fq * (HEAD_DIM ** -0.5)) + _dot(parts, selq_ref[...]) + cq_ref[...]).astype(BF16)
    ka_ref[...] = (_spread_heads(fk) + _dot(parts, selk_ref[...]) + ck_ref[...]).astype(BF16)
    va_ref[...] = (_spread_heads(fv) + cv_ref[...]).astype(BF16)


def _inproj_consts():
    selq = np.zeros((3 * SMALL_W, FOX_HEADS * LANES), np.float32)
    selk = np.zeros((3 * SMALL_W, FOX_HEADS * LANES), np.float32)
    cq = np.zeros((1, FOX_HEADS * LANES), np.float32)
    ck = np.zeros((1, FOX_HEADS * LANES), np.float32)
    cv = np.zeros((1, FOX_HEADS * LANES), np.float32)
    for h in range(FOX_HEADS):
        base = h * LANES + HEAD_DIM
        for p in range(3):
            src = p * SMALL_W + 2 * GDN_HEADS + h
            selq[src, base + p] = 1.0
            ck[0, base + p] = 1.0
            selk[src, base + 3 + p] = -1.0
            cq[0, base + 3 + p] = 1.0
        cv[0, base] = 1.0
    return (jnp.asarray(selq, BF16), jnp.asarray(selk, BF16), jnp.asarray(cq), jnp.asarray(ck), jnp.asarray(cv))


def _prep_inproj(w_in, a_log, dt_bias, fox_fb):
    o = np.cumsum((0, GDN_WIDTH, GDN_WIDTH, GDN_WIDTH, GDN_WIDTH, GDN_HEADS, GDN_HEADS, FOX_WIDTH, FOX_WIDTH,
                   FOX_WIDTH, FOX_HEADS)).tolist()
    d = w_in.shape[0]
    nsmall = 2 * GDN_HEADS + FOX_HEADS
    small = jnp.concatenate([w_in[:, o[4]:o[6]], w_in[:, o[9]:o[10]], jnp.zeros((d, SMALL_W - nsmall), w_in.dtype)], 1)
    w_cat = jnp.concatenate([w_in[:, :o[4]], small, w_in[:, o[6]:o[9]]], axis=1).astype(BF16)
    zero = jnp.zeros((GDN_HEADS,), F32)
    tail = jnp.zeros((SMALL_W - nsmall,), F32)
    padd = jnp.concatenate([dt_bias.astype(F32), zero, fox_fb.astype(F32), tail]).reshape(1, SMALL_W)
    alog = jnp.concatenate([a_log.astype(F32), zero, zero, tail]).reshape(1, SMALL_W)
    return w_cat, padd, alog


def _inproj(x, sh, sc, w_cat, padd, pmul, tm, tiles_per_seq, with_fq=False):
    T, D = x.shape
    rm = sh.shape[1]
    nw = w_cat.shape[1]
    selq, selk, cq, ck, cv = _inproj_consts()
    aw = FOX_HEADS * LANES
    row = lambda w: pl.BlockSpec((tm, w), lambda i: (i, 0))
    const = lambda a: pl.BlockSpec(a.shape, lambda i: (0,) * a.ndim)
    seq = pl.BlockSpec((1, rm, D), lambda i: (i // tiles_per_seq, 0, 0))
    outs = [(3 * GDN_WIDTH, F32), (GDN_WIDTH, F32), (SMALL_W, F32), (FOX_WIDTH, F32), (FOX_WIDTH, F32),
            (aw, BF16), (aw, BF16), (aw, BF16)]
    if with_fq:
        outs.append((FOX_WIDTH, F32))
    return pl.pallas_call(
        functools.partial(_inproj_kernel, tiles_per_seq),
        grid=(T // tm,),
        in_specs=[row(D), seq, seq, const(w_cat), const(padd), const(pmul), const(selq), const(selk), const(cq),
                  const(ck), const(cv)],
        out_specs=[row(w) for w, _ in outs],
        out_shape=[jax.ShapeDtypeStruct((T, w), dt) for w, dt in outs],
        scratch_shapes=[pltpu.VMEM((1, SMALL_W), F32)],
        compiler_params=_params(("arbitrary",), 48),
        name="inproj",
    )(x, sh, sc, w_cat, padd, pmul, selq, selk, cq, ck, cv)


def _head_block(a, base, h, low):
    j, odd = divmod(h, 2)
    blk = a[:, base + j * LANES:base + (j + 1) * LANES]
    if odd:
        blk = pltpu.roll(blk, HEAD_DIM, axis=1)
    return jnp.where(low, blk, 0.0)


def _join_heads(heads):
    return jnp.concatenate([heads[2 * j] + pltpu.roll(heads[2 * j + 1], HEAD_DIM, axis=1)
                            for j in range(len(heads) // 2)], axis=1)


def _unit_lower_inverses(a_mats, r, cc):
    n = a_mats[0].shape[0]
    s = SUBLANES
    same = (r // s) == (cc // s)
    eye = jnp.where(r == cc, 1.0, 0.0)
    bs = [jnp.where(same, -a, 0.0) for a in a_mats]
    ts = [eye + b for b in bs]
    b2s = [_dot3(b, b) for b in bs]
    ts = [t + _dot3(t, b2) for t, b2 in zip(ts, b2s)]
    b4s = [_dot3(b2, b2) for b2 in b2s]
    ts = [t + _dot3(t, b4) for t, b4 in zip(ts, b4s)]
    while s < n:
        same2 = (r // (2 * s)) == (cc // (2 * s))
        new = same2 & jnp.logical_not(same)
        xs = [_dot3(t, jnp.where(new, a, 0.0)) for t, a in zip(ts, a_mats)]
        ts = [t - _dot3(x, t) for t, x in zip(ts, xs)]
        same = same2
        s *= 2
    return ts


def _gdn_kernel(u_ref, z_ref, sm_ref, cw_ref, nw_ref, o_ref, s_out_ref, ubuf, s_ref):
    c = pl.program_id(1)
    C = u_ref.shape[0]
    W3 = 3 * GDN_WIDTH

    @pl.when(c == 0)
    def _():
        ubuf[0:SUBLANES, :] = jnp.zeros((SUBLANES, W3), F32)
        s_ref[...] = jnp.zeros_like(s_ref)

    ubuf[SUBLANES:SUBLANES + C, :] = u_ref[...]
    cw = cw_ref[...]
    conv = cw[CONV_WIDTH - 1:CONV_WIDTH] * ubuf[SUBLANES:SUBLANES + C, :]
    for j in range(1, CONV_WIDTH):
        conv = conv + cw[CONV_WIDTH - 1 - j:CONV_WIDTH - j] * ubuf[SUBLANES - j:SUBLANES - j + C, :]
    ubuf[0:SUBLANES, :] = ubuf[C:C + SUBLANES, :]
    a = _silu(conv)
    z = z_ref[...]
    small = sm_ref[...]

    lane = lax.broadcasted_iota(I32, (C, LANES), 1)
    low = lane < HEAD_DIM
    r = lax.broadcasted_iota(I32, (C, C), 0)
    cc = lax.broadcasted_iota(I32, (C, C), 1)
    causal = cc <= r
    strict = cc < r
    g_cum = _dot(causal.astype(F32), small, precision=HIGHEST)
    g_last = g_cum[C - 1:C, :]
    e_g = jnp.exp(g_cum)
    e_gl = jnp.exp(g_last - g_cum)
    e_last = jnp.exp(g_last)
    neg_g_hi = pltpu.roll(-g_cum, HEAD_DIM, axis=1)
    nw = nw_ref[...]
    heads = range(GDN_HEADS)
    col = lambda m, h: m[:, h:h + 1]
    cqs = [_head_block(a, 0, h, low) for h in heads]
    cks = [_head_block(a, GDN_WIDTH, h, low) for h in heads]
    vs = [_head_block(a, 2 * GDN_WIDTH, h, low) for h in heads]
    qs = [cq * lax.rsqrt(jnp.sum(cq * cq, axis=1, keepdims=True) + 1e-6) * (HEAD_DIM ** -0.5) for cq in cqs]
    ks = [ck * lax.rsqrt(jnp.sum(ck * ck, axis=1, keepdims=True) + 1e-6) for ck in cks]
    betas = [col(small, GDN_HEADS + h) for h in heads]
    g_diffs = [_dot_nt(jnp.where(lane == h, g_cum, jnp.where(lane == h + HEAD_DIM, 1.0, 0.0)),
                       jnp.where(lane == h, 1.0, jnp.where(lane == h + HEAD_DIM, neg_g_hi, 0.0)),
                       precision=HIGHEST) for h in heads]
    decays = [jnp.exp(jnp.where(causal, g, NEG_INF)) for g in g_diffs]
    a_mats = [jnp.where(strict, betas[h] * _bdot_nt(ks[h], ks[h]) * decays[h], 0.0) for h in heads]
    rhss = [vs[h] * betas[h] + pltpu.roll(ks[h] * (betas[h] * col(e_g, h)), HEAD_DIM, axis=1) for h in heads]
    invs = _unit_lower_inverses(a_mats, r, cc)
    xs = [_dot3(invs[h], rhss[h]) for h in heads]
    u_mats = [jnp.where(low, x, 0.0) for x in xs]
    w_mats = [jnp.where(low, pltpu.roll(x, HEAD_DIM, axis=1), 0.0) for x in xs]
    a_qks = [_bdot_nt(qs[h], ks[h]) * decays[h] for h in heads]
    s_olds = [s_ref[h] for h in heads]
    vns = [u_mats[h] - _bdot(w_mats[h], s_olds[h]) for h in heads]
    os_ = [_bdot(qs[h] * col(e_g, h), s_olds[h]) + _bdot(a_qks[h], vns[h]) for h in heads]
    for h in heads:
        s_ref[h] = s_olds[h] * col(e_last, h) + _bdot((ks[h] * col(e_gl, h)).T, vns[h])
    outs = []
    for h in heads:
        o = os_[h]
        ms = jnp.sum(o * o, axis=1, keepdims=True) * (1.0 / HEAD_DIM)
        outs.append(o * lax.rsqrt(ms + EPS) * nw * _silu(_head_block(z, 0, h, low)))
    o_ref[...] = _join_heads(outs).astype(o_ref.dtype)

    @pl.when(c == pl.num_programs(1) - 1)
    def _():
        s_out_ref[0] = s_ref[:, 0:HEAD_DIM, 0:HEAD_DIM]


def _gdn_prompt(gqkv, z, small, conv_w, norm_w, nb, L):
    C = GDN_CHUNK
    nch = L // C
    W3 = 3 * GDN_WIDTH
    nw = jnp.concatenate([norm_w.astype(F32), jnp.zeros((LANES - HEAD_DIM,), F32)]).reshape(1, LANES)
    row = lambda w: pl.BlockSpec((C, w), lambda b, c: (b * nch + c, 0))
    return pl.pallas_call(
        _gdn_kernel,
        grid=(nb, nch),
        in_specs=[row(W3), row(GDN_WIDTH), row(SMALL_W),
                  pl.BlockSpec((CONV_WIDTH, W3), lambda b, c: (0, 0)),
                  pl.BlockSpec((1, LANES), lambda b, c: (0, 0))],
        out_specs=[row(GDN_WIDTH),
                   pl.BlockSpec((1, GDN_HEADS, HEAD_DIM, HEAD_DIM), lambda b, c: (b, 0, 0, 0))],
        out_shape=[jax.ShapeDtypeStruct((nb * L, GDN_WIDTH), BF16),
                   jax.ShapeDtypeStruct((nb, GDN_HEADS, HEAD_DIM, HEAD_DIM), F32)],
        scratch_shapes=[pltpu.VMEM((SUBLANES + C, W3), F32), pltpu.VMEM((GDN_HEADS, LANES, LANES), F32)],
        compiler_params=_params(("arbitrary", "arbitrary")),
        name="gdn_prompt",
    )(gqkv, z, small, conv_w, nw)


def _fox_kernel(qt_ref, kt_ref, q_ref, k_ref, v_ref, o_ref, m_ref, acc_ref):
    p = pl.program_id(2)
    qi = qt_ref[p]
    ki = kt_ref[p]
    tq = q_ref.shape[0]
    tk = k_ref.shape[0]

    @pl.when(ki == 0)
    def _():
        m_ref[...] = jnp.full_like(m_ref, NEG_INF)
        acc_ref[...] = jnp.zeros_like(acc_ref)

    def step(on_diagonal):
        tr = tq // FOX_ROW_SPLIT
        blocks = [slice(i * tr, (i + 1) * tr) for i in range(FOX_ROW_SPLIT)]
        k = k_ref[...]
        v = v_ref[...]
        ss = [_dot_nt(q_ref[b, :], k) for b in blocks]
        if on_diagonal:
            row = lax.broadcasted_iota(I32, (tr, tk), 0)
            col = lax.broadcasted_iota(I32, (tr, tk), 1)
            ss = [jnp.where(col <= row + i * tr, s, NEG_INF) for i, s in enumerate(ss)]
        m_olds = [m_ref[b, :] for b in blocks]
        m_news = [jnp.maximum(m, jnp.max(s, axis=1, keepdims=True)) for m, s in zip(m_olds, ss)]
        ps = [jnp.exp(s - m).astype(BF16) for s, m in zip(ss, m_news)]
        for b, m_old, m_new, p in zip(blocks, m_olds, m_news, ps):
            acc_ref[b, :] = acc_ref[b, :] * jnp.exp(m_old - m_new) + _dot(p, v)
            m_ref[b, :] = m_new

    @pl.when(ki < qi)
    def _():
        step(False)

    @pl.when(ki == qi)
    def _():
        step(True)
        acc = acc_ref[...]
        o_ref[...] = (acc / acc[:, HEAD_DIM:HEAD_DIM + 1]).astype(o_ref.dtype)


def _fox_prompt(qa, ka, va, nb, L, tq):
    nq = L // tq
    pairs = [(i, j) for i in range(nq) for j in range(i + 1)]
    qt = jnp.asarray([p[0] for p in pairs], I32)
    kt = jnp.asarray([p[1] for p in pairs], I32)
    qspec = pl.BlockSpec((tq, LANES), lambda b, h, p, qt, kt: (b * nq + qt[p], h))
    kspec = pl.BlockSpec((tq, LANES), lambda b, h, p, qt, kt: (b * nq + kt[p], h))
    return pl.pallas_call(
        _fox_kernel,
        grid_spec=pltpu.PrefetchScalarGridSpec(
            num_scalar_prefetch=2,
            grid=(nb, FOX_HEADS, len(pairs)),
            in_specs=[qspec, kspec, kspec],
            out_specs=qspec,
            scratch_shapes=[pltpu.VMEM((tq, 1), F32), pltpu.VMEM((tq, LANES), F32)]),
        out_shape=jax.ShapeDtypeStruct(qa.shape, BF16),
        compiler_params=_params(("arbitrary", "arbitrary", "arbitrary")),
        name="fox_prompt",
    )(qt, kt, qa, ka, va)


def _outproj_kernel(og_ref, of_ref, x_ref, g1_ref, sh_ref, sc_ref, wg_ref, wf_ref, wq_ref, x1_ref, h2_ref, qp_ref):
    m = _dot(og_ref[...], wg_ref[...]) + _dot(of_ref[...], wf_ref[...])
    x1 = x_ref[...] + g1_ref[0] * m
    x1_ref[...] = x1
    h2 = _rms(x1) * (1.0 + sc_ref[0]) + sh_ref[0]
    h2_ref[...] = h2
    qp_ref[...] = _dot(h2.astype(BF16), wq_ref[...]).astype(BF16)


def _prep_outproj(w_out, peer_wq):
    wg = w_out[:GDN_WIDTH].astype(BF16)
    wf = w_out[GDN_WIDTH:].reshape(FOX_HEADS, HEAD_DIM, -1)
    wf = jnp.pad(wf, ((0, 0), (0, LANES - HEAD_DIM), (0, 0))).reshape(FOX_HEADS * LANES, -1).astype(BF16)
    return wg, wf, peer_wq.astype(BF16)


def _outproj(og, of, x, g1, sh2, sc2, wg, wf, wq, tm, tiles_per_seq):
    T, D = x.shape
    rm = g1.shape[1]
    nq = wq.shape[1]
    row = lambda w: pl.BlockSpec((tm, w), lambda i: (i, 0))
    const = lambda a: pl.BlockSpec(a.shape, lambda i: (0,) * a.ndim)
    seq = pl.BlockSpec((1, rm, D), lambda i: (i // tiles_per_seq, 0, 0))
    return pl.pallas_call(
        _outproj_kernel,
        grid=(T // tm,),
        in_specs=[row(og.shape[1]), row(of.shape[1]), row(D), seq, seq, seq, const(wg), const(wf), const(wq)],
        out_specs=[row(D), row(D), row(nq)],
        out_shape=[jax.ShapeDtypeStruct((T, D), F32), jax.ShapeDtypeStruct((T, D), F32),
                   jax.ShapeDtypeStruct((T, nq), BF16)],
        compiler_params=_params(("arbitrary",), 48),
        name="outproj",
    )(og, of, x, g1, sh2, sc2, wg, wf, wq)


def _topk_rows(ss, k):
    n = ss[0].shape[0]
    iota_n = lax.broadcasted_iota(I32, ss[0].shape, 0)
    vals = [[] for _ in ss]
    idxs = [[] for _ in ss]
    for _ in range(k):
        ms = [jnp.max(s, axis=0, keepdims=True) for s in ss]
        ids = [jnp.min(jnp.where(s == m, iota_n, n), axis=0, keepdims=True) for s, m in zip(ss, ms)]
        ss = [jnp.where(iota_n == i, NEG_INF, s) for s, i in zip(ss, ids)]
        for j, (m, i) in enumerate(zip(ms, ids)):
            vals[j].append(m)
            idxs[j].append(i)
    return [jnp.concatenate(v, axis=0) for v in vals], [jnp.concatenate(i, axis=0) for i in idxs]


def _pair_rows(a0, a1, op):
    k = PEER_TOPK
    rows = [op(a0[0:1], a1[0:SUBLANES]), op(a0[0:1], a1[SUBLANES:k])]
    rows += [op(a0[a:a + 1], a1[0:SUBLANES]) for a in range(1, SUBLANES)]
    rows.append(op(a0[SUBLANES:k], a1[0:1]))
    return jnp.concatenate(rows, axis=0)


def _topk_kernel(q_ref, sk_ref, r_ref, gx_ref, sv_ref, si_ref):
    hp = pl.program_id(1)
    tl = q_ref.shape[0]
    k = PEER_TOPK
    nsub = sk_ref.shape[0]
    dk = sk_ref.shape[2]
    ss = [_dot_nt(sk_ref[i], q_ref[:, i * dk:(i + 1) * dk]) for i in range(nsub)]
    vals, idxs = _topk_rows(ss, k)
    for i in range(nsub):
        sv_ref[hp * nsub + i] = vals[i]
        si_ref[hp * nsub + i] = idxs[i]

    @pl.when(hp == pl.num_programs(1) - 1)
    def _():
        nrow = 10 * SUBLANES
        ridx = lax.broadcasted_iota(I32, (nrow, tl), 0)
        blk = ridx // SUBLANES
        w = ridx % SUBLANES
        flat = jnp.where(blk == 0, w, jnp.where(blk == 1, SUBLANES + w,
                         jnp.where(blk <= SUBLANES, k * (blk - 1) + w, k * (SUBLANES + w))))
        heads = range(PEER_HEADS)
        cands = [_pair_rows(sv_ref[2 * h], sv_ref[2 * h + 1], lambda a, b: a + b) for h in heads]
        ecands = [_pair_rows(si_ref[2 * h], si_ref[2 * h + 1], lambda a, b: a * N_KEYS + b) for h in heads]
        fvs = [[] for _ in heads]
        es = [[] for _ in heads]
        for _ in range(k):
            ms = [jnp.max(c, axis=0, keepdims=True) for c in cands]
            fs = [jnp.min(jnp.where(c == m, flat, k * k), axis=0, keepdims=True) for c, m in zip(cands, ms)]
            sels = [flat == f for f in fs]
            for h in heads:
                es[h].append(jnp.max(jnp.where(sels[h], ecands[h], -1), axis=0, keepdims=True))
                fvs[h].append(ms[h])
            cands = [jnp.where(sel, NEG_INF, c) for sel, c in zip(sels, cands)]
        g_all = []
        for h in heads:
            fv = jnp.concatenate(fvs[h], axis=0)
            ex = jnp.exp(fv - fv[0:1])
            g_all.append(ex / jnp.sum(ex, axis=0, keepdims=True))
        e_mat = jnp.concatenate([jnp.concatenate(e, axis=0) for e in es], axis=0)
        g_mat = jnp.concatenate(g_all, axis=0)
        row_mat = (e_mat & (TABLE_ROWS - 1)).astype(F32)
        half_mat = e_mat >> TABLE_ROWS_LOG2
        for j in range(tl // LANES):
            tok = slice(j * LANES, (j + 1) * LANES)
            r_ref[tok, :] = row_mat[:, tok].T.astype(I32)
        par = lax.broadcasted_iota(I32, (TILE_ROWS, tl), 0) % 2
        for i in range(PEER_HEADS * k // SUBLANES):
            rows = [jnp.where(par == half_mat[n:n + 1], g_mat[n:n + 1], 0.0)
                    for n in range(i * SUBLANES, (i + 1) * SUBLANES)]
            blk = jnp.concatenate(rows, axis=0)
            for j in range(tl // LANES):
                tok = slice(j * LANES, (j + 1) * LANES)
                gx_ref[tok, i * LANES:(i + 1) * LANES] = blk[:, tok].T


def _topk(qp, subkeys, tl):
    T = qp.shape[0]
    nhp = subkeys.shape[0]
    ne = PEER_HEADS * PEER_TOPK
    out = lambda w: pl.BlockSpec((tl, w), lambda i, hp: (i, 0))
    return pl.pallas_call(
        _topk_kernel,
        grid=(T // tl, nhp // TOPK_SETS_PER_STEP),
        in_specs=[pl.BlockSpec((tl, TOPK_SETS_PER_STEP * PEER_DK_HALF), lambda i, hp: (i, hp)),
                  pl.BlockSpec((TOPK_SETS_PER_STEP, N_KEYS, PEER_DK_HALF), lambda i, hp: (hp, 0, 0))],
        out_specs=[out(ne), out(ne * TILE_ROWS)],
        out_shape=[jax.ShapeDtypeStruct((T, ne), I32), jax.ShapeDtypeStruct((T, ne * TILE_ROWS), F32)],
        scratch_shapes=[pltpu.VMEM((nhp, PEER_TOPK, tl), F32), pltpu.VMEM((nhp, PEER_TOPK, tl), I32)],
        compiler_params=_params(("arbitrary", "arbitrary")),
        name="peer_topk",
    )(qp, subkeys)


def _pack_table(w):
    E, D = w.shape
    assert E == 2 * TABLE_ROWS and D == (TILE_ROWS // 2) * LANES
    t = w.astype(BF16).reshape(2, TABLE_ROWS, D // LANES, LANES)
    return jnp.transpose(t, (1, 2, 0, 3)).reshape(TABLE_ROWS, TILE_ROWS, LANES)


def _gather_tiles(tab_ref, r_ref, t, c):
    return jnp.concatenate([tab_ref[r_ref[t, c * EXPERTS_PER_DOT + j]] for j in range(EXPERTS_PER_DOT)], axis=0)


def _fold(x, y, d, low):
    return jnp.where(low, x, y) + pltpu.roll(jnp.where(low, y, x), SUBLANES - d, axis=0)


def _sublane_sums(vs, lows):
    for d, low in zip((1, 2, 4), lows):
        vs = [_fold(vs[2 * i], vs[2 * i + 1], d, low) for i in range(len(vs) // 2)]
    return vs[0]


def _segment_mask():
    cw = EXPERTS_PER_DOT * TILE_ROWS
    sub = lax.broadcasted_iota(I32, (SUBLANES, cw), 0)
    lane = lax.broadcasted_iota(I32, (SUBLANES, cw), 1)
    return (lane % TILE_ROWS) // 2 == sub


def _peer_u_kernel(r_ref, x_ref, gx_ref, tab_ref, act_ref):
    tb = x_ref.shape[0]
    ne = r_ref.shape[1]
    cw = EXPERTS_PER_DOT * TILE_ROWS
    sub = lax.broadcasted_iota(I32, (SUBLANES, LANES), 0)
    lows = [(sub & d) == 0 for d in (1, 2, 4)]
    lane = lax.broadcasted_iota(I32, (SUBLANES, LANES), 1)
    seg_mask = _segment_mask()

    def group(base):
        xbs = [x_ref[base + tt].astype(BF16) for tt in range(SUBLANES)]
        cols = []
        for c in range(ne // EXPERTS_PER_DOT):
            zs = [jnp.where(seg_mask, _dot_nt(xbs[tt], _gather_tiles(tab_ref, r_ref, base + tt, c)), 0.0)
                  for tt in range(SUBLANES)]
            for v in range(cw // LANES):
                col = _sublane_sums([z[:, v * LANES:(v + 1) * LANES] for z in zs], lows)
                for dist in (2, 4, 8):
                    up = pltpu.roll(col, LANES - dist, axis=1)
                    dn = pltpu.roll(col, dist, axis=1)
                    col = col + jnp.where((lane & dist) == 0, up, dn)
                cols.append(col)
        d = jnp.concatenate(cols, axis=1)
        gelu = 0.5 * d * (1.0 + lax.erf(d * (2.0 ** -0.5)))
        act_ref[pl.ds(base, SUBLANES), :] = gelu * gx_ref[pl.ds(base, SUBLANES), :]

    def groups(gi, carry):
        for g in range(U_GROUPS_PER_ITER):
            group(pl.multiple_of((gi * U_GROUPS_PER_ITER + g) * SUBLANES, SUBLANES))
        return carry

    lax.fori_loop(0, tb // (SUBLANES * U_GROUPS_PER_ITER), groups, 0)


def _peer_u(rows, x3, gx, tab, tb):
    T, ne = rows.shape
    wide = pl.BlockSpec((tb, gx.shape[1]), lambda i: (i, 0))
    return pl.pallas_call(
        _peer_u_kernel,
        grid=(T // tb,),
        in_specs=[pl.BlockSpec((tb, ne), lambda i: (i, 0), memory_space=pltpu.SMEM),
                  pl.BlockSpec((tb,) + x3.shape[1:], lambda i: (i, 0, 0)),
                  wide,
                  pl.BlockSpec(tab.shape, lambda i: (0, 0, 0), pipeline_mode=pl.Buffered(1))],
        out_specs=wide,
        out_shape=jax.ShapeDtypeStruct(gx.shape, F32),
        compiler_params=_params(("arbitrary",), 48),
        name="peer_u",
    )(rows, x3, gx, tab)


def _peer_v_kernel(r_ref, a_ref, tab_ref, o_ref):
    tb = o_ref.shape[0]
    ne = r_ref.shape[1]
    cw = EXPERTS_PER_DOT * TILE_ROWS
    seg_mask = _segment_mask()

    def group(gi, carry):
        base = pl.multiple_of(gi * SUBLANES, SUBLANES)
        a8 = a_ref[pl.ds(base, SUBLANES), :]
        for tt in range(SUBLANES):
            acc = jnp.zeros(o_ref.shape[1:], F32)
            for c in range(ne // EXPERTS_PER_DOT):
                lhs = jnp.where(seg_mask, a8[tt:tt + 1, c * cw:(c + 1) * cw], 0.0).astype(BF16)
                acc = acc + _dot(lhs, _gather_tiles(tab_ref, r_ref, base + tt, c))
            o_ref[base + tt] = acc
        return carry

    lax.fori_loop(0, tb // SUBLANES, group, 0)


def _peer_v(rows, act, tab, tb):
    T, ne = rows.shape
    seg = tab.shape[1] // 2
    return pl.pallas_call(
        _peer_v_kernel,
        grid=(T // tb,),
        in_specs=[pl.BlockSpec((tb, ne), lambda i: (i, 0), memory_space=pltpu.SMEM),
                  pl.BlockSpec((tb, act.shape[1]), lambda i: (i, 0)),
                  pl.BlockSpec(tab.shape, lambda i: (0, 0, 0), pipeline_mode=pl.Buffered(1))],
        out_specs=pl.BlockSpec((tb, seg, LANES), lambda i: (i, 0, 0)),
        out_shape=jax.ShapeDtypeStruct((T, seg, LANES), F32),
        compiler_params=_params(("arbitrary",), 48),
        name="peer_v",
    )(rows, act, tab)


def _final_kernel(x1_ref, p_ref, g2_ref, fw_ref, y_ref):
    x2 = x1_ref[...] + g2_ref[0] * p_ref[...]
    y_ref[...] = _rms(x2) * fw_ref[...]


def _final(x1, peer, g2, fw, tm, tiles_per_seq):
    T, D = x1.shape
    rm = g2.shape[1]
    row = pl.BlockSpec((tm, D), lambda i: (i, 0))
    return pl.pallas_call(
        _final_kernel,
        grid=(T // tm,),
        in_specs=[row, row, pl.BlockSpec((1, rm, D), lambda i: (i // tiles_per_seq, 0, 0)),
                  pl.BlockSpec((1, D), lambda i: (0, 0))],
        out_specs=row,
        out_shape=jax.ShapeDtypeStruct((T, D), F32),
        compiler_params=_params(("arbitrary",)),
        name="final",
    )(x1, peer, g2, fw.reshape(1, D).astype(F32))


TILES = (256, 512, 256, 256, 64, 512)


def _post_mixers(og, of, x, g1, sh2, sc2, g2, wts, tm, rows_per_seq, tl, tb, tm_fin):
    wg, wf, wq, subkeys, tab_u, tab_v, fw = wts
    T, D = x.shape
    x1, h2, qp = _outproj(og, of, x, g1, sh2, sc2, wg, wf, wq, tm, rows_per_seq // tm)
    rows, gx = _topk(qp, subkeys, tl)
    act = _peer_u(rows, h2.reshape(T, D // LANES, LANES), gx, tab_u, tb)
    peer = _peer_v(rows, act, tab_v, tb).reshape(T, D)
    return _final(x1, peer, g2, fw, tm_fin, rows_per_seq // tm_fin)


def _row_to_col(row, n):
    r = lax.broadcasted_iota(I32, (n, n), 0)
    c = lax.broadcasted_iota(I32, (n, n), 1)
    return jnp.sum(jnp.where(r == c, row, 0.0), axis=1, keepdims=True)


def _gdn_step_kernel(u_ref, z_ref, sm_ref, cs_ref, s0_ref, cw_ref, nw_ref, o_ref, cs_out_ref, s_out_ref):
    u = u_ref[0]
    st = cs_ref[0]
    cw = cw_ref[...]
    conv = cw[CONV_WIDTH - 1:CONV_WIDTH] * u
    for j in range(CONV_WIDTH - 1):
        conv = conv + cw[j:j + 1] * st[j:j + 1]
    cs_out_ref[0] = jnp.concatenate([st[1:CONV_WIDTH - 1], u], axis=0)
    a = _silu(conv)
    z = z_ref[0]
    small = sm_ref[0]
    low = lax.broadcasted_iota(I32, (1, LANES), 1) < HEAD_DIM
    nw = nw_ref[...][:, :HEAD_DIM]
    outs = []
    for h in range(GDN_HEADS):
        cq = _head_block(a, 0, h, low)[:, :HEAD_DIM]
        ck = _head_block(a, GDN_WIDTH, h, low)[:, :HEAD_DIM]
        v = _head_block(a, 2 * GDN_WIDTH, h, low)[:, :HEAD_DIM]
        q = cq * lax.rsqrt(jnp.sum(cq * cq, axis=1, keepdims=True) + 1e-6) * (HEAD_DIM ** -0.5)
        k = ck * lax.rsqrt(jnp.sum(ck * ck, axis=1, keepdims=True) + 1e-6)
        g = small[:, h:h + 1]
        beta = small[:, GDN_HEADS + h:GDN_HEADS + h + 1]
        eg = jnp.exp(g)
        s0 = s0_ref[0, h]
        w_col = _row_to_col(k * (beta * eg), HEAD_DIM)
        vn = v * beta - jnp.sum(w_col * s0, axis=0, keepdims=True)
        q_col = _row_to_col(q * eg, HEAD_DIM)
        o = jnp.sum(q_col * s0, axis=0, keepdims=True) + jnp.sum(q * k, axis=1, keepdims=True) * vn
        s_out_ref[0, h] = s0 * eg + _row_to_col(k, HEAD_DIM) * vn
        zh = _head_block(z, 0, h, low)[:, :HEAD_DIM]
        outs.append(_rms(o) * nw * _silu(zh))
    o_ref[0] = jnp.concatenate(outs, axis=1).astype(o_ref.dtype)


def _gdn_step(gqkv, z, small, state_conv, state_delta, conv_w, norm_w):
    nb = gqkv.shape[0]
    W3 = 3 * GDN_WIDTH
    nw = jnp.concatenate([norm_w.astype(F32), jnp.zeros((LANES - HEAD_DIM,), F32)]).reshape(1, LANES)
    per_b = lambda *shape: pl.BlockSpec((1,) + shape, lambda b: (b,) + (0,) * len(shape))
    return pl.pallas_call(
        _gdn_step_kernel,
        grid=(nb,),
        in_specs=[per_b(1, W3), per_b(1, GDN_WIDTH), per_b(1, SMALL_W), per_b(CONV_WIDTH - 1, W3),
                  per_b(GDN_HEADS, HEAD_DIM, HEAD_DIM),
                  pl.BlockSpec((CONV_WIDTH, W3), lambda b: (0, 0)), pl.BlockSpec((1, LANES), lambda b: (0, 0))],
        out_specs=[per_b(1, GDN_WIDTH), per_b(CONV_WIDTH - 1, W3), per_b(GDN_HEADS, HEAD_DIM, HEAD_DIM)],
        out_shape=[jax.ShapeDtypeStruct((nb, 1, GDN_WIDTH), BF16),
                   jax.ShapeDtypeStruct((nb, CONV_WIDTH - 1, W3), F32),
                   jax.ShapeDtypeStruct((nb, GDN_HEADS, HEAD_DIM, HEAD_DIM), F32)],
        compiler_params=_params(("arbitrary",)),
        name="gdn_step",
    )(gqkv.reshape(nb, 1, W3), z.reshape(nb, 1, GDN_WIDTH), small.reshape(nb, 1, SMALL_W), state_conv, state_delta,
      conv_w, nw)


def _fox_decode_kernel(pt_ref, q_ref, kn_ref, vn_ref, sm_ref, *refs):
    page_refs = refs[:-6]
    o_ref, q_s, m_ref, l_ref, acc_ref, carry_ref = refs[-6:]
    p = pl.program_id(1)
    H = FOX_HEADS
    ps = page_refs[0].shape[1]
    n = ps * H
    rows = n // LANES
    strides = [H << i for i in range(int(np.log2(LANES // H)))]

    def tile_heads(x):
        for d in strides:
            x = x + pltpu.roll(x, d, axis=1)
        return x

    @pl.when(p == 0)
    def _():
        q = (q_ref[0] * (HEAD_DIM ** -0.5)).astype(BF16)
        q_s[...] = q
        m_ref[...] = jnp.sum(q.astype(F32) * kn_ref[0].astype(BF16).astype(F32), axis=1, keepdims=True)
        l_ref[...] = jnp.ones_like(l_ref)
        acc_ref[...] = vn_ref[0].astype(BF16).astype(F32)
        lane1 = lax.broadcasted_iota(I32, (1, LANES), 1)
        lf_new = jnp.where(lane1 < H, pltpu.roll(sm_ref[0], LANES - 2 * GDN_HEADS, axis=1), 0.0)
        carry_ref[...] = tile_heads(lf_new)

    li = lax.broadcasted_iota(I32, (LANES, 2 * LANES), 0)
    lo = lax.broadcasted_iota(I32, (LANES, 2 * LANES), 1)
    same_head = li % H == lo % H
    sel = jnp.where(same_head & ((lo >= LANES) | (li > lo)), 1.0, 0.0)
    ra = lax.broadcasted_iota(I32, (rows, rows), 0)
    ca = lax.broadcasted_iota(I32, (rows, rows), 1)
    upper = (ca > ra).astype(F32)
    own = lax.broadcasted_iota(I32, (H, n), 1) % H == lax.broadcasted_iota(I32, (H, n), 0)
    npages = len(page_refs) // 3
    k_refs, v_refs, lf_refs = page_refs[:npages], page_refs[npages:2 * npages], page_refs[2 * npages:]
    sums = [_dot(r[0], sel, precision=HIGHEST) for r in lf_refs]
    laters = [_dot(upper, sm[:, LANES:], precision=HIGHEST) for sm in sums]
    carry = carry_ref[...]
    ss = []
    for i in range(npages):
        bias = sums[i][:, :LANES] + laters[i] + carry
        carry = carry + jnp.sum(sums[i][:, LANES:], axis=0, keepdims=True)
        s = _dot_nt(q_s[...], k_refs[i][0].reshape(n, HEAD_DIM).astype(BF16))
        s = s + jnp.concatenate([jnp.broadcast_to(bias[a:a + 1, :], (H, LANES)) for a in range(rows)], axis=1)
        ss.append(jnp.where(own, s, NEG_INF))
    carry_ref[...] = carry
    m_old = m_ref[...]
    m_new = m_old
    for s in ss:
        m_new = jnp.maximum(m_new, jnp.max(s, axis=1, keepdims=True))
    alpha = jnp.exp(m_old - m_new)
    pms = [jnp.exp(s - m_new) for s in ss]
    l_new = l_ref[...] * alpha
    acc = acc_ref[...] * alpha
    for i in range(npages):
        l_new = l_new + jnp.sum(pms[i], axis=1, keepdims=True)
        acc = acc + _dot(pms[i].astype(BF16), v_refs[i][0].reshape(n, HEAD_DIM).astype(BF16))
    l_ref[...] = l_new
    acc_ref[...] = acc
    m_ref[...] = m_new

    @pl.when(p == pl.num_programs(1) - 1)
    def _():
        out = acc_ref[...] / l_ref[...]
        o_ref[0] = jnp.concatenate([out, jnp.zeros((H, LANES - HEAD_DIM), F32)], axis=1).astype(o_ref.dtype)


def _fox_decode(page_table, fq, fk, fv, small, cache_k, cache_v, cache_logf):
    nb, npg = page_table.shape
    npool, ps = cache_k.shape[:2]
    H = FOX_HEADS
    lf_rows = ps * H // LANES
    P = PAGES_PER_STEP
    assert npg % P == 0
    per_b = lambda *shape: pl.BlockSpec((1,) + shape, lambda b, p, pt: (b,) + (0,) * len(shape))

    def pages(*shape):
        return [pl.BlockSpec((1,) + shape, lambda b, p, pt, i=i: (pt[b, npg - 1 - (p * P + i)],) + (0,) * len(shape))
                for i in range(P)]

    heads = lambda a: a.reshape(nb, H, HEAD_DIM)
    lf = cache_logf.reshape(npool, lf_rows, LANES)
    return pl.pallas_call(
        _fox_decode_kernel,
        grid_spec=pltpu.PrefetchScalarGridSpec(
            num_scalar_prefetch=1,
            grid=(nb, npg // P),
            in_specs=[per_b(H, HEAD_DIM), per_b(H, HEAD_DIM), per_b(H, HEAD_DIM), per_b(1, SMALL_W)]
            + pages(ps, H, HEAD_DIM) + pages(ps, H, HEAD_DIM) + pages(lf_rows, LANES),
            out_specs=per_b(H, LANES),
            scratch_shapes=[pltpu.VMEM((H, HEAD_DIM), BF16), pltpu.VMEM((H, 1), F32), pltpu.VMEM((H, 1), F32),
                            pltpu.VMEM((H, HEAD_DIM), F32), pltpu.VMEM((1, LANES), F32)]),
        out_shape=jax.ShapeDtypeStruct((nb, H, LANES), BF16),
        compiler_params=_params(("arbitrary", "arbitrary"), 48),
        name="fox_decode",
    )(page_table, heads(fq), heads(fk), heads(fv), small.reshape(nb, 1, SMALL_W),
      *([cache_k] * P + [cache_v] * P + [lf] * P))


def kernel(x_prompt, x_sample, cache_k, cache_v, cache_logf, state_conv, state_delta, page_table, c_prompt, c_sample, w_mod, b_mod, w_in, conv_w, a_log, dt_bias, gdn_norm_w, fox_fb, w_out, peer_wq, peer_subkeys, peer_u, peer_v, final_norm_w):
    assert w_mod.shape[0] == 1, "one layer"
    D = x_prompt.shape[-1]
    Bp, L = x_prompt.shape[:2]
    Bs = x_sample.shape[0]
    assert x_sample.shape[1] == 1 and L % GDN_CHUNK == 0 and Bs % SUBLANES == 0
    tm_in, tq, tm_out, tl, tb, tm_fin = (min(t, L) for t in TILES)

    c = jnp.concatenate([c_prompt, c_sample], axis=0)
    mod = _mod(c, w_mod[0], b_mod[0])
    mod_p = [mod[:Bp, None, j * D:(j + 1) * D] for j in range(N_MOD)]
    pad_s = (-Bs) % LANES
    mod_s = [mod[None, Bp:, j * D:(j + 1) * D] for j in range(N_MOD)]
    mod_s_pad = [jnp.pad(m, ((0, 0), (0, pad_s), (0, 0))) for m in mod_s]

    w_cat, padd, alog = _prep_inproj(w_in[0], a_log[0], dt_bias[0], fox_fb[0])
    wg, wf, wq = _prep_outproj(w_out[0], peer_wq[0])
    subkeys = peer_subkeys[0].reshape(PEER_HEADS * 2, N_KEYS, PEER_DK_HALF).astype(BF16)
    wts = (wg, wf, wq, subkeys, _pack_table(peer_u[0]), _pack_table(peer_v[0]), final_norm_w)

    xp = x_prompt.reshape(Bp * L, D)
    gqkv, z, small, fk, fv, qa, ka, va = _inproj(xp, mod_p[0], mod_p[1], w_cat, padd, alog, tm_in, L // tm_in)
    og, delta_p = _gdn_prompt(gqkv, z, small, conv_w[0], gdn_norm_w[0], Bp, L)
    of = _fox_prompt(qa, ka, va, Bp, L, tq)
    y_p = _post_mixers(og, of, xp, mod_p[2], mod_p[3], mod_p[4], mod_p[5], wts, tm_out, L, tl, tb, tm_fin)

    xs = x_sample.reshape(Bs, D)
    gqkv_s, z_s, small_s, fk_s, fv_s, _, _, _, fq_s = _inproj(xs, mod_s[0], mod_s[1], w_cat, padd, alog, Bs, 1,
                                                              with_fq=True)
    og_s, conv_s, delta_s = _gdn_step(gqkv_s, z_s, small_s, state_conv[0], state_delta[0], conv_w[0], gdn_norm_w[0])
    of_s = _fox_decode(page_table, fq_s, fk_s, fv_s, small_s, cache_k[0], cache_v[0], cache_logf[0])
    rows = lambda a: jnp.pad(a.reshape(Bs, -1), ((0, pad_s), (0, 0)))
    ts = Bs + pad_s
    y_s = _post_mixers(rows(og_s), rows(of_s), rows(xs), mod_s_pad[2], mod_s_pad[3], mod_s_pad[4], mod_s_pad[5], wts,
                       ts, ts, ts, min(tb, ts), ts)[:Bs]

    nf = 2 * GDN_HEADS
    return (y_p.reshape(Bp, L, D), y_s.reshape(Bs, 1, D),
            fk.reshape(1, Bp, L, FOX_HEADS, HEAD_DIM), fv.reshape(1, Bp, L, FOX_HEADS, HEAD_DIM),
            small[:, nf:nf + FOX_HEADS].reshape(1, Bp, L, FOX_HEADS),
            gqkv.reshape(Bp, L, -1)[None, :, L - (CONV_WIDTH - 1):], delta_p[None],
            fk_s.reshape(1, Bs, 1, FOX_HEADS, HEAD_DIM), fv_s.reshape(1, Bs, 1, FOX_HEADS, HEAD_DIM),
            small_s[:, nf:nf + FOX_HEADS].reshape(1, Bs, 1, FOX_HEADS), conv_s[None], delta_s[None])
```

```python
import functools

import numpy as np
import jax
import jax.numpy as jnp
from jax import lax
from jax.experimental import pallas as pl
from jax.experimental.pallas import tpu as pltpu

F32 = jnp.float32
BF16 = jnp.bfloat16
I32 = jnp.int32

HEAD_DIM = 64
GDN_HEADS = 8
FOX_HEADS = 8
GDN_WIDTH = GDN_HEADS * HEAD_DIM
FOX_WIDTH = FOX_HEADS * HEAD_DIM
CONV_WIDTH = 4
PEER_HEADS = 8
N_KEYS = 128
PEER_TOPK = 16
PEER_DK_HALF = 128
N_MOD = 6
EPS = 1e-6
LANES = 128
SUBLANES = 8
GDN_CHUNK = 128
SMALL_W = LANES
BF16_TILE_ROWS = 16
TABLE_ROWS_LOG2 = 13
TABLE_ROWS = 1 << TABLE_ROWS_LOG2
TILE_ROWS = BF16_TILE_ROWS
EXPERTS_PER_DOT = 16
U_GROUPS_PER_ITER = 4
PAGES_PER_STEP = 8
TOPK_SETS_PER_STEP = 4
FOX_ROW_SPLIT = 4
HIGHEST = lax.Precision.HIGHEST
NEG_INF = float("-inf")


def _dot(a, b, precision=None):
    return jnp.dot(a, b, preferred_element_type=F32, precision=precision)


def _dot_nt(a, b, precision=None):
    return lax.dot_general(a, b, (((1,), (1,)), ((), ())), preferred_element_type=F32, precision=precision)


def _bdot(a, b):
    return _dot(a.astype(BF16), b.astype(BF16))


def _bdot_nt(a, b):
    return _dot_nt(a.astype(BF16), b.astype(BF16))


def _split3(x):
    hi = x.astype(BF16)
    r1 = x - hi.astype(F32)
    mid = r1.astype(BF16)
    lo = (r1 - mid.astype(F32)).astype(BF16)
    return hi, mid, lo


def _dot3(a, b):
    ah = a.astype(BF16)
    al = (a - ah.astype(F32)).astype(BF16)
    bh = b.astype(BF16)
    bl = (b - bh.astype(F32)).astype(BF16)
    return _dot(ah, bh) + (_dot(ah, bl) + _dot(al, bh))


def _params(sem, vmem_mb=None):
    kw = dict(dimension_semantics=sem)
    if vmem_mb is not None:
        kw["vmem_limit_bytes"] = vmem_mb * 1024 * 1024
    return pltpu.CompilerParams(**kw)


def _rms(x):
    return x * lax.rsqrt(jnp.mean(x * x, axis=-1, keepdims=True) + EPS)


def _silu(x):
    return x * jax.nn.sigmoid(x)


def _mod_kernel(c_ref, w_ref, b_ref, o_ref):
    s = _silu(c_ref[...])
    o_ref[...] = _bdot(s, w_ref[...]) + b_ref[...]


def _mod(c, w, b):
    n, d = c.shape
    nout = w.shape[1]
    tn = 1024
    return pl.pallas_call(
        _mod_kernel,
        grid=(nout // tn,),
        in_specs=[pl.BlockSpec((n, d), lambda j: (0, 0)),
                  pl.BlockSpec((d, tn), lambda j: (0, j)),
                  pl.BlockSpec((1, tn), lambda j: (0, j))],
        out_specs=pl.BlockSpec((n, tn), lambda j: (0, j)),
        out_shape=jax.ShapeDtypeStruct((n, nout), F32),
        compiler_params=_params(("arbitrary",)),
        name="mod",
    )(c, w, b.reshape(1, nout))


def _spread_heads(a):
    tm = a.shape[0]
    lane = lax.broadcasted_iota(I32, (tm, LANES), 1)
    low = lane < HEAD_DIM
    out = []
    for j in range(a.shape[1] // LANES):
        blk = a[:, j * LANES:(j + 1) * LANES]
        out.append(jnp.where(low, blk, 0.0))
        out.append(jnp.where(low, pltpu.roll(blk, HEAD_DIM, axis=1), 0.0))
    return jnp.concatenate(out, axis=1)


def _inproj_kernel(tiles_per_seq, x_ref, sh_ref, sc_ref, w_ref, padd_ref, alog_ref, selq_ref, selk_ref, cq_ref,
                   ck_ref, cv_ref, gqkv_ref, z_ref, small_ref, fk_ref, fv_ref, qa_ref, ka_ref, va_ref, *rest):
    carry_ref = rest[-1]
    i = pl.program_id(0)
    tm = x_ref.shape[0]
    h = _rms(x_ref[...]) * (1.0 + sc_ref[0]) + sh_ref[0]
    hb = h.astype(BF16)
    o = 0
    gqkv_ref[...] = _dot(hb, w_ref[:, o:o + 3 * GDN_WIDTH]); o += 3 * GDN_WIDTH
    z_ref[...] = _dot(hb, w_ref[:, o:o + GDN_WIDTH]); o += GDN_WIDTH
    sm = _dot(hb, w_ref[:, o:o + SMALL_W]); o += SMALL_W
    fq = _dot(hb, w_ref[:, o:o + FOX_WIDTH]); o += FOX_WIDTH
    fk = _dot(hb, w_ref[:, o:o + FOX_WIDTH]); o += FOX_WIDTH
    fv = _dot(hb, w_ref[:, o:o + FOX_WIDTH])
    fk_ref[...] = fk
    fv_ref[...] = fv
    if len(rest) == 2:
        rest[0][...] = fq

    lane = lax.broadcasted_iota(I32, (tm, SMALL_W), 1)
    y = sm + padd_ref[...]
    t = jnp.log1p(jnp.exp(-jnp.abs(y)))
    softplus = jnp.maximum(y, 0.0) + t
    logsig = jnp.minimum(y, 0.0) - t
    small = jnp.where(lane < GDN_HEADS, -jnp.exp(alog_ref[...]) * softplus,
                      jnp.where(lane < 2 * GDN_HEADS, jax.nn.sigmoid(sm),
                                jnp.where(lane < 2 * GDN_HEADS + FOX_HEADS, logsig, 0.0)))
    small_ref[...] = small

    @pl.when(i % tiles_per_seq == 0)
    def _():
        carry_ref[...] = jnp.zeros_like(carry_ref)

    r = lax.broadcasted_iota(I32, (tm, tm), 0)
    c = lax.broadcasted_iota(I32, (tm, tm), 1)
    ltri = (c <= r).astype(F32)
    cum = _dot(ltri, small, precision=HIGHEST) + carry_ref[...]
    carry_ref[...] = cum[tm - 1:tm, :]
    hi, mid, lo = _split3(cum)
    parts = jnp.concatenate([hi, mid, lo], axis=1)
    qa_ref[...] = (_spread_heads(fq * (HEAD_DIM ** -0.5)) + _dot(parts, selq_ref[...]) + cq_ref[...]).astype(BF16)
    ka_ref[...] = (_spread_heads(fk) + _dot(parts, selk_ref[...]) + ck_ref[...]).astype(BF16)
    va_ref[...] = (_spread_heads(fv) + cv_ref[...]).astype(BF16)


def _inproj_consts():
    selq = np.zeros((3 * SMALL_W, FOX_HEADS * LANES), np.float32)
    selk = np.zeros((3 * SMALL_W, FOX_HEADS * LANES), np.float32)
    cq = np.zeros((1, FOX_HEADS * LANES), np.float32)
    ck = np.zeros((1, FOX_HEADS * LANES), np.float32)
    cv = np.zeros((1, FOX_HEADS * LANES), np.float32)
    for h in range(FOX_HEADS):
        base = h * LANES + HEAD_DIM
        for p in range(3):
            src = p * SMALL_W + 2 * GDN_HEADS + h
            selq[src, base + p] = 1.0
            ck[0, base + p] = 1.0
            selk[src, base + 3 + p] = -1.0
            cq[0, base + 3 + p] = 1.0
        cv[0, base] = 1.0
    return (jnp.asarray(selq, BF16), jnp.asarray(selk, BF16), jnp.asarray(cq), jnp.asarray(ck), jnp.asarray(cv))


def _prep_inproj(w_in, a_log, dt_bias, fox_fb):
    o = np.cumsum((0, GDN_WIDTH, GDN_WIDTH, GDN_WIDTH, GDN_WIDTH, GDN_HEADS, GDN_HEADS, FOX_WIDTH, FOX_WIDTH,
                   FOX_WIDTH, FOX_HEADS)).tolist()
    d = w_in.shape[0]
    nsmall = 2 * GDN_HEADS + FOX_HEADS
    small = jnp.concatenate([w_in[:, o[4]:o[6]], w_in[:, o[9]:o[10]], jnp.zeros((d, SMALL_W - nsmall), w_in.dtype)], 1)
    w_cat = jnp.concatenate([w_in[:, :o[4]], small, w_in[:, o[6]:o[9]]], axis=1).astype(BF16)
    zero = jnp.zeros((GDN_HEADS,), F32)
    tail = jnp.zeros((SMALL_W - nsmall,), F32)
    padd = jnp.concatenate([dt_bias.astype(F32), zero, fox_fb.astype(F32), tail]).reshape(1, SMALL_W)
    alog = jnp.concatenate([a_log.astype(F32), zero, zero, tail]).reshape(1, SMALL_W)
    return w_cat, padd, alog


def _inproj(x, sh, sc, w_cat, padd, pmul, tm, tiles_per_seq, with_fq=False):
    T, D = x.shape
    rm = sh.shape[1]
    nw = w_cat.shape[1]
    selq, selk, cq, ck, cv = _inproj_consts()
    aw = FOX_HEADS * LANES
    row = lambda w: pl.BlockSpec((tm, w), lambda i: (i, 0))
    const = lambda a: pl.BlockSpec(a.shape, lambda i: (0,) * a.ndim)
    seq = pl.BlockSpec((1, rm, D), lambda i: (i // tiles_per_seq, 0, 0))
    outs = [(3 * GDN_WIDTH, F32), (GDN_WIDTH, F32), (SMALL_W, F32), (FOX_WIDTH, F32), (FOX_WIDTH, F32),
            (aw, BF16), (aw, BF16), (aw, BF16)]
    if with_fq:
        outs.append((FOX_WIDTH, F32))
    return pl.pallas_call(
        functools.partial(_inproj_kernel, tiles_per_seq),
        grid=(T // tm,),
        in_specs=[row(D), seq, seq, const(w_cat), const(padd), const(pmul), const(selq), const(selk), const(cq),
                  const(ck), const(cv)],
        out_specs=[row(w) for w, _ in outs],
        out_shape=[jax.ShapeDtypeStruct((T, w), dt) for w, dt in outs],
        scratch_shapes=[pltpu.VMEM((1, SMALL_W), F32)],
        compiler_params=_params(("arbitrary",), 48),
        name="inproj",
    )(x, sh, sc, w_cat, padd, pmul, selq, selk, cq, ck, cv)


def _head_block(a, base, h, low):
    j, odd = divmod(h, 2)
    blk = a[:, base + j * LANES:base + (j + 1) * LANES]
    if odd:
        blk = pltpu.roll(blk, HEAD_DIM, axis=1)
    return jnp.where(low, blk, 0.0)


def _join_heads(heads):
    return jnp.concatenate([heads[2 * j] + pltpu.roll(heads[2 * j + 1], HEAD_DIM, axis=1)
                            for j in range(len(heads) // 2)], axis=1)


def _unit_lower_inverses(a_mats, r, cc):
    n = a_mats[0].shape[0]
    s = SUBLANES
    same = (r // s) == (cc // s)
    eye = jnp.where(r == cc, 1.0, 0.0)
    bs = [jnp.where(same, -a, 0.0) for a in a_mats]
    ts = [eye + b for b in bs]
    b2s = [_dot3(b, b) for b in bs]
    ts = [t + _dot3(t, b2) for t, b2 in zip(ts, b2s)]
    b4s = [_dot3(b2, b2) for b2 in b2s]
    ts = [t + _dot3(t, b4) for t, b4 in zip(ts, b4s)]
    while s < n:
        same2 = (r // (2 * s)) == (cc // (2 * s))
        new = same2 & jnp.logical_not(same)
        xs = [_dot3(t, jnp.where(new, a, 0.0)) for t, a in zip(ts, a_mats)]
        ts = [t - _dot3(x, t) for t, x in zip(ts, xs)]
        same = same2
        s *= 2
    return ts


def _gdn_kernel(u_ref, z_ref, sm_ref, cw_ref, nw_ref, o_ref, s_out_ref, ubuf, s_ref):
    c = pl.program_id(1)
    C = u_ref.shape[0]
    W3 = 3 * GDN_WIDTH

    @pl.when(c == 0)
    def _():
        ubuf[0:SUBLANES, :] = jnp.zeros((SUBLANES, W3), F32)
        s_ref[...] = jnp.zeros_like(s_ref)

    ubuf[SUBLANES:SUBLANES + C, :] = u_ref[...]
    cw = cw_ref[...]
    conv = cw[CONV_WIDTH - 1:CONV_WIDTH] * ubuf[SUBLANES:SUBLANES + C, :]
    for j in range(1, CONV_WIDTH):
        conv = conv + cw[CONV_WIDTH - 1 - j:CONV_WIDTH - j] * ubuf[SUBLANES - j:SUBLANES - j + C, :]
    ubuf[0:SUBLANES, :] = ubuf[C:C + SUBLANES, :]
    a = _silu(conv)
    z = z_ref[...]
    small = sm_ref[...]

    lane = lax.broadcasted_iota(I32, (C, LANES), 1)
    low = lane < HEAD_DIM
    r = lax.broadcasted_iota(I32, (C, C), 0)
    cc = lax.broadcasted_iota(I32, (C, C), 1)
    causal = cc <= r
    strict = cc < r
    g_cum = _dot(causal.astype(F32), small, precision=HIGHEST)
    g_last = g_cum[C - 1:C, :]
    e_g = jnp.exp(g_cum)
    e_gl = jnp.exp(g_last - g_cum)
    e_last = jnp.exp(g_last)
    neg_g_hi = pltpu.roll(-g_cum, HEAD_DIM, axis=1)
    nw = nw_ref[...]
    heads = range(GDN_HEADS)
    col = lambda m, h: m[:, h:h + 1]
    cqs = [_head_block(a, 0, h, low) for h in heads]
    cks = [_head_block(a, GDN_WIDTH, h, low) for h in heads]
    vs = [_head_block(a, 2 * GDN_WIDTH, h, low) for h in heads]
    qs = [cq * lax.rsqrt(jnp.sum(cq * cq, axis=1, keepdims=True) + 1e-6) * (HEAD_DIM ** -0.5) for cq in cqs]
    ks = [ck * lax.rsqrt(jnp.sum(ck * ck, axis=1, keepdims=True) + 1e-6) for ck in cks]
    betas = [col(small, GDN_HEADS + h) for h in heads]
    g_diffs = [_dot_nt(jnp.where(lane == h, g_cum, jnp.where(lane == h + HEAD_DIM, 1.0, 0.0)),
                       jnp.where(lane == h, 1.0, jnp.where(lane == h + HEAD_DIM, neg_g_hi, 0.0)),
                       precision=HIGHEST) for h in heads]
    decays = [jnp.exp(jnp.where(causal, g, NEG_INF)) for g in g_diffs]
    a_mats = [jnp.where(strict, betas[h] * _bdot_nt(ks[h], ks[h]) * decays[h], 0.0) for h in heads]
    rhss = [vs[h] * betas[h] + pltpu.roll(ks[h] * (betas[h] * col(e_g, h)), HEAD_DIM, axis=1) for h in heads]
    invs = _unit_lower_inverses(a_mats, r, cc)
    xs = [_dot3(invs[h], rhss[h]) for h in heads]
    u_mats = [jnp.where(low, x, 0.0) for x in xs]
    w_mats = [jnp.where(low, pltpu.roll(x, HEAD_DIM, axis=1), 0.0) for x in xs]
    a_qks = [_bdot_nt(qs[h], ks[h]) * decays[h] for h in heads]
    s_olds = [s_ref[h] for h in heads]
    vns = [u_mats[h] - _bdot(w_mats[h], s_olds[h]) for h in heads]
    os_ = [_bdot(qs[h] * col(e_g, h), s_olds[h]) + _bdot(a_qks[h], vns[h]) for h in heads]
    for h in heads:
        s_ref[h] = s_olds[h] * col(e_last, h) + _bdot((ks[h] * col(e_gl, h)).T, vns[h])
    outs = []
    for h in heads:
        o = os_[h]
        ms = jnp.sum(o * o, axis=1, keepdims=True) * (1.0 / HEAD_DIM)
        outs.append(o * lax.rsqrt(ms + EPS) * nw * _silu(_head_block(z, 0, h, low)))
    o_ref[...] = _join_heads(outs).astype(o_ref.dtype)

    @pl.when(c == pl.num_programs(1) - 1)
    def _():
        s_out_ref[0] = s_ref[:, 0:HEAD_DIM, 0:HEAD_DIM]


def _gdn_prompt(gqkv, z, small, conv_w, norm_w, nb, L):
    C = GDN_CHUNK
    nch = L // C
    W3 = 3 * GDN_WIDTH
    nw = jnp.concatenate([norm_w.astype(F32), jnp.zeros((LANES - HEAD_DIM,), F32)]).reshape(1, LANES)
    row = lambda w: pl.BlockSpec((C, w), lambda b, c: (b * nch + c, 0))
    return pl.pallas_call(
        _gdn_kernel,
        grid=(nb, nch),
        in_specs=[row(W3), row(GDN_WIDTH), row(SMALL_W),
                  pl.BlockSpec((CONV_WIDTH, W3), lambda b, c: (0, 0)),
                  pl.BlockSpec((1, LANES), lambda b, c: (0, 0))],
        out_specs=[row(GDN_WIDTH),
                   pl.BlockSpec((1, GDN_HEADS, HEAD_DIM, HEAD_DIM), lambda b, c: (b, 0, 0, 0))],
        out_shape=[jax.ShapeDtypeStruct((nb * L, GDN_WIDTH), BF16),
                   jax.ShapeDtypeStruct((nb, GDN_HEADS, HEAD_DIM, HEAD_DIM), F32)],
        scratch_shapes=[pltpu.VMEM((SUBLANES + C, W3), F32), pltpu.VMEM((GDN_HEADS, LANES, LANES), F32)],
        compiler_params=_params(("arbitrary", "arbitrary")),
        name="gdn_prompt",
    )(gqkv, z, small, conv_w, nw)


def _fox_kernel(qt_ref, kt_ref, q_ref, k_ref, v_ref, o_ref, m_ref, acc_ref):
    p = pl.program_id(2)
    qi = qt_ref[p]
    ki = kt_ref[p]
    tq = q_ref.shape[0]
    tk = k_ref.shape[0]

    @pl.when(ki == 0)
    def _():
        m_ref[...] = jnp.full_like(m_ref, NEG_INF)
        acc_ref[...] = jnp.zeros_like(acc_ref)

    def step(on_diagonal):
        tr = tq // FOX_ROW_SPLIT
        blocks = [slice(i * tr, (i + 1) * tr) for i in range(FOX_ROW_SPLIT)]
        k = k_ref[...]
        v = v_ref[...]
        ss = [_dot_nt(q_ref[b, :], k) for b in blocks]
        if on_diagonal:
            row = lax.broadcasted_iota(I32, (tr, tk), 0)
            col = lax.broadcasted_iota(I32, (tr, tk), 1)
            ss = [jnp.where(col <= row + i * tr, s, NEG_INF) for i, s in enumerate(ss)]
        m_olds = [m_ref[b, :] for b in blocks]
        m_news = [jnp.maximum(m, jnp.max(s, axis=1, keepdims=True)) for m, s in zip(m_olds, ss)]
        ps = [jnp.exp(s - m).astype(BF16) for s, m in zip(ss, m_news)]
        for b, m_old, m_new, p in zip(blocks, m_olds, m_news, ps):
            acc_ref[b, :] = acc_ref[b, :] * jnp.exp(m_old - m_new) + _dot(p, v)
            m_ref[b, :] = m_new

    @pl.when(ki < qi)
    def _():
        step(False)

    @pl.when(ki == qi)
    def _():
        step(True)
        acc = acc_ref[...]
        o_ref[...] = (acc / acc[:, HEAD_DIM:HEAD_DIM + 1]).astype(o_ref.dtype)


def _fox_prompt(qa, ka, va, nb, L, tq):
    nq = L // tq
    pairs = [(i, j) for i in range(nq) for j in range(i + 1)]
    qt = jnp.asarray([p[0] for p in pairs], I32)
    kt = jnp.asarray([p[1] for p in pairs], I32)
    qspec = pl.BlockSpec((tq, LANES), lambda b, h, p, qt, kt: (b * nq + qt[p], h))
    kspec = pl.BlockSpec((tq, LANES), lambda b, h, p, qt, kt: (b * nq + kt[p], h))
    return pl.pallas_call(
        _fox_kernel,
        grid_spec=pltpu.PrefetchScalarGridSpec(
            num_scalar_prefetch=2,
            grid=(nb, FOX_HEADS, len(pairs)),
            in_specs=[qspec, kspec, kspec],
            out_specs=qspec,
            scratch_shapes=[pltpu.VMEM((tq, 1), F32), pltpu.VMEM((tq, LANES), F32)]),
        out_shape=jax.ShapeDtypeStruct(qa.shape, BF16),
        compiler_params=_params(("arbitrary", "arbitrary", "arbitrary")),
        name="fox_prompt",
    )(qt, kt, qa, ka, va)


def _outproj_kernel(og_ref, of_ref, x_ref, g1_ref, sh_ref, sc_ref, wg_ref, wf_ref, wq_ref, x1_ref, h2_ref, qp_ref):
    m = _dot(og_ref[...], wg_ref[...]) + _dot(of_ref[...], wf_ref[...])
    x1 = x_ref[...] + g1_ref[0] * m
    x1_ref[...] = x1
    h2 = _rms(x1) * (1.0 + sc_ref[0]) + sh_ref[0]
    h2_ref[...] = h2
    qp_ref[...] = _dot(h2.astype(BF16), wq_ref[...]).astype(BF16)


def _prep_outproj(w_out, peer_wq):
    wg = w_out[:GDN_WIDTH].astype(BF16)
    wf = w_out[GDN_WIDTH:].reshape(FOX_HEADS, HEAD_DIM, -1)
    wf = jnp.pad(wf, ((0, 0), (0, LANES - HEAD_DIM), (0, 0))).reshape(FOX_HEADS * LANES, -1).astype(BF16)
    return wg, wf, peer_wq.astype(BF16)


def _outproj(og, of, x, g1, sh2, sc2, wg, wf, wq, tm, tiles_per_seq):
    T, D = x.shape
    rm = g1.shape[1]
    nq = wq.shape[1]
    row = lambda w: pl.BlockSpec((tm, w), lambda i: (i, 0))
    const = lambda a: pl.BlockSpec(a.shape, lambda i: (0,) * a.ndim)
    seq = pl.BlockSpec((1, rm, D), lambda i: (i // tiles_per_seq, 0, 0))
    return pl.pallas_call(
        _outproj_kernel,
        grid=(T // tm,),
        in_specs=[row(og.shape[1]), row(of.shape[1]), row(D), seq, seq, seq, const(wg), const(wf), const(wq)],
        out_specs=[row(D), row(D), row(nq)],
        out_shape=[jax.ShapeDtypeStruct((T, D), F32), jax.ShapeDtypeStruct((T, D), F32),
                   jax.ShapeDtypeStruct((T, nq), BF16)],
        compiler_params=_params(("arbitrary",), 48),
        name="outproj",
    )(og, of, x, g1, sh2, sc2, wg, wf, wq)


def _topk_rows(ss, k):
    n = ss[0].shape[0]
    iota_n = lax.broadcasted_iota(I32, ss[0].shape, 0)
    vals = [[] for _ in ss]
    idxs = [[] for _ in ss]
    for _ in range(k):
        ms = [jnp.max(s, axis=0, keepdims=True) for s in ss]
        ids = [jnp.min(jnp.where(s == m, iota_n, n), axis=0, keepdims=True) for s, m in zip(ss, ms)]
        ss = [jnp.where(iota_n == i, NEG_INF, s) for s, i in zip(ss, ids)]
        for j, (m, i) in enumerate(zip(ms, ids)):
            vals[j].append(m)
            idxs[j].append(i)
    return [jnp.concatenate(v, axis=0) for v in vals], [jnp.concatenate(i, axis=0) for i in idxs]


def _pair_rows(a0, a1, op):
    k = PEER_TOPK
    rows = [op(a0[0:1], a1[0:SUBLANES]), op(a0[0:1], a1[SUBLANES:k])]
    rows += [op(a0[a:a + 1], a1[0:SUBLANES]) for a in range(1, SUBLANES)]
    rows.append(op(a0[SUBLANES:k], a1[0:1]))
    return jnp.concatenate(rows, axis=0)


def _topk_kernel(q_ref, sk_ref, r_ref, gx_ref, sv_ref, si_ref):
    hp = pl.program_id(1)
    tl = q_ref.shape[0]
    k = PEER_TOPK
    nsub = sk_ref.shape[0]
    dk = sk_ref.shape[2]
    ss = [_dot_nt(sk_ref[i], q_ref[:, i * dk:(i + 1) * dk]) for i in range(nsub)]
    vals, idxs = _topk_rows(ss, k)
    for i in range(nsub):
        sv_ref[hp * nsub + i] = vals[i]
        si_ref[hp * nsub + i] = idxs[i]

    @pl.when(hp == pl.num_programs(1) - 1)
    def _():
        nrow = 10 * SUBLANES
        ridx = lax.broadcasted_iota(I32, (nrow, tl), 0)
        blk = ridx // SUBLANES
        w = ridx % SUBLANES
        flat = jnp.where(blk == 0, w, jnp.where(blk == 1, SUBLANES + w,
                         jnp.where(blk <= SUBLANES, k * (blk - 1) + w, k * (SUBLANES + w))))
        heads = range(PEER_HEADS)
        cands = [_pair_rows(sv_ref[2 * h], sv_ref[2 * h + 1], lambda a, b: a + b) for h in heads]
        ecands = [_pair_rows(si_ref[2 * h], si_ref[2 * h + 1], lambda a, b: a * N_KEYS + b) for h in heads]
        fvs = [[] for _ in heads]
        es = [[] for _ in heads]
        for _ in range(k):
            ms = [jnp.max(c, axis=0, keepdims=True) for c in cands]
            fs = [jnp.min(jnp.where(c == m, flat, k * k), axis=0, keepdims=True) for c, m in zip(cands, ms)]
            sels = [flat == f for f in fs]
            for h in heads:
                es[h].append(jnp.max(jnp.where(sels[h], ecands[h], -1), axis=0, keepdims=True))
                fvs[h].append(ms[h])
            cands = [jnp.where(sel, NEG_INF, c) for sel, c in zip(sels, cands)]
        g_all = []
        for h in heads:
            fv = jnp.concatenate(fvs[h], axis=0)
            ex = jnp.exp(fv - fv[0:1])
            g_all.append(ex / jnp.sum(ex, axis=0, keepdims=True))
        e_mat = jnp.concatenate([jnp.concatenate(e, axis=0) for e in es], axis=0)
        g_mat = jnp.concatenate(g_all, axis=0)
        row_mat = (e_mat & (TABLE_ROWS - 1)).astype(F32)
        half_mat = e_mat >> TABLE_ROWS_LOG2
        for j in range(tl // LANES):
            tok = slice(j * LANES, (j + 1) * LANES)
            r_ref[tok, :] = row_mat[:, tok].T.astype(I32)
        par = lax.broadcasted_iota(I32, (TILE_ROWS, tl), 0) % 2
        for i in range(PEER_HEADS * k // SUBLANES):
            rows = [jnp.where(par == half_mat[n:n + 1], g_mat[n:n + 1], 0.0)
                    for n in range(i * SUBLANES, (i + 1) * SUBLANES)]
            blk = jnp.concatenate(rows, axis=0)
            for j in range(tl // LANES):
                tok = slice(j * LANES, (j + 1) * LANES)
                gx_ref[tok, i * LANES:(i + 1) * LANES] = blk[:, tok].T


def _topk(qp, subkeys, tl):
    T = qp.shape[0]
    nhp = subkeys.shape[0]
    ne = PEER_HEADS * PEER_TOPK
    out = lambda w: pl.BlockSpec((tl, w), lambda i, hp: (i, 0))
    return pl.pallas_call(
        _topk_kernel,
        grid=(T // tl, nhp // TOPK_SETS_PER_STEP),
        in_specs=[pl.BlockSpec((tl, TOPK_SETS_PER_STEP * PEER_DK_HALF), lambda i, hp: (i, hp)),
                  pl.BlockSpec((TOPK_SETS_PER_STEP, N_KEYS, PEER_DK_HALF), lambda i, hp: (hp, 0, 0))],
        out_specs=[out(ne), out(ne * TILE_ROWS)],
        out_shape=[jax.ShapeDtypeStruct((T, ne), I32), jax.ShapeDtypeStruct((T, ne * TILE_ROWS), F32)],
        scratch_shapes=[pltpu.VMEM((nhp, PEER_TOPK, tl), F32), pltpu.VMEM((nhp, PEER_TOPK, tl), I32)],
        compiler_params=_params(("arbitrary", "arbitrary")),
        name="peer_topk",
    )(qp, subkeys)


def _pack_table(w):
    E, D = w.shape
    assert E == 2 * TABLE_ROWS and D == (TILE_ROWS // 2) * LANES
    t = w.astype(BF16).reshape(2, TABLE_ROWS, D // LANES, LANES)
    return jnp.transpose(t, (1, 2, 0, 3)).reshape(TABLE_ROWS, TILE_ROWS, LANES)


def _gather_tiles(tab_ref, r_ref, t, c):
    return jnp.concatenate([tab_ref[r_ref[t, c * EXPERTS_PER_DOT + j]] for j in range(EXPERTS_PER_DOT)], axis=0)


def _fold(x, y, d, low):
    return jnp.where(low, x, y) + pltpu.roll(jnp.where(low, y, x), SUBLANES - d, axis=0)


def _sublane_sums(vs, lows):
    for d, low in zip((1, 2, 4), lows):
        vs = [_fold(vs[2 * i], vs[2 * i + 1], d, low) for i in range(len(vs) // 2)]
    return vs[0]


def _swap8(vs):
    sub = lax.broadcasted_iota(I32, (SUBLANES, LANES), 0)
    for d in (4, 2, 1):
        low = (sub & d) == 0
        out = list(vs)
        for i in range(SUBLANES):
            if i & d == 0:
                out[i] = jnp.where(low, vs[i], pltpu.roll(vs[i + d], d, axis=0))
                out[i + d] = jnp.where(low, pltpu.roll(vs[i], SUBLANES - d, axis=0), vs[i + d])
        vs = out
    return vs


def _segment_mask():
    cw = EXPERTS_PER_DOT * TILE_ROWS
    sub = lax.broadcasted_iota(I32, (SUBLANES, cw), 0)
    lane = lax.broadcasted_iota(I32, (SUBLANES, cw), 1)
    return (lane % TILE_ROWS) // 2 == sub


def _peer_u_kernel(r_ref, x_ref, gx_ref, tab_ref, act_ref):
    tb = x_ref.shape[0]
    ne = r_ref.shape[1]
    cw = EXPERTS_PER_DOT * TILE_ROWS
    sub = lax.broadcasted_iota(I32, (SUBLANES, LANES), 0)
    lows = [(sub & d) == 0 for d in (1, 2, 4)]
    lane = lax.broadcasted_iota(I32, (SUBLANES, LANES), 1)
    seg_mask = _segment_mask()

    def group(base):
        xbs = [x.astype(BF16) for x in
               _swap8([x_ref[pl.ds(base, SUBLANES), s * LANES:(s + 1) * LANES] for s in range(SUBLANES)])]
        cols = []
        for c in range(ne // EXPERTS_PER_DOT):
            zs = [jnp.where(seg_mask, _dot_nt(xbs[tt], _gather_tiles(tab_ref, r_ref, base + tt, c)), 0.0)
                  for tt in range(SUBLANES)]
            for v in range(cw // LANES):
                col = _sublane_sums([z[:, v * LANES:(v + 1) * LANES] for z in zs], lows)
                for dist in (2, 4, 8):
                    up = pltpu.roll(col, LANES - dist, axis=1)
                    dn = pltpu.roll(col, dist, axis=1)
                    col = col + jnp.where((lane & dist) == 0, up, dn)
                cols.append(col)
        d = jnp.concatenate(cols, axis=1)
        gelu = 0.5 * d * (1.0 + lax.erf(d * (2.0 ** -0.5)))
        act_ref[pl.ds(base, SUBLANES), :] = gelu * gx_ref[pl.ds(base, SUBLANES), :]

    def groups(gi, carry):
        for g in range(U_GROUPS_PER_ITER):
            group(pl.multiple_of((gi * U_GROUPS_PER_ITER + g) * SUBLANES, SUBLANES))
        return carry

    lax.fori_loop(0, tb // (SUBLANES * U_GROUPS_PER_ITER), groups, 0)


def _peer_u(rows, x, gx, tab, tb):
    T, ne = rows.shape
    assert x.shape[1] == SUBLANES * LANES
    wide = pl.BlockSpec((tb, gx.shape[1]), lambda i: (i, 0))
    return pl.pallas_call(
        _peer_u_kernel,
        grid=(T // tb,),
        in_specs=[pl.BlockSpec((tb, ne), lambda i: (i, 0), memory_space=pltpu.SMEM),
                  pl.BlockSpec((tb, x.shape[1]), lambda i: (i, 0)),
                  wide,
                  pl.BlockSpec(tab.shape, lambda i: (0, 0, 0), pipeline_mode=pl.Buffered(1))],
        out_specs=wide,
        out_shape=jax.ShapeDtypeStruct(gx.shape, F32),
        compiler_params=_params(("arbitrary",), 48),
        name="peer_u",
    )(rows, x, gx, tab)


def _peer_v_kernel(r_ref, a_ref, x1_ref, g2_ref, fw_ref, tab_ref, y_ref):
    tb = y_ref.shape[0]
    ne = r_ref.shape[1]
    cw = EXPERTS_PER_DOT * TILE_ROWS
    seg_mask = _segment_mask()
    per_row = g2_ref.shape[1] > 1

    def group(gi, carry):
        base = pl.multiple_of(gi * SUBLANES, SUBLANES)
        rows8 = pl.ds(base, SUBLANES)
        a8 = a_ref[rows8, :]
        accs = []
        for tt in range(SUBLANES):
            acc = jnp.zeros((SUBLANES, LANES), F32)
            for c in range(ne // EXPERTS_PER_DOT):
                lhs = jnp.where(seg_mask, a8[tt:tt + 1, c * cw:(c + 1) * cw], 0.0).astype(BF16)
                acc = acc + _dot(lhs, _gather_tiles(tab_ref, r_ref, base + tt, c))
            accs.append(acc)
        peer = jnp.concatenate(_swap8(accs), axis=1)
        g2 = g2_ref[0, rows8, :] if per_row else g2_ref[0]
        y_ref[rows8, :] = _rms(x1_ref[rows8, :] + g2 * peer) * fw_ref[...]
        return carry

    lax.fori_loop(0, tb // SUBLANES, group, 0)


def _peer_v(rows, act, x1, g2, fw, tab, tb, tiles_per_seq):
    T, ne = rows.shape
    D = x1.shape[1]
    assert D == SUBLANES * LANES
    if g2.shape[1] == 1:
        g2_spec = pl.BlockSpec((1, 1, D), lambda i: (i // tiles_per_seq, 0, 0))
    else:
        g2_spec = pl.BlockSpec((1, tb, D), lambda i: (i // tiles_per_seq, i % tiles_per_seq, 0))
    row = pl.BlockSpec((tb, D), lambda i: (i, 0))
    return pl.pallas_call(
        _peer_v_kernel,
        grid=(T // tb,),
        in_specs=[pl.BlockSpec((tb, ne), lambda i: (i, 0), memory_space=pltpu.SMEM),
                  pl.BlockSpec((tb, act.shape[1]), lambda i: (i, 0)),
                  row, g2_spec, pl.BlockSpec((1, D), lambda i: (0, 0)),
                  pl.BlockSpec(tab.shape, lambda i: (0, 0, 0), pipeline_mode=pl.Buffered(1))],
        out_specs=row,
        out_shape=jax.ShapeDtypeStruct((T, D), F32),
        compiler_params=_params(("arbitrary",), 48),
        name="peer_v",
    )(rows, act, x1, g2, fw.reshape(1, D).astype(F32), tab)


TILES = (256, 512, 256, 256, 64)


def _post_mixers(og, of, x, g1, sh2, sc2, g2, wts, tm, rows_per_seq, tl, tb):
    wg, wf, wq, subkeys, tab_u, tab_v, fw = wts
    x1, h2, qp = _outproj(og, of, x, g1, sh2, sc2, wg, wf, wq, tm, rows_per_seq // tm)
    rows, gx = _topk(qp, subkeys, tl)
    act = _peer_u(rows, h2, gx, tab_u, tb)
    return _peer_v(rows, act, x1, g2, fw, tab_v, tb, rows_per_seq // tb)


def _row_to_col(row, n):
    r = lax.broadcasted_iota(I32, (n, n), 0)
    c = lax.broadcasted_iota(I32, (n, n), 1)
    return jnp.sum(jnp.where(r == c, row, 0.0), axis=1, keepdims=True)


def _gdn_step_kernel(u_ref, z_ref, sm_ref, cs_ref, s0_ref, cw_ref, nw_ref, o_ref, cs_out_ref, s_out_ref):
    u = u_ref[0]
    st = cs_ref[0]
    cw = cw_ref[...]
    conv = cw[CONV_WIDTH - 1:CONV_WIDTH] * u
    for j in range(CONV_WIDTH - 1):
        conv = conv + cw[j:j + 1] * st[j:j + 1]
    cs_out_ref[0] = jnp.concatenate([st[1:CONV_WIDTH - 1], u], axis=0)
    a = _silu(conv)
    z = z_ref[0]
    small = sm_ref[0]
    low = lax.broadcasted_iota(I32, (1, LANES), 1) < HEAD_DIM
    nw = nw_ref[...][:, :HEAD_DIM]
    outs = []
    for h in range(GDN_HEADS):
        cq = _head_block(a, 0, h, low)[:, :HEAD_DIM]
        ck = _head_block(a, GDN_WIDTH, h, low)[:, :HEAD_DIM]
        v = _head_block(a, 2 * GDN_WIDTH, h, low)[:, :HEAD_DIM]
        q = cq * lax.rsqrt(jnp.sum(cq * cq, axis=1, keepdims=True) + 1e-6) * (HEAD_DIM ** -0.5)
        k = ck * lax.rsqrt(jnp.sum(ck * ck, axis=1, keepdims=True) + 1e-6)
        g = small[:, h:h + 1]
        beta = small[:, GDN_HEADS + h:GDN_HEADS + h + 1]
        eg = jnp.exp(g)
        s0 = s0_ref[0, h]
        w_col = _row_to_col(k * (beta * eg), HEAD_DIM)
        vn = v * beta - jnp.sum(w_col * s0, axis=0, keepdims=True)
        q_col = _row_to_col(q * eg, HEAD_DIM)
        o = jnp.sum(q_col * s0, axis=0, keepdims=True) + jnp.sum(q * k, axis=1, keepdims=True) * vn
        s_out_ref[0, h] = s0 * eg + _row_to_col(k, HEAD_DIM) * vn
        zh = _head_block(z, 0, h, low)[:, :HEAD_DIM]
        outs.append(_rms(o) * nw * _silu(zh))
    o_ref[0] = jnp.concatenate(outs, axis=1).astype(o_ref.dtype)


def _gdn_step(gqkv, z, small, state_conv, state_delta, conv_w, norm_w):
    nb = gqkv.shape[0]
    W3 = 3 * GDN_WIDTH
    nw = jnp.concatenate([norm_w.astype(F32), jnp.zeros((LANES - HEAD_DIM,), F32)]).reshape(1, LANES)
    per_b = lambda *shape: pl.BlockSpec((1,) + shape, lambda b: (b,) + (0,) * len(shape))
    return pl.pallas_call(
        _gdn_step_kernel,
        grid=(nb,),
        in_specs=[per_b(1, W3), per_b(1, GDN_WIDTH), per_b(1, SMALL_W), per_b(CONV_WIDTH - 1, W3),
                  per_b(GDN_HEADS, HEAD_DIM, HEAD_DIM),
                  pl.BlockSpec((CONV_WIDTH, W3), lambda b: (0, 0)), pl.BlockSpec((1, LANES), lambda b: (0, 0))],
        out_specs=[per_b(1, GDN_WIDTH), per_b(CONV_WIDTH - 1, W3), per_b(GDN_HEADS, HEAD_DIM, HEAD_DIM)],
        out_shape=[jax.ShapeDtypeStruct((nb, 1, GDN_WIDTH), BF16),
                   jax.ShapeDtypeStruct((nb, CONV_WIDTH - 1, W3), F32),
                   jax.ShapeDtypeStruct((nb, GDN_HEADS, HEAD_DIM, HEAD_DIM), F32)],
        compiler_params=_params(("arbitrary",)),
        name="gdn_step",
    )(gqkv.reshape(nb, 1, W3), z.reshape(nb, 1, GDN_WIDTH), small.reshape(nb, 1, SMALL_W), state_conv, state_delta,
      conv_w, nw)


def _fox_decode_kernel(pt_ref, q_ref, kn_ref, vn_ref, sm_ref, *refs):
    page_refs = refs[:-6]
    o_ref, q_s, m_ref, l_ref, acc_ref, carry_ref = refs[-6:]
    p = pl.program_id(1)
    H = FOX_HEADS
    W = FOX_WIDTH
    ps = page_refs[0].shape[3]
    diag = lax.broadcasted_iota(I32, (H, W), 1) // HEAD_DIM == lax.broadcasted_iota(I32, (H, W), 0)
    block_diag = lambda a: jnp.where(diag, jnp.concatenate([a] * H, axis=1), 0.0)

    @pl.when(p == 0)
    def _():
        q = (q_ref[0] * (HEAD_DIM ** -0.5)).astype(BF16)
        q_s[...] = block_diag(q.astype(F32)).astype(BF16)
        m_ref[...] = jnp.sum(q.astype(F32) * kn_ref[0].astype(BF16).astype(F32), axis=1, keepdims=True)
        l_ref[...] = jnp.ones_like(l_ref)
        acc_ref[...] = block_diag(vn_ref[0].astype(BF16).astype(F32))
        lane = lax.broadcasted_iota(I32, (H, SMALL_W), 1)
        row = lax.broadcasted_iota(I32, (H, SMALL_W), 0)
        carry_ref[...] = jnp.sum(jnp.where(lane == row + 2 * GDN_HEADS, sm_ref[0], 0.0), axis=1, keepdims=True)

    j = lax.broadcasted_iota(I32, (ps, ps), 0)
    pos = lax.broadcasted_iota(I32, (ps, ps), 1)
    later = (j > pos).astype(F32)
    npages = len(page_refs) // 3
    k_refs, v_refs, lf_refs = page_refs[:npages], page_refs[npages:2 * npages], page_refs[2 * npages:]
    suffix = [_dot(r[0], later, precision=HIGHEST) for r in lf_refs]
    carry = carry_ref[...]
    ss = []
    for i in range(npages):
        kt = k_refs[i][0].reshape(W, ps).astype(BF16)
        ss.append(_dot(q_s[...], kt) + (suffix[i] + carry))
        carry = carry + jnp.sum(lf_refs[i][0], axis=1, keepdims=True)
    carry_ref[...] = carry
    m_old = m_ref[...]
    m_new = m_old
    for s in ss:
        m_new = jnp.maximum(m_new, jnp.max(s, axis=1, keepdims=True))
    alpha = jnp.exp(m_old - m_new)
    pms = [jnp.exp(s - m_new) for s in ss]
    l_new = l_ref[...] * alpha
    acc = acc_ref[...] * alpha
    for i in range(npages):
        l_new = l_new + jnp.sum(pms[i], axis=1, keepdims=True)
        acc = acc + _dot_nt(pms[i].astype(BF16), v_refs[i][0].reshape(W, ps).astype(BF16))
    l_ref[...] = l_new
    acc_ref[...] = acc
    m_ref[...] = m_new

    @pl.when(p == pl.num_programs(1) - 1)
    def _():
        out = jnp.sum(jnp.where(diag, acc_ref[...] / l_ref[...], 0.0), axis=0, keepdims=True)
        o_ref[0] = _spread_heads(out).astype(o_ref.dtype)


def _fox_decode(page_table, fq, fk, fv, small, cache_k, cache_v, cache_logf):
    nb, npg = page_table.shape
    npool, ps = cache_k.shape[:2]
    H = FOX_HEADS
    W = FOX_WIDTH
    P = PAGES_PER_STEP
    k_t = jnp.transpose(cache_k, (0, 2, 3, 1))
    v_t = jnp.transpose(cache_v, (0, 2, 3, 1))
    lf_t = jnp.transpose(cache_logf, (0, 2, 1))
    assert npg % P == 0
    per_b = lambda *shape: pl.BlockSpec((1,) + shape, lambda b, p, pt: (b,) + (0,) * len(shape))

    def pages(*shape):
        return [pl.BlockSpec((1,) + shape, lambda b, p, pt, i=i: (pt[b, npg - 1 - (p * P + i)],) + (0,) * len(shape))
                for i in range(P)]

    heads = lambda a: a.reshape(nb, H, HEAD_DIM)
    return pl.pallas_call(
        _fox_decode_kernel,
        grid_spec=pltpu.PrefetchScalarGridSpec(
            num_scalar_prefetch=1,
            grid=(nb, npg // P),
            in_specs=[per_b(H, HEAD_DIM), per_b(H, HEAD_DIM), per_b(H, HEAD_DIM), per_b(1, SMALL_W)]
            + pages(H, HEAD_DIM, ps) + pages(H, HEAD_DIM, ps) + pages(H, ps),
            out_specs=per_b(1, H * LANES),
            scratch_shapes=[pltpu.VMEM((H, W), BF16), pltpu.VMEM((H, 1), F32), pltpu.VMEM((H, 1), F32),
                            pltpu.VMEM((H, W), F32), pltpu.VMEM((H, 1), F32)]),
        out_shape=jax.ShapeDtypeStruct((nb, 1, H * LANES), BF16),
        compiler_params=_params(("arbitrary", "arbitrary"), 48),
        name="fox_decode",
    )(page_table, heads(fq), heads(fk), heads(fv), small.reshape(nb, 1, SMALL_W),
      *([k_t] * P + [v_t] * P + [lf_t] * P))


def kernel(x_prompt, x_sample, cache_k, cache_v, cache_logf, state_conv, state_delta, page_table, c_prompt, c_sample, w_mod, b_mod, w_in, conv_w, a_log, dt_bias, gdn_norm_w, fox_fb, w_out, peer_wq, peer_subkeys, peer_u, peer_v, final_norm_w):
    assert w_mod.shape[0] == 1, "one layer"
    D = x_prompt.shape[-1]
    Bp, L = x_prompt.shape[:2]
    Bs = x_sample.shape[0]
    assert x_sample.shape[1] == 1 and L % GDN_CHUNK == 0 and Bs % SUBLANES == 0
    tm_in, tq, tm_out, tl, tb = (min(t, L) for t in TILES)

    c = jnp.concatenate([c_prompt, c_sample], axis=0)
    mod = _mod(c, w_mod[0], b_mod[0])
    mod_p = [mod[:Bp, None, j * D:(j + 1) * D] for j in range(N_MOD)]
    pad_s = (-Bs) % LANES
    mod_s = [mod[None, Bp:, j * D:(j + 1) * D] for j in range(N_MOD)]
    mod_s_pad = [jnp.pad(m, ((0, 0), (0, pad_s), (0, 0))) for m in mod_s]

    w_cat, padd, alog = _prep_inproj(w_in[0], a_log[0], dt_bias[0], fox_fb[0])
    wg, wf, wq = _prep_outproj(w_out[0], peer_wq[0])
    subkeys = peer_subkeys[0].reshape(PEER_HEADS * 2, N_KEYS, PEER_DK_HALF).astype(BF16)
    wts = (wg, wf, wq, subkeys, _pack_table(peer_u[0]), _pack_table(peer_v[0]), final_norm_w)

    xp = x_prompt.reshape(Bp * L, D)
    gqkv, z, small, fk, fv, qa, ka, va = _inproj(xp, mod_p[0], mod_p[1], w_cat, padd, alog, tm_in, L // tm_in)
    og, delta_p = _gdn_prompt(gqkv, z, small, conv_w[0], gdn_norm_w[0], Bp, L)
    of = _fox_prompt(qa, ka, va, Bp, L, tq)
    y_p = _post_mixers(og, of, xp, mod_p[2], mod_p[3], mod_p[4], mod_p[5], wts, tm_out, L, tl, tb)

    xs = x_sample.reshape(Bs, D)
    gqkv_s, z_s, small_s, fk_s, fv_s, _, _, _, fq_s = _inproj(xs, mod_s[0], mod_s[1], w_cat, padd, alog, Bs, 1,
                                                              with_fq=True)
    og_s, conv_s, delta_s = _gdn_step(gqkv_s, z_s, small_s, state_conv[0], state_delta[0], conv_w[0], gdn_norm_w[0])
    of_s = _fox_decode(page_table, fq_s, fk_s, fv_s, small_s, cache_k[0], cache_v[0], cache_logf[0])
    rows = lambda a: jnp.pad(a.reshape(Bs, -1), ((0, pad_s), (0, 0)))
    ts = Bs + pad_s
    y_s = _post_mixers(rows(og_s), rows(of_s), rows(xs), mod_s_pad[2], mod_s_pad[3], mod_s_pad[4], mod_s_pad[5], wts,
                       ts, ts, ts, min(tb, ts))[:Bs]

    nf = 2 * GDN_HEADS
    return (y_p.reshape(Bp, L, D), y_s.reshape(Bs, 1, D),
            fk.reshape(1, Bp, L, FOX_HEADS, HEAD_DIM), fv.reshape(1, Bp, L, FOX_HEADS, HEAD_DIM),
            small[:, nf:nf + FOX_HEADS].reshape(1, Bp, L, FOX_HEADS),
            gqkv.reshape(Bp, L, -1)[None, :, L - (CONV_WIDTH - 1):], delta_p[None],
            fk_s.reshape(1, Bs, 1, FOX_HEADS, HEAD_DIM), fv_s.reshape(1, Bs, 1, FOX_HEADS, HEAD_DIM),
            small_s[:, nf:nf + FOX_HEADS].reshape(1, Bs, 1, FOX_HEADS), conv_s[None], delta_s[None])
```

```python
import functools

import numpy as np
import jax
import jax.numpy as jnp
from jax import lax
from jax.experimental import pallas as pl
from jax.experimental.pallas import tpu as pltpu

F32 = jnp.float32
BF16 = jnp.bfloat16
I32 = jnp.int32

HEAD_DIM = 64
GDN_HEADS = 8
FOX_HEADS = 8
GDN_WIDTH = GDN_HEADS * HEAD_DIM
FOX_WIDTH = FOX_HEADS * HEAD_DIM
CONV_WIDTH = 4
PEER_HEADS = 8
N_KEYS = 128
PEER_TOPK = 16
PEER_DK_HALF = 128
N_MOD = 6
EPS = 1e-6
LANES = 128
SUBLANES = 8
GDN_CHUNK = 128
SMALL_W = LANES
BF16_TILE_ROWS = 16
TABLE_ROWS_LOG2 = 13
TABLE_ROWS = 1 << TABLE_ROWS_LOG2
TILE_ROWS = BF16_TILE_ROWS
EXPERTS_PER_DOT = 16
U_GROUPS_PER_ITER = 4
PAGES_PER_STEP = 8
TOPK_SETS_PER_STEP = 4
FOX_ROW_SPLIT = 4
HIGHEST = lax.Precision.HIGHEST
NEG_INF = float("-inf")


def _dot(a, b, precision=None):
    return jnp.dot(a, b, preferred_element_type=F32, precision=precision)


def _dot_nt(a, b, precision=None):
    return lax.dot_general(a, b, (((1,), (1,)), ((), ())), preferred_element_type=F32, precision=precision)


def _bdot(a, b):
    return _dot(a.astype(BF16), b.astype(BF16))


def _bdot_nt(a, b):
    return _dot_nt(a.astype(BF16), b.astype(BF16))


def _split3(x):
    hi = x.astype(BF16)
    r1 = x - hi.astype(F32)
    mid = r1.astype(BF16)
    lo = (r1 - mid.astype(F32)).astype(BF16)
    return hi, mid, lo


def _dot3(a, b):
    ah = a.astype(BF16)
    al = (a - ah.astype(F32)).astype(BF16)
    bh = b.astype(BF16)
    bl = (b - bh.astype(F32)).astype(BF16)
    return _dot(ah, bh) + (_dot(ah, bl) + _dot(al, bh))


def _params(sem, vmem_mb=None):
    kw = dict(dimension_semantics=sem)
    if vmem_mb is not None:
        kw["vmem_limit_bytes"] = vmem_mb * 1024 * 1024
    return pltpu.CompilerParams(**kw)


def _rms(x):
    return x * lax.rsqrt(jnp.mean(x * x, axis=-1, keepdims=True) + EPS)


def _silu(x):
    return x * jax.nn.sigmoid(x)


def _mod_kernel(c_ref, w_ref, b_ref, o_ref):
    s = _silu(c_ref[...])
    o_ref[...] = _bdot(s, w_ref[...]) + b_ref[...]


def _mod(c, w, b):
    n, d = c.shape
    nout = w.shape[1]
    tn = 1024
    return pl.pallas_call(
        _mod_kernel,
        grid=(nout // tn,),
        in_specs=[pl.BlockSpec((n, d), lambda j: (0, 0)),
                  pl.BlockSpec((d, tn), lambda j: (0, j)),
                  pl.BlockSpec((1, tn), lambda j: (0, j))],
        out_specs=pl.BlockSpec((n, tn), lambda j: (0, j)),
        out_shape=jax.ShapeDtypeStruct((n, nout), F32),
        compiler_params=_params(("arbitrary",)),
        name="mod",
    )(c, w, b.reshape(1, nout))


def _spread_heads(a):
    tm = a.shape[0]
    lane = lax.broadcasted_iota(I32, (tm, LANES), 1)
    low = lane < HEAD_DIM
    out = []
    for j in range(a.shape[1] // LANES):
        blk = a[:, j * LANES:(j + 1) * LANES]
        out.append(jnp.where(low, blk, 0.0))
        out.append(jnp.where(low, pltpu.roll(blk, HEAD_DIM, axis=1), 0.0))
    return jnp.concatenate(out, axis=1)


def _inproj_kernel(tiles_per_seq, x_ref, sh_ref, sc_ref, w_ref, padd_ref, alog_ref, selq_ref, selk_ref, cq_ref,
                   ck_ref, cv_ref, gqkv_ref, z_ref, small_ref, fk_ref, fv_ref, *rest):
    carry_ref = rest[-1]
    i = pl.program_id(0)
    tm = x_ref.shape[0]
    h = _rms(x_ref[...]) * (1.0 + sc_ref[0]) + sh_ref[0]
    hb = h.astype(BF16)
    o = 0
    gqkv_ref[...] = _dot(hb, w_ref[:, o:o + 3 * GDN_WIDTH]); o += 3 * GDN_WIDTH
    z_ref[...] = _dot(hb, w_ref[:, o:o + GDN_WIDTH]); o += GDN_WIDTH
    sm = _dot(hb, w_ref[:, o:o + SMALL_W]); o += SMALL_W
    fq = _dot(hb, w_ref[:, o:o + FOX_WIDTH]); o += FOX_WIDTH
    fk = _dot(hb, w_ref[:, o:o + FOX_WIDTH]); o += FOX_WIDTH
    fv = _dot(hb, w_ref[:, o:o + FOX_WIDTH])
    fk_ref[...] = fk
    fv_ref[...] = fv

    lane = lax.broadcasted_iota(I32, (tm, SMALL_W), 1)
    y = sm + padd_ref[...]
    t = jnp.log1p(jnp.exp(-jnp.abs(y)))
    softplus = jnp.maximum(y, 0.0) + t
    logsig = jnp.minimum(y, 0.0) - t
    small = jnp.where(lane < GDN_HEADS, -jnp.exp(alog_ref[...]) * softplus,
                      jnp.where(lane < 2 * GDN_HEADS, jax.nn.sigmoid(sm),
                                jnp.where(lane < 2 * GDN_HEADS + FOX_HEADS, logsig, 0.0)))
    small_ref[...] = small
    if len(rest) == 2:
        rest[0][...] = fq
        return
    qa_ref, ka_ref, vat_ref = rest[:3]

    @pl.when(i % tiles_per_seq == 0)
    def _():
        carry_ref[...] = jnp.zeros_like(carry_ref)

    r = lax.broadcasted_iota(I32, (tm, tm), 0)
    c = lax.broadcasted_iota(I32, (tm, tm), 1)
    ltri = (c <= r).astype(F32)
    cum = _dot(ltri, small, precision=HIGHEST) + carry_ref[...]
    carry_ref[...] = cum[tm - 1:tm, :]
    hi, mid, lo = _split3(cum)
    parts = jnp.concatenate([hi, mid, lo], axis=1)
    qa_ref[...] = (_spread_heads(fq * (HEAD_DIM ** -0.5)) + _dot(parts, selq_ref[...]) + cq_ref[...]).astype(BF16)
    ka_ref[...] = (_spread_heads(fk) + _dot(parts, selk_ref[...]) + ck_ref[...]).astype(BF16)
    va = _spread_heads(fv) + cv_ref[...]
    for j in range(va.shape[1] // LANES):
        for t in range(tm // LANES):
            vat_ref[j * LANES:(j + 1) * LANES, t * LANES:(t + 1) * LANES] = (
                va[t * LANES:(t + 1) * LANES, j * LANES:(j + 1) * LANES].T.astype(BF16))


def _inproj_consts():
    selq = np.zeros((3 * SMALL_W, FOX_HEADS * LANES), np.float32)
    selk = np.zeros((3 * SMALL_W, FOX_HEADS * LANES), np.float32)
    cq = np.zeros((1, FOX_HEADS * LANES), np.float32)
    ck = np.zeros((1, FOX_HEADS * LANES), np.float32)
    cv = np.zeros((1, FOX_HEADS * LANES), np.float32)
    for h in range(FOX_HEADS):
        base = h * LANES + HEAD_DIM
        for p in range(3):
            src = p * SMALL_W + 2 * GDN_HEADS + h
            selq[src, base + p] = 1.0
            ck[0, base + p] = 1.0
            selk[src, base + 3 + p] = -1.0
            cq[0, base + 3 + p] = 1.0
        cv[0, base] = 1.0
    return (jnp.asarray(selq, BF16), jnp.asarray(selk, BF16), jnp.asarray(cq), jnp.asarray(ck), jnp.asarray(cv))


def _prep_inproj(w_in, a_log, dt_bias, fox_fb):
    o = np.cumsum((0, GDN_WIDTH, GDN_WIDTH, GDN_WIDTH, GDN_WIDTH, GDN_HEADS, GDN_HEADS, FOX_WIDTH, FOX_WIDTH,
                   FOX_WIDTH, FOX_HEADS)).tolist()
    d = w_in.shape[0]
    nsmall = 2 * GDN_HEADS + FOX_HEADS
    small = jnp.concatenate([w_in[:, o[4]:o[6]], w_in[:, o[9]:o[10]], jnp.zeros((d, SMALL_W - nsmall), w_in.dtype)], 1)
    w_cat = jnp.concatenate([w_in[:, :o[4]], small, w_in[:, o[6]:o[9]]], axis=1).astype(BF16)
    zero = jnp.zeros((GDN_HEADS,), F32)
    tail = jnp.zeros((SMALL_W - nsmall,), F32)
    padd = jnp.concatenate([dt_bias.astype(F32), zero, fox_fb.astype(F32), tail]).reshape(1, SMALL_W)
    alog = jnp.concatenate([a_log.astype(F32), zero, zero, tail]).reshape(1, SMALL_W)
    return w_cat, padd, alog


def _inproj(x, sh, sc, w_cat, padd, pmul, tm, tiles_per_seq, decode=False):
    T, D = x.shape
    rm = sh.shape[1]
    nw = w_cat.shape[1]
    selq, selk, cq, ck, cv = _inproj_consts()
    aw = FOX_HEADS * LANES
    row = lambda w: pl.BlockSpec((tm, w), lambda i: (i, 0))
    const = lambda a: pl.BlockSpec(a.shape, lambda i: (0,) * a.ndim)
    seq = pl.BlockSpec((1, rm, D), lambda i: (i // tiles_per_seq, 0, 0))
    outs = [(3 * GDN_WIDTH, F32), (GDN_WIDTH, F32), (SMALL_W, F32), (FOX_WIDTH, F32), (FOX_WIDTH, F32)]
    outs += [(FOX_WIDTH, F32)] if decode else [(aw, BF16), (aw, BF16)]
    out_specs = [row(w) for w, _ in outs]
    out_shape = [jax.ShapeDtypeStruct((T, w), dt) for w, dt in outs]
    if not decode:
        out_specs.append(pl.BlockSpec((aw, tm), lambda i: (0, i)))
        out_shape.append(jax.ShapeDtypeStruct((aw, T), BF16))
    return pl.pallas_call(
        functools.partial(_inproj_kernel, tiles_per_seq),
        grid=(T // tm,),
        in_specs=[row(D), seq, seq, const(w_cat), const(padd), const(pmul), const(selq), const(selk), const(cq),
                  const(ck), const(cv)],
        out_specs=out_specs,
        out_shape=out_shape,
        scratch_shapes=[pltpu.VMEM((1, SMALL_W), F32)],
        compiler_params=_params(("arbitrary",), 48),
        name="inproj",
    )(x, sh, sc, w_cat, padd, pmul, selq, selk, cq, ck, cv)


def _head_block(a, base, h, low):
    j, odd = divmod(h, 2)
    blk = a[:, base + j * LANES:base + (j + 1) * LANES]
    if odd:
        blk = pltpu.roll(blk, HEAD_DIM, axis=1)
    return jnp.where(low, blk, 0.0)


def _join_heads(heads):
    return jnp.concatenate([heads[2 * j] + pltpu.roll(heads[2 * j + 1], HEAD_DIM, axis=1)
                            for j in range(len(heads) // 2)], axis=1)


def _unit_lower_inverses(a_mats, r, cc):
    n = a_mats[0].shape[0]
    s = SUBLANES
    same = (r // s) == (cc // s)
    eye = jnp.where(r == cc, 1.0, 0.0)
    bs = [jnp.where(same, -a, 0.0) for a in a_mats]
    ts = [eye + b for b in bs]
    b2s = [_dot3(b, b) for b in bs]
    ts = [t + _dot3(t, b2) for t, b2 in zip(ts, b2s)]
    b4s = [_dot3(b2, b2) for b2 in b2s]
    ts = [t + _dot3(t, b4) for t, b4 in zip(ts, b4s)]
    while s < n:
        same2 = (r // (2 * s)) == (cc // (2 * s))
        new = same2 & jnp.logical_not(same)
        xs = [_dot3(t, jnp.where(new, a, 0.0)) for t, a in zip(ts, a_mats)]
        ts = [t - _dot3(x, t) for t, x in zip(ts, xs)]
        same = same2
        s *= 2
    return ts


def _gdn_kernel(u_ref, z_ref, sm_ref, cw_ref, nw_ref, o_ref, s_out_ref, ubuf, s_ref):
    c = pl.program_id(1)
    C = u_ref.shape[0]
    W3 = 3 * GDN_WIDTH

    @pl.when(c == 0)
    def _():
        ubuf[0:SUBLANES, :] = jnp.zeros((SUBLANES, W3), F32)
        s_ref[...] = jnp.zeros_like(s_ref)

    ubuf[SUBLANES:SUBLANES + C, :] = u_ref[...]
    cw = cw_ref[...]
    conv = cw[CONV_WIDTH - 1:CONV_WIDTH] * ubuf[SUBLANES:SUBLANES + C, :]
    for j in range(1, CONV_WIDTH):
        conv = conv + cw[CONV_WIDTH - 1 - j:CONV_WIDTH - j] * ubuf[SUBLANES - j:SUBLANES - j + C, :]
    ubuf[0:SUBLANES, :] = ubuf[C:C + SUBLANES, :]
    a = _silu(conv)
    z = z_ref[...]
    small = sm_ref[...]

    lane = lax.broadcasted_iota(I32, (C, LANES), 1)
    low = lane < HEAD_DIM
    r = lax.broadcasted_iota(I32, (C, C), 0)
    cc = lax.broadcasted_iota(I32, (C, C), 1)
    causal = cc <= r
    strict = cc < r
    g_cum = _dot(causal.astype(F32), small, precision=HIGHEST)
    g_last = g_cum[C - 1:C, :]
    e_g = jnp.exp(g_cum)
    e_gl = jnp.exp(g_last - g_cum)
    e_last = jnp.exp(g_last)
    neg_g_hi = pltpu.roll(-g_cum, HEAD_DIM, axis=1)
    nw = nw_ref[...]
    heads = range(GDN_HEADS)
    col = lambda m, h: m[:, h:h + 1]
    cqs = [_head_block(a, 0, h, low) for h in heads]
    cks = [_head_block(a, GDN_WIDTH, h, low) for h in heads]
    vs = [_head_block(a, 2 * GDN_WIDTH, h, low) for h in heads]
    qs = [cq * lax.rsqrt(jnp.sum(cq * cq, axis=1, keepdims=True) + 1e-6) * (HEAD_DIM ** -0.5) for cq in cqs]
    ks = [ck * lax.rsqrt(jnp.sum(ck * ck, axis=1, keepdims=True) + 1e-6) for ck in cks]
    betas = [col(small, GDN_HEADS + h) for h in heads]
    g_diffs = [_dot_nt(jnp.where(lane == h, g_cum, jnp.where(lane == h + HEAD_DIM, 1.0, 0.0)),
                       jnp.where(lane == h, 1.0, jnp.where(lane == h + HEAD_DIM, neg_g_hi, 0.0)),
                       precision=HIGHEST) for h in heads]
    decays = [jnp.exp(jnp.where(causal, g, NEG_INF)) for g in g_diffs]
    a_mats = [jnp.where(strict, betas[h] * _bdot_nt(ks[h], ks[h]) * decays[h], 0.0) for h in heads]
    rhss = [vs[h] * betas[h] + pltpu.roll(ks[h] * (betas[h] * col(e_g, h)), HEAD_DIM, axis=1) for h in heads]
    invs = _unit_lower_inverses(a_mats, r, cc)
    xs = [_dot3(invs[h], rhss[h]) for h in heads]
    u_mats = [jnp.where(low, x, 0.0) for x in xs]
    w_mats = [jnp.where(low, pltpu.roll(x, HEAD_DIM, axis=1), 0.0) for x in xs]
    a_qks = [_bdot_nt(qs[h], ks[h]) * decays[h] for h in heads]
    s_olds = [s_ref[h] for h in heads]
    vns = [u_mats[h] - _bdot(w_mats[h], s_olds[h]) for h in heads]
    os_ = [_bdot(qs[h] * col(e_g, h), s_olds[h]) + _bdot(a_qks[h], vns[h]) for h in heads]
    for h in heads:
        s_ref[h] = s_olds[h] * col(e_last, h) + _bdot((ks[h] * col(e_gl, h)).T, vns[h])
    outs = []
    for h in heads:
        o = os_[h]
        ms = jnp.sum(o * o, axis=1, keepdims=True) * (1.0 / HEAD_DIM)
        outs.append(o * lax.rsqrt(ms + EPS) * nw * _silu(_head_block(z, 0, h, low)))
    o_ref[...] = _join_heads(outs).astype(o_ref.dtype)

    @pl.when(c == pl.num_programs(1) - 1)
    def _():
        s_out_ref[0] = s_ref[:, 0:HEAD_DIM, 0:HEAD_DIM]


def _gdn_prompt(gqkv, z, small, conv_w, norm_w, nb, L):
    C = GDN_CHUNK
    nch = L // C
    W3 = 3 * GDN_WIDTH
    nw = jnp.concatenate([norm_w.astype(F32), jnp.zeros((LANES - HEAD_DIM,), F32)]).reshape(1, LANES)
    row = lambda w: pl.BlockSpec((C, w), lambda b, c: (b * nch + c, 0))
    return pl.pallas_call(
        _gdn_kernel,
        grid=(nb, nch),
        in_specs=[row(W3), row(GDN_WIDTH), row(SMALL_W),
                  pl.BlockSpec((CONV_WIDTH, W3), lambda b, c: (0, 0)),
                  pl.BlockSpec((1, LANES), lambda b, c: (0, 0))],
        out_specs=[row(GDN_WIDTH),
                   pl.BlockSpec((1, GDN_HEADS, HEAD_DIM, HEAD_DIM), lambda b, c: (b, 0, 0, 0))],
        out_shape=[jax.ShapeDtypeStruct((nb * L, GDN_WIDTH), BF16),
                   jax.ShapeDtypeStruct((nb, GDN_HEADS, HEAD_DIM, HEAD_DIM), F32)],
        scratch_shapes=[pltpu.VMEM((SUBLANES + C, W3), F32), pltpu.VMEM((GDN_HEADS, LANES, LANES), F32)],
        compiler_params=_params(("arbitrary", "arbitrary")),
        name="gdn_prompt",
    )(gqkv, z, small, conv_w, nw)


def _fox_kernel(qt_ref, kt_ref, q_ref, k_ref, vt_ref, o_ref, m_ref, acc_ref):
    p = pl.program_id(2)
    qi = qt_ref[p]
    ki = kt_ref[p]
    tq = q_ref.shape[0]
    tk = k_ref.shape[0]

    @pl.when(ki == 0)
    def _():
        m_ref[...] = jnp.full_like(m_ref, NEG_INF)
        acc_ref[...] = jnp.zeros_like(acc_ref)

    def step(on_diagonal):
        nsplit = min(FOX_ROW_SPLIT, tq // LANES)
        tr = tq // nsplit
        blocks = [slice(i * tr, (i + 1) * tr) for i in range(nsplit)]
        k = k_ref[...]
        vt = vt_ref[...]
        ss = [_dot_nt(k, q_ref[b, :]) for b in blocks]
        if on_diagonal:
            key = lax.broadcasted_iota(I32, (tk, tr), 0)
            qry = lax.broadcasted_iota(I32, (tk, tr), 1)
            ss = [jnp.where(key <= qry + i * tr, s, NEG_INF) for i, s in enumerate(ss)]
        m_olds = [m_ref[:, b] for b in blocks]
        m_news = [jnp.maximum(m, jnp.max(s, axis=0, keepdims=True)) for m, s in zip(m_olds, ss)]
        ps = [jnp.exp(s - m).astype(BF16) for s, m in zip(ss, m_news)]
        for b, m_old, m_new, p in zip(blocks, m_olds, m_news, ps):
            acc_ref[:, b] = acc_ref[:, b] * jnp.exp(m_old - m_new) + _dot(vt, p)
            m_ref[:, b] = m_new

    @pl.when(ki < qi)
    def _():
        step(False)

    @pl.when(ki == qi)
    def _():
        step(True)
        acc = acc_ref[...]
        out = acc / acc[HEAD_DIM:HEAD_DIM + 1, :]
        for j in range(tq // LANES):
            o_ref[j * LANES:(j + 1) * LANES, :] = out[:, j * LANES:(j + 1) * LANES].T.astype(o_ref.dtype)


def _fox_prompt(qa, ka, vat, nb, L, tq):
    nq = L // tq
    assert tq % LANES == 0
    pairs = [(i, j) for i in range(nq) for j in range(i + 1)]
    qt = jnp.asarray([p[0] for p in pairs], I32)
    kt = jnp.asarray([p[1] for p in pairs], I32)
    qspec = pl.BlockSpec((tq, LANES), lambda b, h, p, qt, kt: (b * nq + qt[p], h))
    kspec = pl.BlockSpec((tq, LANES), lambda b, h, p, qt, kt: (b * nq + kt[p], h))
    vspec = pl.BlockSpec((LANES, tq), lambda b, h, p, qt, kt: (h, b * nq + kt[p]))
    return pl.pallas_call(
        _fox_kernel,
        grid_spec=pltpu.PrefetchScalarGridSpec(
            num_scalar_prefetch=2,
            grid=(nb, FOX_HEADS, len(pairs)),
            in_specs=[qspec, kspec, vspec],
            out_specs=qspec,
            scratch_shapes=[pltpu.VMEM((1, tq), F32), pltpu.VMEM((LANES, tq), F32)]),
        out_shape=jax.ShapeDtypeStruct(qa.shape, BF16),
        compiler_params=_params(("arbitrary", "arbitrary", "arbitrary")),
        name="fox_prompt",
    )(qt, kt, qa, ka, vat)


def _outproj_kernel(og_ref, of_ref, x_ref, g1_ref, sh_ref, sc_ref, wg_ref, wf_ref, wq_ref, x1_ref, h2_ref, qp_ref):
    m = _dot(og_ref[...], wg_ref[...]) + _dot(of_ref[...], wf_ref[...])
    x1 = x_ref[...] + g1_ref[0] * m
    x1_ref[...] = x1
    h2 = _rms(x1) * (1.0 + sc_ref[0]) + sh_ref[0]
    h2_ref[...] = h2
    qp_ref[...] = _dot(h2.astype(BF16), wq_ref[...]).astype(BF16)


def _prep_outproj(w_out, peer_wq):
    wg = w_out[:GDN_WIDTH].astype(BF16)
    wf = w_out[GDN_WIDTH:].reshape(FOX_HEADS, HEAD_DIM, -1)
    wf = jnp.pad(wf, ((0, 0), (0, LANES - HEAD_DIM), (0, 0))).reshape(FOX_HEADS * LANES, -1).astype(BF16)
    return wg, wf, peer_wq.astype(BF16)


def _outproj(og, of, x, g1, sh2, sc2, wg, wf, wq, tm, tiles_per_seq):
    T, D = x.shape
    rm = g1.shape[1]
    nq = wq.shape[1]
    row = lambda w: pl.BlockSpec((tm, w), lambda i: (i, 0))
    const = lambda a: pl.BlockSpec(a.shape, lambda i: (0,) * a.ndim)
    seq = pl.BlockSpec((1, rm, D), lambda i: (i // tiles_per_seq, 0, 0))
    return pl.pallas_call(
        _outproj_kernel,
        grid=(T // tm,),
        in_specs=[row(og.shape[1]), row(of.shape[1]), row(D), seq, seq, seq, const(wg), const(wf), const(wq)],
        out_specs=[row(D), row(D), row(nq)],
        out_shape=[jax.ShapeDtypeStruct((T, D), F32), jax.ShapeDtypeStruct((T, D), F32),
                   jax.ShapeDtypeStruct((T, nq), BF16)],
        compiler_params=_params(("arbitrary",), 48),
        name="outproj",
    )(og, of, x, g1, sh2, sc2, wg, wf, wq)


def _topk_rows(ss, k):
    n = ss[0].shape[0]
    iota_n = lax.broadcasted_iota(I32, ss[0].shape, 0)
    vals = [[] for _ in ss]
    idxs = [[] for _ in ss]
    for _ in range(k):
        ms = [jnp.max(s, axis=0, keepdims=True) for s in ss]
        ids = [jnp.min(jnp.where(s == m, iota_n, n), axis=0, keepdims=True) for s, m in zip(ss, ms)]
        ss = [jnp.where(iota_n == i, NEG_INF, s) for s, i in zip(ss, ids)]
        for j, (m, i) in enumerate(zip(ms, ids)):
            vals[j].append(m)
            idxs[j].append(i)
    return [jnp.concatenate(v, axis=0) for v in vals], [jnp.concatenate(i, axis=0) for i in idxs]


def _pair_rows(a0, a1, op):
    k = PEER_TOPK
    rows = [op(a0[0:1], a1[0:SUBLANES]), op(a0[0:1], a1[SUBLANES:k])]
    rows += [op(a0[a:a + 1], a1[0:SUBLANES]) for a in range(1, SUBLANES)]
    rows.append(op(a0[SUBLANES:k], a1[0:1]))
    return jnp.concatenate(rows, axis=0)


def _topk_kernel(q_ref, sk_ref, r_ref, gx_ref, sv_ref, si_ref):
    hp = pl.program_id(1)
    tl = q_ref.shape[0]
    k = PEER_TOPK
    nsub = sk_ref.shape[0]
    dk = sk_ref.shape[2]
    ss = [_dot_nt(sk_ref[i], q_ref[:, i * dk:(i + 1) * dk]) for i in range(nsub)]
    vals, idxs = _topk_rows(ss, k)
    for i in range(nsub):
        sv_ref[hp * nsub + i] = vals[i]
        si_ref[hp * nsub + i] = idxs[i]

    @pl.when(hp == pl.num_programs(1) - 1)
    def _():
        nrow = 10 * SUBLANES
        ridx = lax.broadcasted_iota(I32, (nrow, tl), 0)
        blk = ridx // SUBLANES
        w = ridx % SUBLANES
        flat = jnp.where(blk == 0, w, jnp.where(blk == 1, SUBLANES + w,
                         jnp.where(blk <= SUBLANES, k * (blk - 1) + w, k * (SUBLANES + w))))
        heads = range(PEER_HEADS)
        cands = [_pair_rows(sv_ref[2 * h], sv_ref[2 * h + 1], lambda a, b: a + b) for h in heads]
        ecands = [_pair_rows(si_ref[2 * h], si_ref[2 * h + 1], lambda a, b: a * N_KEYS + b) for h in heads]
        fvs = [[] for _ in heads]
        es = [[] for _ in heads]
        for _ in range(k):
            ms = [jnp.max(c, axis=0, keepdims=True) for c in cands]
            fs = [jnp.min(jnp.where(c == m, flat, k * k), axis=0, keepdims=True) for c, m in zip(cands, ms)]
            sels = [flat == f for f in fs]
            for h in heads:
                es[h].append(jnp.max(jnp.where(sels[h], ecands[h], -1), axis=0, keepdims=True))
                fvs[h].append(ms[h])
            cands = [jnp.where(sel, NEG_INF, c) for sel, c in zip(sels, cands)]
        g_all = []
        for h in heads:
            fv = jnp.concatenate(fvs[h], axis=0)
            ex = jnp.exp(fv - fv[0:1])
            g_all.append(ex / jnp.sum(ex, axis=0, keepdims=True))
        e_mat = jnp.concatenate([jnp.concatenate(e, axis=0) for e in es], axis=0)
        g_mat = jnp.concatenate(g_all, axis=0)
        row_mat = (e_mat & (TABLE_ROWS - 1)).astype(F32)
        half_mat = e_mat >> TABLE_ROWS_LOG2
        for j in range(tl // LANES):
            tok = slice(j * LANES, (j + 1) * LANES)
            r_ref[tok, :] = row_mat[:, tok].T.astype(I32)
        par = lax.broadcasted_iota(I32, (TILE_ROWS, tl), 0) % 2
        for i in range(PEER_HEADS * k // SUBLANES):
            rows = [jnp.where(par == half_mat[n:n + 1], g_mat[n:n + 1], 0.0)
                    for n in range(i * SUBLANES, (i + 1) * SUBLANES)]
            blk = jnp.concatenate(rows, axis=0)
            for j in range(tl // LANES):
                tok = slice(j * LANES, (j + 1) * LANES)
                gx_ref[tok, i * LANES:(i + 1) * LANES] = blk[:, tok].T


def _topk(qp, subkeys, tl):
    T = qp.shape[0]
    nhp = subkeys.shape[0]
    ne = PEER_HEADS * PEER_TOPK
    out = lambda w: pl.BlockSpec((tl, w), lambda i, hp: (i, 0))
    return pl.pallas_call(
        _topk_kernel,
        grid=(T // tl, nhp // TOPK_SETS_PER_STEP),
        in_specs=[pl.BlockSpec((tl, TOPK_SETS_PER_STEP * PEER_DK_HALF), lambda i, hp: (i, hp)),
                  pl.BlockSpec((TOPK_SETS_PER_STEP, N_KEYS, PEER_DK_HALF), lambda i, hp: (hp, 0, 0))],
        out_specs=[out(ne), out(ne * TILE_ROWS)],
        out_shape=[jax.ShapeDtypeStruct((T, ne), I32), jax.ShapeDtypeStruct((T, ne * TILE_ROWS), F32)],
        scratch_shapes=[pltpu.VMEM((nhp, PEER_TOPK, tl), F32), pltpu.VMEM((nhp, PEER_TOPK, tl), I32)],
        compiler_params=_params(("arbitrary", "arbitrary")),
        name="peer_topk",
    )(qp, subkeys)


def _pack_table(w):
    E, D = w.shape
    assert E == 2 * TABLE_ROWS and D == (TILE_ROWS // 2) * LANES
    t = w.astype(BF16).reshape(2, TABLE_ROWS, D // LANES, LANES)
    return jnp.transpose(t, (1, 2, 0, 3)).reshape(TABLE_ROWS, TILE_ROWS, LANES)


def _gather_tiles(tab_ref, r_ref, t, c):
    return jnp.concatenate([tab_ref[r_ref[t, c * EXPERTS_PER_DOT + j]] for j in range(EXPERTS_PER_DOT)], axis=0)


def _fold(x, y, d, low):
    return jnp.where(low, x, y) + pltpu.roll(jnp.where(low, y, x), SUBLANES - d, axis=0)


def _sublane_sums(vs, lows):
    for d, low in zip((1, 2, 4), lows):
        vs = [_fold(vs[2 * i], vs[2 * i + 1], d, low) for i in range(len(vs) // 2)]
    return vs[0]


def _swap8(vs):
    sub = lax.broadcasted_iota(I32, (SUBLANES, LANES), 0)
    for d in (4, 2, 1):
        low = (sub & d) == 0
        out = list(vs)
        for i in range(SUBLANES):
            if i & d == 0:
                out[i] = jnp.where(low, vs[i], pltpu.roll(vs[i + d], d, axis=0))
                out[i + d] = jnp.where(low, pltpu.roll(vs[i], SUBLANES - d, axis=0), vs[i + d])
        vs = out
    return vs


def _segment_mask():
    cw = EXPERTS_PER_DOT * TILE_ROWS
    sub = lax.broadcasted_iota(I32, (SUBLANES, cw), 0)
    lane = lax.broadcasted_iota(I32, (SUBLANES, cw), 1)
    return (lane % TILE_ROWS) // 2 == sub


def _peer_u_kernel(r_ref, x_ref, gx_ref, tab_ref, act_ref):
    tb = x_ref.shape[0]
    ne = r_ref.shape[1]
    cw = EXPERTS_PER_DOT * TILE_ROWS
    sub = lax.broadcasted_iota(I32, (SUBLANES, LANES), 0)
    lows = [(sub & d) == 0 for d in (1, 2, 4)]
    lane = lax.broadcasted_iota(I32, (SUBLANES, LANES), 1)
    seg_mask = _segment_mask()

    def group(base):
        xbs = [x.astype(BF16) for x in
               _swap8([x_ref[pl.ds(base, SUBLANES), s * LANES:(s + 1) * LANES] for s in range(SUBLANES)])]
        cols = []
        for c in range(ne // EXPERTS_PER_DOT):
            zs = [jnp.where(seg_mask, _dot_nt(xbs[tt], _gather_tiles(tab_ref, r_ref, base + tt, c)), 0.0)
                  for tt in range(SUBLANES)]
            for v in range(cw // LANES):
                col = _sublane_sums([z[:, v * LANES:(v + 1) * LANES] for z in zs], lows)
                for dist in (2, 4, 8):
                    up = pltpu.roll(col, LANES - dist, axis=1)
                    dn = pltpu.roll(col, dist, axis=1)
                    col = col + jnp.where((lane & dist) == 0, up, dn)
                cols.append(col)
        d = jnp.concatenate(cols, axis=1)
        gelu = 0.5 * d * (1.0 + lax.erf(d * (2.0 ** -0.5)))
        act_ref[pl.ds(base, SUBLANES), :] = gelu * gx_ref[pl.ds(base, SUBLANES), :]

    def groups(gi, carry):
        for g in range(U_GROUPS_PER_ITER):
            group(pl.multiple_of((gi * U_GROUPS_PER_ITER + g) * SUBLANES, SUBLANES))
        return carry

    lax.fori_loop(0, tb // (SUBLANES * U_GROUPS_PER_ITER), groups, 0)


def _peer_u(rows, x, gx, tab, tb):
    T, ne = rows.shape
    assert x.shape[1] == SUBLANES * LANES
    wide = pl.BlockSpec((tb, gx.shape[1]), lambda i: (i, 0))
    return pl.pallas_call(
        _peer_u_kernel,
        grid=(T // tb,),
        in_specs=[pl.BlockSpec((tb, ne), lambda i: (i, 0), memory_space=pltpu.SMEM),
                  pl.BlockSpec((tb, x.shape[1]), lambda i: (i, 0)),
                  wide,
                  pl.BlockSpec(tab.shape, lambda i: (0, 0, 0), pipeline_mode=pl.Buffered(1))],
        out_specs=wide,
        out_shape=jax.ShapeDtypeStruct(gx.shape, F32),
        compiler_params=_params(("arbitrary",), 48),
        name="peer_u",
    )(rows, x, gx, tab)


def _peer_v_kernel(r_ref, a_ref, x1_ref, g2_ref, fw_ref, tab_ref, y_ref):
    tb = y_ref.shape[0]
    ne = r_ref.shape[1]
    cw = EXPERTS_PER_DOT * TILE_ROWS
    seg_mask = _segment_mask()
    per_row = g2_ref.shape[1] > 1

    def group(gi, carry):
        base = pl.multiple_of(gi * SUBLANES, SUBLANES)
        rows8 = pl.ds(base, SUBLANES)
        a8 = a_ref[rows8, :]
        accs = []
        for tt in range(SUBLANES):
            acc = jnp.zeros((SUBLANES, LANES), F32)
            for c in range(ne // EXPERTS_PER_DOT):
                lhs = jnp.where(seg_mask, a8[tt:tt + 1, c * cw:(c + 1) * cw], 0.0).astype(BF16)
                acc = acc + _dot(lhs, _gather_tiles(tab_ref, r_ref, base + tt, c))
            accs.append(acc)
        peer = jnp.concatenate(_swap8(accs), axis=1)
        g2 = g2_ref[0, rows8, :] if per_row else g2_ref[0]
        y_ref[rows8, :] = _rms(x1_ref[rows8, :] + g2 * peer) * fw_ref[...]
        return carry

    lax.fori_loop(0, tb // SUBLANES, group, 0)


def _peer_v(rows, act, x1, g2, fw, tab, tb, tiles_per_seq):
    T, ne = rows.shape
    D = x1.shape[1]
    assert D == SUBLANES * LANES
    if g2.shape[1] == 1:
        g2_spec = pl.BlockSpec((1, 1, D), lambda i: (i // tiles_per_seq, 0, 0))
    else:
        g2_spec = pl.BlockSpec((1, tb, D), lambda i: (i // tiles_per_seq, i % tiles_per_seq, 0))
    row = pl.BlockSpec((tb, D), lambda i: (i, 0))
    return pl.pallas_call(
        _peer_v_kernel,
        grid=(T // tb,),
        in_specs=[pl.BlockSpec((tb, ne), lambda i: (i, 0), memory_space=pltpu.SMEM),
                  pl.BlockSpec((tb, act.shape[1]), lambda i: (i, 0)),
                  row, g2_spec, pl.BlockSpec((1, D), lambda i: (0, 0)),
                  pl.BlockSpec(tab.shape, lambda i: (0, 0, 0), pipeline_mode=pl.Buffered(1))],
        out_specs=row,
        out_shape=jax.ShapeDtypeStruct((T, D), F32),
        compiler_params=_params(("arbitrary",), 48),
        name="peer_v",
    )(rows, act, x1, g2, fw.reshape(1, D).astype(F32), tab)


TILES = (256, 1024, 256, 256, 64)


def _post_mixers(og, of, x, g1, sh2, sc2, g2, wts, tm, rows_per_seq, tl, tb):
    wg, wf, wq, subkeys, tab_u, tab_v, fw = wts
    x1, h2, qp = _outproj(og, of, x, g1, sh2, sc2, wg, wf, wq, tm, rows_per_seq // tm)
    rows, gx = _topk(qp, subkeys, tl)
    act = _peer_u(rows, h2, gx, tab_u, tb)
    return _peer_v(rows, act, x1, g2, fw, tab_v, tb, rows_per_seq // tb)


def _row_to_col(row, n):
    r = lax.broadcasted_iota(I32, (n, n), 0)
    c = lax.broadcasted_iota(I32, (n, n), 1)
    return jnp.sum(jnp.where(r == c, row, 0.0), axis=1, keepdims=True)


def _gdn_step_kernel(u_ref, z_ref, sm_ref, cs_ref, s0_ref, cw_ref, nw_ref, o_ref, cs_out_ref, s_out_ref):
    u = u_ref[0]
    st = cs_ref[0]
    cw = cw_ref[...]
    conv = cw[CONV_WIDTH - 1:CONV_WIDTH] * u
    for j in range(CONV_WIDTH - 1):
        conv = conv + cw[j:j + 1] * st[j:j + 1]
    cs_out_ref[0] = jnp.concatenate([st[1:CONV_WIDTH - 1], u], axis=0)
    a = _silu(conv)
    z = z_ref[0]
    small = sm_ref[0]
    low = lax.broadcasted_iota(I32, (1, LANES), 1) < HEAD_DIM
    nw = nw_ref[...][:, :HEAD_DIM]
    outs = []
    for h in range(GDN_HEADS):
        cq = _head_block(a, 0, h, low)[:, :HEAD_DIM]
        ck = _head_block(a, GDN_WIDTH, h, low)[:, :HEAD_DIM]
        v = _head_block(a, 2 * GDN_WIDTH, h, low)[:, :HEAD_DIM]
        q = cq * lax.rsqrt(jnp.sum(cq * cq, axis=1, keepdims=True) + 1e-6) * (HEAD_DIM ** -0.5)
        k = ck * lax.rsqrt(jnp.sum(ck * ck, axis=1, keepdims=True) + 1e-6)
        g = small[:, h:h + 1]
        beta = small[:, GDN_HEADS + h:GDN_HEADS + h + 1]
        eg = jnp.exp(g)
        s0 = s0_ref[0, h]
        w_col = _row_to_col(k * (beta * eg), HEAD_DIM)
        vn = v * beta - jnp.sum(w_col * s0, axis=0, keepdims=True)
        q_col = _row_to_col(q * eg, HEAD_DIM)
        o = jnp.sum(q_col * s0, axis=0, keepdims=True) + jnp.sum(q * k, axis=1, keepdims=True) * vn
        s_out_ref[0, h] = s0 * eg + _row_to_col(k, HEAD_DIM) * vn
        zh = _head_block(z, 0, h, low)[:, :HEAD_DIM]
        outs.append(_rms(o) * nw * _silu(zh))
    o_ref[0] = jnp.concatenate(outs, axis=1).astype(o_ref.dtype)


def _gdn_step(gqkv, z, small, state_conv, state_delta, conv_w, norm_w):
    nb = gqkv.shape[0]
    W3 = 3 * GDN_WIDTH
    nw = jnp.concatenate([norm_w.astype(F32), jnp.zeros((LANES - HEAD_DIM,), F32)]).reshape(1, LANES)
    per_b = lambda *shape: pl.BlockSpec((1,) + shape, lambda b: (b,) + (0,) * len(shape))
    return pl.pallas_call(
        _gdn_step_kernel,
        grid=(nb,),
        in_specs=[per_b(1, W3), per_b(1, GDN_WIDTH), per_b(1, SMALL_W), per_b(CONV_WIDTH - 1, W3),
                  per_b(GDN_HEADS, HEAD_DIM, HEAD_DIM),
                  pl.BlockSpec((CONV_WIDTH, W3), lambda b: (0, 0)), pl.BlockSpec((1, LANES), lambda b: (0, 0))],
        out_specs=[per_b(1, GDN_WIDTH), per_b(CONV_WIDTH - 1, W3), per_b(GDN_HEADS, HEAD_DIM, HEAD_DIM)],
        out_shape=[jax.ShapeDtypeStruct((nb, 1, GDN_WIDTH), BF16),
                   jax.ShapeDtypeStruct((nb, CONV_WIDTH - 1, W3), F32),
                   jax.ShapeDtypeStruct((nb, GDN_HEADS, HEAD_DIM, HEAD_DIM), F32)],
        compiler_params=_params(("arbitrary",)),
        name="gdn_step",
    )(gqkv.reshape(nb, 1, W3), z.reshape(nb, 1, GDN_WIDTH), small.reshape(nb, 1, SMALL_W), state_conv, state_delta,
      conv_w, nw)


def _fox_decode_kernel(pt_ref, q_ref, kn_ref, vn_ref, sm_ref, *refs):
    page_refs = refs[:-6]
    o_ref, q_s, m_ref, l_ref, acc_ref, carry_ref = refs[-6:]
    p = pl.program_id(1)
    H = FOX_HEADS
    W = FOX_WIDTH
    ps = page_refs[0].shape[3]
    diag = lax.broadcasted_iota(I32, (H, W), 1) // HEAD_DIM == lax.broadcasted_iota(I32, (H, W), 0)
    block_diag = lambda a: jnp.where(diag, jnp.concatenate([a] * H, axis=1), 0.0)

    @pl.when(p == 0)
    def _():
        q = (q_ref[0] * (HEAD_DIM ** -0.5)).astype(BF16)
        q_s[...] = block_diag(q.astype(F32)).astype(BF16)
        m_ref[...] = jnp.sum(q.astype(F32) * kn_ref[0].astype(BF16).astype(F32), axis=1, keepdims=True)
        l_ref[...] = jnp.ones_like(l_ref)
        acc_ref[...] = block_diag(vn_ref[0].astype(BF16).astype(F32))
        lane = lax.broadcasted_iota(I32, (H, SMALL_W), 1)
        row = lax.broadcasted_iota(I32, (H, SMALL_W), 0)
        carry_ref[...] = jnp.sum(jnp.where(lane == row + 2 * GDN_HEADS, sm_ref[0], 0.0), axis=1, keepdims=True)

    j = lax.broadcasted_iota(I32, (ps, ps), 0)
    pos = lax.broadcasted_iota(I32, (ps, ps), 1)
    later = (j > pos).astype(F32)
    npages = len(page_refs) // 3
    k_refs, v_refs, lf_refs = page_refs[:npages], page_refs[npages:2 * npages], page_refs[2 * npages:]
    suffix = [_dot(r[0], later, precision=HIGHEST) for r in lf_refs]
    carry = carry_ref[...]
    ss = []
    for i in range(npages):
        kt = k_refs[i][0].reshape(W, ps).astype(BF16)
        ss.append(_dot(q_s[...], kt) + (suffix[i] + carry))
        carry = carry + jnp.sum(lf_refs[i][0], axis=1, keepdims=True)
    carry_ref[...] = carry
    m_old = m_ref[...]
    m_new = m_old
    for s in ss:
        m_new = jnp.maximum(m_new, jnp.max(s, axis=1, keepdims=True))
    alpha = jnp.exp(m_old - m_new)
    pms = [jnp.exp(s - m_new) for s in ss]
    l_new = l_ref[...] * alpha
    acc = acc_ref[...] * alpha
    for i in range(npages):
        l_new = l_new + jnp.sum(pms[i], axis=1, keepdims=True)
        acc = acc + _dot_nt(pms[i].astype(BF16), v_refs[i][0].reshape(W, ps).astype(BF16))
    l_ref[...] = l_new
    acc_ref[...] = acc
    m_ref[...] = m_new

    @pl.when(p == pl.num_programs(1) - 1)
    def _():
        out = jnp.sum(jnp.where(diag, acc_ref[...] / l_ref[...], 0.0), axis=0, keepdims=True)
        o_ref[0] = _spread_heads(out).astype(o_ref.dtype)


def _fox_decode(page_table, fq, fk, fv, small, cache_k, cache_v, cache_logf):
    nb, npg = page_table.shape
    npool, ps = cache_k.shape[:2]
    H = FOX_HEADS
    W = FOX_WIDTH
    P = PAGES_PER_STEP
    k_t = jnp.transpose(cache_k, (0, 2, 3, 1))
    v_t = jnp.transpose(cache_v, (0, 2, 3, 1))
    lf_t = jnp.transpose(cache_logf, (0, 2, 1))
    assert npg % P == 0
    per_b = lambda *shape: pl.BlockSpec((1,) + shape, lambda b, p, pt: (b,) + (0,) * len(shape))

    def pages(*shape):
        return [pl.BlockSpec((1,) + shape, lambda b, p, pt, i=i: (pt[b, npg - 1 - (p * P + i)],) + (0,) * len(shape))
                for i in range(P)]

    heads = lambda a: a.reshape(nb, H, HEAD_DIM)
    return pl.pallas_call(
        _fox_decode_kernel,
        grid_spec=pltpu.PrefetchScalarGridSpec(
            num_scalar_prefetch=1,
            grid=(nb, npg // P),
            in_specs=[per_b(H, HEAD_DIM), per_b(H, HEAD_DIM), per_b(H, HEAD_DIM), per_b(1, SMALL_W)]
            + pages(H, HEAD_DIM, ps) + pages(H, HEAD_DIM, ps) + pages(H, ps),
            out_specs=per_b(1, H * LANES),
            scratch_shapes=[pltpu.VMEM((H, W), BF16), pltpu.VMEM((H, 1), F32), pltpu.VMEM((H, 1), F32),
                            pltpu.VMEM((H, W), F32), pltpu.VMEM((H, 1), F32)]),
        out_shape=jax.ShapeDtypeStruct((nb, 1, H * LANES), BF16),
        compiler_params=_params(("arbitrary", "arbitrary"), 48),
        name="fox_decode",
    )(page_table, heads(fq), heads(fk), heads(fv), small.reshape(nb, 1, SMALL_W),
      *([k_t] * P + [v_t] * P + [lf_t] * P))


def kernel(x_prompt, x_sample, cache_k, cache_v, cache_logf, state_conv, state_delta, page_table, c_prompt, c_sample, w_mod, b_mod, w_in, conv_w, a_log, dt_bias, gdn_norm_w, fox_fb, w_out, peer_wq, peer_subkeys, peer_u, peer_v, final_norm_w):
    assert w_mod.shape[0] == 1, "one layer"
    D = x_prompt.shape[-1]
    Bp, L = x_prompt.shape[:2]
    Bs = x_sample.shape[0]
    assert x_sample.shape[1] == 1 and L % GDN_CHUNK == 0 and Bs % SUBLANES == 0
    tm_in, tq, tm_out, tl, tb = (min(t, L) for t in TILES)

    c = jnp.concatenate([c_prompt, c_sample], axis=0)
    mod = _mod(c, w_mod[0], b_mod[0])
    mod_p = [mod[:Bp, None, j * D:(j + 1) * D] for j in range(N_MOD)]
    pad_s = (-Bs) % LANES
    mod_s = [mod[None, Bp:, j * D:(j + 1) * D] for j in range(N_MOD)]
    mod_s_pad = [jnp.pad(m, ((0, 0), (0, pad_s), (0, 0))) for m in mod_s]

    w_cat, padd, alog = _prep_inproj(w_in[0], a_log[0], dt_bias[0], fox_fb[0])
    wg, wf, wq = _prep_outproj(w_out[0], peer_wq[0])
    subkeys = peer_subkeys[0].reshape(PEER_HEADS * 2, N_KEYS, PEER_DK_HALF).astype(BF16)
    wts = (wg, wf, wq, subkeys, _pack_table(peer_u[0]), _pack_table(peer_v[0]), final_norm_w)

    xp = x_prompt.reshape(Bp * L, D)
    gqkv, z, small, fk, fv, qa, ka, vat = _inproj(xp, mod_p[0], mod_p[1], w_cat, padd, alog, tm_in, L // tm_in)
    og, delta_p = _gdn_prompt(gqkv, z, small, conv_w[0], gdn_norm_w[0], Bp, L)
    of = _fox_prompt(qa, ka, vat, Bp, L, tq)
    y_p = _post_mixers(og, of, xp, mod_p[2], mod_p[3], mod_p[4], mod_p[5], wts, tm_out, L, tl, tb)

    xs = x_sample.reshape(Bs, D)
    gqkv_s, z_s, small_s, fk_s, fv_s, fq_s = _inproj(xs, mod_s[0], mod_s[1], w_cat, padd, alog, Bs, 1, decode=True)
    og_s, conv_s, delta_s = _gdn_step(gqkv_s, z_s, small_s, state_conv[0], state_delta[0], conv_w[0], gdn_norm_w[0])
    of_s = _fox_decode(page_table, fq_s, fk_s, fv_s, small_s, cache_k[0], cache_v[0], cache_logf[0])
    rows = lambda a: jnp.pad(a.reshape(Bs, -1), ((0, pad_s), (0, 0)))
    ts = Bs + pad_s
    y_s = _post_mixers(rows(og_s), rows(of_s), rows(xs), mod_s_pad[2], mod_s_pad[3], mod_s_pad[4], mod_s_pad[5], wts,
                       ts, ts, ts, min(tb, ts))[:Bs]

    nf = 2 * GDN_HEADS
    return (y_p.reshape(Bp, L, D), y_s.reshape(Bs, 1, D),
            fk.reshape(1, Bp, L, FOX_HEADS, HEAD_DIM), fv.reshape(1, Bp, L, FOX_HEADS, HEAD_DIM),
            small[:, nf:nf + FOX_HEADS].reshape(1, Bp, L, FOX_HEADS),
            gqkv.reshape(Bp, L, -1)[None, :, L - (CONV_WIDTH - 1):], delta_p[None],
            fk_s.reshape(1, Bs, 1, FOX_HEADS, HEAD_DIM), fv_s.reshape(1, Bs, 1, FOX_HEADS, HEAD_DIM),
            small_s[:, nf:nf + FOX_HEADS].reshape(1, Bs, 1, FOX_HEADS), conv_s[None], delta_s[None])
```

```python
import functools

import numpy as np
import jax
import jax.numpy as jnp
from jax import lax
from jax.experimental import pallas as pl
from jax.experimental.pallas import tpu as pltpu

F32 = jnp.float32
BF16 = jnp.bfloat16
I32 = jnp.int32

HEAD_DIM = 64
GDN_HEADS = 8
FOX_HEADS = 8
GDN_WIDTH = GDN_HEADS * HEAD_DIM
FOX_WIDTH = FOX_HEADS * HEAD_DIM
CONV_WIDTH = 4
PEER_HEADS = 8
N_KEYS = 128
PEER_TOPK = 16
PEER_DK_HALF = 128
N_MOD = 6
EPS = 1e-6
LANES = 128
SUBLANES = 8
GDN_CHUNK = 128
SMALL_W = LANES
BF16_TILE_ROWS = 16
TABLE_ROWS_LOG2 = 13
TABLE_ROWS = 1 << TABLE_ROWS_LOG2
TILE_ROWS = BF16_TILE_ROWS
EXPERTS_PER_DOT = 16
U_GROUPS_PER_ITER = 4
V_GROUPS_PER_ITER = 4
PAGES_PER_STEP = 16
TOPK_SETS_PER_STEP = 4
FOX_ROW_SPLIT = 4
HIGHEST = lax.Precision.HIGHEST
NEG_INF = float("-inf")


def _dot(a, b, precision=None):
    return jnp.dot(a, b, preferred_element_type=F32, precision=precision)


def _dot_nt(a, b, precision=None):
    return lax.dot_general(a, b, (((1,), (1,)), ((), ())), preferred_element_type=F32, precision=precision)


def _bdot(a, b):
    return _dot(a.astype(BF16), b.astype(BF16))


def _bdot_nt(a, b):
    return _dot_nt(a.astype(BF16), b.astype(BF16))


def _split3(x):
    hi = x.astype(BF16)
    r1 = x - hi.astype(F32)
    mid = r1.astype(BF16)
    lo = (r1 - mid.astype(F32)).astype(BF16)
    return hi, mid, lo


def _dot3(a, b):
    ah = a.astype(BF16)
    al = (a - ah.astype(F32)).astype(BF16)
    bh = b.astype(BF16)
    bl = (b - bh.astype(F32)).astype(BF16)
    return _dot(ah, bh) + (_dot(ah, bl) + _dot(al, bh))


def _params(sem, vmem_mb=None):
    kw = dict(dimension_semantics=sem)
    if vmem_mb is not None:
        kw["vmem_limit_bytes"] = vmem_mb * 1024 * 1024
    return pltpu.CompilerParams(**kw)


def _rms(x):
    return x * lax.rsqrt(jnp.mean(x * x, axis=-1, keepdims=True) + EPS)


def _silu(x):
    return x * jax.nn.sigmoid(x)


def _mod_kernel(c_ref, w_ref, b_ref, o_ref):
    s = _silu(c_ref[...])
    o_ref[...] = _bdot(s, w_ref[...]) + b_ref[...]


def _mod(c, w, b):
    n, d = c.shape
    nout = w.shape[1]
    tn = 1024
    return pl.pallas_call(
        _mod_kernel,
        grid=(nout // tn,),
        in_specs=[pl.BlockSpec((n, d), lambda j: (0, 0)),
                  pl.BlockSpec((d, tn), lambda j: (0, j)),
                  pl.BlockSpec((1, tn), lambda j: (0, j))],
        out_specs=pl.BlockSpec((n, tn), lambda j: (0, j)),
        out_shape=jax.ShapeDtypeStruct((n, nout), F32),
        compiler_params=_params(("arbitrary",)),
        name="mod",
    )(c, w, b.reshape(1, nout))


def _spread_heads(a):
    tm = a.shape[0]
    lane = lax.broadcasted_iota(I32, (tm, LANES), 1)
    low = lane < HEAD_DIM
    out = []
    for j in range(a.shape[1] // LANES):
        blk = a[:, j * LANES:(j + 1) * LANES]
        out.append(jnp.where(low, blk, 0.0))
        out.append(jnp.where(low, pltpu.roll(blk, HEAD_DIM, axis=1), 0.0))
    return jnp.concatenate(out, axis=1)


def _inproj_kernel(tiles_per_seq, x_ref, sh_ref, sc_ref, w_ref, padd_ref, alog_ref, selq_ref, selk_ref, cq_ref,
                   ck_ref, cv_ref, gqkv_ref, z_ref, small_ref, fk_ref, fv_ref, *rest):
    carry_ref = rest[-1]
    i = pl.program_id(0)
    tm = x_ref.shape[0]
    h = _rms(x_ref[...]) * (1.0 + sc_ref[0]) + sh_ref[0]
    hb = h.astype(BF16)
    o = 0
    gqkv_ref[...] = _dot(hb, w_ref[:, o:o + 3 * GDN_WIDTH]); o += 3 * GDN_WIDTH
    z_ref[...] = _dot(hb, w_ref[:, o:o + GDN_WIDTH]); o += GDN_WIDTH
    sm = _dot(hb, w_ref[:, o:o + SMALL_W]); o += SMALL_W
    fq = _dot(hb, w_ref[:, o:o + FOX_WIDTH]); o += FOX_WIDTH
    fk = _dot(hb, w_ref[:, o:o + FOX_WIDTH]); o += FOX_WIDTH
    fv = _dot(hb, w_ref[:, o:o + FOX_WIDTH])
    fk_ref[...] = fk
    fv_ref[...] = fv

    lane = lax.broadcasted_iota(I32, (tm, SMALL_W), 1)
    y = sm + padd_ref[...]
    t = jnp.log1p(jnp.exp(-jnp.abs(y)))
    softplus = jnp.maximum(y, 0.0) + t
    logsig = jnp.minimum(y, 0.0) - t
    small = jnp.where(lane < GDN_HEADS, -jnp.exp(alog_ref[...]) * softplus,
                      jnp.where(lane < 2 * GDN_HEADS, jax.nn.sigmoid(sm),
                                jnp.where(lane < 2 * GDN_HEADS + FOX_HEADS, logsig, 0.0)))
    small_ref[...] = small
    if len(rest) == 2:
        rest[0][...] = fq
        return
    qa_ref, ka_ref, vat_ref = rest[:3]

    @pl.when(i % tiles_per_seq == 0)
    def _():
        carry_ref[...] = jnp.zeros_like(carry_ref)

    r = lax.broadcasted_iota(I32, (tm, tm), 0)
    c = lax.broadcasted_iota(I32, (tm, tm), 1)
    ltri = (c <= r).astype(F32)
    cum = _dot(ltri, small, precision=HIGHEST) + carry_ref[...]
    carry_ref[...] = cum[tm - 1:tm, :]
    hi, mid, lo = _split3(cum)
    parts = jnp.concatenate([hi, mid, lo], axis=1)
    qa_ref[...] = (_spread_heads(fq * (HEAD_DIM ** -0.5)) + _dot(parts, selq_ref[...]) + cq_ref[...]).astype(BF16)
    ka_ref[...] = (_spread_heads(fk) + _dot(parts, selk_ref[...]) + ck_ref[...]).astype(BF16)
    va = _spread_heads(fv) + cv_ref[...]
    for j in range(va.shape[1] // LANES):
        for t in range(tm // LANES):
            vat_ref[j * LANES:(j + 1) * LANES, t * LANES:(t + 1) * LANES] = (
                va[t * LANES:(t + 1) * LANES, j * LANES:(j + 1) * LANES].T.astype(BF16))


def _inproj_consts():
    selq = np.zeros((3 * SMALL_W, FOX_HEADS * LANES), np.float32)
    selk = np.zeros((3 * SMALL_W, FOX_HEADS * LANES), np.float32)
    cq = np.zeros((1, FOX_HEADS * LANES), np.float32)
    ck = np.zeros((1, FOX_HEADS * LANES), np.float32)
    cv = np.zeros((1, FOX_HEADS * LANES), np.float32)
    for h in range(FOX_HEADS):
        base = h * LANES + HEAD_DIM
        for p in range(3):
            src = p * SMALL_W + 2 * GDN_HEADS + h
            selq[src, base + p] = 1.0
            ck[0, base + p] = 1.0
            selk[src, base + 3 + p] = -1.0
            cq[0, base + 3 + p] = 1.0
        cv[0, base] = 1.0
    return (jnp.asarray(selq, BF16), jnp.asarray(selk, BF16), jnp.asarray(cq), jnp.asarray(ck), jnp.asarray(cv))


def _prep_inproj(w_in, a_log, dt_bias, fox_fb):
    o = np.cumsum((0, GDN_WIDTH, GDN_WIDTH, GDN_WIDTH, GDN_WIDTH, GDN_HEADS, GDN_HEADS, FOX_WIDTH, FOX_WIDTH,
                   FOX_WIDTH, FOX_HEADS)).tolist()
    d = w_in.shape[0]
    nsmall = 2 * GDN_HEADS + FOX_HEADS
    small = jnp.concatenate([w_in[:, o[4]:o[6]], w_in[:, o[9]:o[10]], jnp.zeros((d, SMALL_W - nsmall), w_in.dtype)], 1)
    w_cat = jnp.concatenate([w_in[:, :o[4]], small, w_in[:, o[6]:o[9]]], axis=1).astype(BF16)
    zero = jnp.zeros((GDN_HEADS,), F32)
    tail = jnp.zeros((SMALL_W - nsmall,), F32)
    padd = jnp.concatenate([dt_bias.astype(F32), zero, fox_fb.astype(F32), tail]).reshape(1, SMALL_W)
    alog = jnp.concatenate([a_log.astype(F32), zero, zero, tail]).reshape(1, SMALL_W)
    return w_cat, padd, alog


def _inproj(x, sh, sc, w_cat, padd, pmul, tm, tiles_per_seq, decode=False):
    T, D = x.shape
    rm = sh.shape[1]
    nw = w_cat.shape[1]
    selq, selk, cq, ck, cv = _inproj_consts()
    aw = FOX_HEADS * LANES
    row = lambda w: pl.BlockSpec((tm, w), lambda i: (i, 0))
    const = lambda a: pl.BlockSpec(a.shape, lambda i: (0,) * a.ndim)
    seq = pl.BlockSpec((1, rm, D), lambda i: (i // tiles_per_seq, 0, 0))
    outs = [(3 * GDN_WIDTH, F32), (GDN_WIDTH, F32), (SMALL_W, F32), (FOX_WIDTH, F32), (FOX_WIDTH, F32)]
    outs += [(FOX_WIDTH, F32)] if decode else [(aw, BF16), (aw, BF16)]
    out_specs = [row(w) for w, _ in outs]
    out_shape = [jax.ShapeDtypeStruct((T, w), dt) for w, dt in outs]
    if not decode:
        out_specs.append(pl.BlockSpec((aw, tm), lambda i: (0, i)))
        out_shape.append(jax.ShapeDtypeStruct((aw, T), BF16))
    return pl.pallas_call(
        functools.partial(_inproj_kernel, tiles_per_seq),
        grid=(T // tm,),
        in_specs=[row(D), seq, seq, const(w_cat), const(padd), const(pmul), const(selq), const(selk), const(cq),
                  const(ck), const(cv)],
        out_specs=out_specs,
        out_shape=out_shape,
        scratch_shapes=[pltpu.VMEM((1, SMALL_W), F32)],
        compiler_params=_params(("arbitrary",), 48),
        name="inproj",
    )(x, sh, sc, w_cat, padd, pmul, selq, selk, cq, ck, cv)


def _head_block(a, base, h, low):
    j, odd = divmod(h, 2)
    blk = a[:, base + j * LANES:base + (j + 1) * LANES]
    if odd:
        blk = pltpu.roll(blk, HEAD_DIM, axis=1)
    return jnp.where(low, blk, 0.0)


def _join_heads(heads):
    return jnp.concatenate([heads[2 * j] + pltpu.roll(heads[2 * j + 1], HEAD_DIM, axis=1)
                            for j in range(len(heads) // 2)], axis=1)


def _unit_lower_inverses(a_mats, r, cc):
    n = a_mats[0].shape[0]
    s = SUBLANES
    same = (r // s) == (cc // s)
    eye = jnp.where(r == cc, 1.0, 0.0)
    bs = [jnp.where(same, -a, 0.0) for a in a_mats]
    ts = [eye + b for b in bs]
    b2s = [_dot3(b, b) for b in bs]
    ts = [t + _dot3(t, b2) for t, b2 in zip(ts, b2s)]
    b4s = [_dot3(b2, b2) for b2 in b2s]
    ts = [t + _dot3(t, b4) for t, b4 in zip(ts, b4s)]
    while s < n:
        same2 = (r // (2 * s)) == (cc // (2 * s))
        new = same2 & jnp.logical_not(same)
        xs = [_dot3(t, jnp.where(new, a, 0.0)) for t, a in zip(ts, a_mats)]
        ts = [t - _dot3(x, t) for t, x in zip(ts, xs)]
        same = same2
        s *= 2
    return ts


def _gdn_kernel(u_ref, z_ref, sm_ref, cw_ref, nw_ref, o_ref, s_out_ref, ubuf, s_ref):
    c = pl.program_id(1)
    C = u_ref.shape[0]
    W3 = 3 * GDN_WIDTH

    @pl.when(c == 0)
    def _():
        ubuf[0:SUBLANES, :] = jnp.zeros((SUBLANES, W3), F32)
        s_ref[...] = jnp.zeros_like(s_ref)

    ubuf[SUBLANES:SUBLANES + C, :] = u_ref[...]
    cw = cw_ref[...]
    conv = cw[CONV_WIDTH - 1:CONV_WIDTH] * ubuf[SUBLANES:SUBLANES + C, :]
    for j in range(1, CONV_WIDTH):
        conv = conv + cw[CONV_WIDTH - 1 - j:CONV_WIDTH - j] * ubuf[SUBLANES - j:SUBLANES - j + C, :]
    ubuf[0:SUBLANES, :] = ubuf[C:C + SUBLANES, :]
    a = _silu(conv)
    z = z_ref[...]
    small = sm_ref[...]

    lane = lax.broadcasted_iota(I32, (C, LANES), 1)
    low = lane < HEAD_DIM
    r = lax.broadcasted_iota(I32, (C, C), 0)
    cc = lax.broadcasted_iota(I32, (C, C), 1)
    causal = cc <= r
    strict = cc < r
    g_cum = _dot(causal.astype(F32), small, precision=HIGHEST)
    g_last = g_cum[C - 1:C, :]
    e_g = jnp.exp(g_cum)
    e_gl = jnp.exp(g_last - g_cum)
    e_last = jnp.exp(g_last)
    pieces = [p.astype(F32) for p in _split3(g_cum)]
    pos_parts = [pieces[0]] + [pltpu.roll(p, GDN_HEADS * i, axis=1) for i, p in enumerate(pieces) if i]
    neg_parts = [pltpu.roll(-p, HEAD_DIM + GDN_HEADS * i, axis=1) for i, p in enumerate(pieces)]

    def g_operands(h):
        p_mat = jnp.zeros((C, LANES), F32)
        q_mat = jnp.zeros((C, LANES), F32)
        for i in range(3):
            lo_lane = lane == h + GDN_HEADS * i
            hi_lane = lane == h + HEAD_DIM + GDN_HEADS * i
            p_mat = jnp.where(lo_lane, pos_parts[i], jnp.where(hi_lane, 1.0, p_mat))
            q_mat = jnp.where(lo_lane, 1.0, jnp.where(hi_lane, neg_parts[i], q_mat))
        return p_mat.astype(BF16), q_mat.astype(BF16)

    nw = nw_ref[...]
    heads = range(GDN_HEADS)
    col = lambda m, h: m[:, h:h + 1]
    cqs = [_head_block(a, 0, h, low) for h in heads]
    cks = [_head_block(a, GDN_WIDTH, h, low) for h in heads]
    vs = [_head_block(a, 2 * GDN_WIDTH, h, low) for h in heads]
    qs = [cq * lax.rsqrt(jnp.sum(cq * cq, axis=1, keepdims=True) + 1e-6) * (HEAD_DIM ** -0.5) for cq in cqs]
    ks = [ck * lax.rsqrt(jnp.sum(ck * ck, axis=1, keepdims=True) + 1e-6) for ck in cks]
    betas = [col(small, GDN_HEADS + h) for h in heads]
    g_diffs = [_dot_nt(*g_operands(h)) for h in heads]
    decays = [jnp.exp(jnp.where(causal, g, NEG_INF)) for g in g_diffs]
    a_mats = [jnp.where(strict, betas[h] * _bdot_nt(ks[h], ks[h]) * decays[h], 0.0) for h in heads]
    rhss = [vs[h] * betas[h] + pltpu.roll(ks[h] * (betas[h] * col(e_g, h)), HEAD_DIM, axis=1) for h in heads]
    invs = _unit_lower_inverses(a_mats, r, cc)
    xs = [_dot3(invs[h], rhss[h]) for h in heads]
    u_mats = [jnp.where(low, x, 0.0) for x in xs]
    w_mats = [jnp.where(low, pltpu.roll(x, HEAD_DIM, axis=1), 0.0) for x in xs]
    a_qks = [_bdot_nt(qs[h], ks[h]) * decays[h] for h in heads]
    s_olds = [s_ref[h] for h in heads]
    vns = [u_mats[h] - _bdot(w_mats[h], s_olds[h]) for h in heads]
    os_ = [_bdot(qs[h] * col(e_g, h), s_olds[h]) + _bdot(a_qks[h], vns[h]) for h in heads]
    for h in heads:
        s_ref[h] = s_olds[h] * col(e_last, h) + _bdot((ks[h] * col(e_gl, h)).T, vns[h])
    outs = []
    for h in heads:
        o = os_[h]
        ms = jnp.sum(o * o, axis=1, keepdims=True) * (1.0 / HEAD_DIM)
        outs.append(o * lax.rsqrt(ms + EPS) * nw * _silu(_head_block(z, 0, h, low)))
    o_ref[...] = _join_heads(outs).astype(o_ref.dtype)

    @pl.when(c == pl.num_programs(1) - 1)
    def _():
        s_out_ref[0] = s_ref[:, 0:HEAD_DIM, 0:HEAD_DIM]


def _gdn_prompt(gqkv, z, small, conv_w, norm_w, nb, L):
    C = GDN_CHUNK
    nch = L // C
    W3 = 3 * GDN_WIDTH
    nw = jnp.concatenate([norm_w.astype(F32), jnp.zeros((LANES - HEAD_DIM,), F32)]).reshape(1, LANES)
    row = lambda w: pl.BlockSpec((C, w), lambda b, c: (b * nch + c, 0))
    return pl.pallas_call(
        _gdn_kernel,
        grid=(nb, nch),
        in_specs=[row(W3), row(GDN_WIDTH), row(SMALL_W),
                  pl.BlockSpec((CONV_WIDTH, W3), lambda b, c: (0, 0)),
                  pl.BlockSpec((1, LANES), lambda b, c: (0, 0))],
        out_specs=[row(GDN_WIDTH),
                   pl.BlockSpec((1, GDN_HEADS, HEAD_DIM, HEAD_DIM), lambda b, c: (b, 0, 0, 0))],
        out_shape=[jax.ShapeDtypeStruct((nb * L, GDN_WIDTH), BF16),
                   jax.ShapeDtypeStruct((nb, GDN_HEADS, HEAD_DIM, HEAD_DIM), F32)],
        scratch_shapes=[pltpu.VMEM((SUBLANES + C, W3), F32), pltpu.VMEM((GDN_HEADS, LANES, LANES), F32)],
        compiler_params=_params(("arbitrary", "arbitrary")),
        name="gdn_prompt",
    )(gqkv, z, small, conv_w, nw)


def _fox_kernel(qt_ref, kt_ref, q_ref, k_ref, vt_ref, o_ref, m_ref, acc_ref):
    p = pl.program_id(2)
    qi = qt_ref[p]
    ki = kt_ref[p]
    tq = q_ref.shape[0]
    tk = k_ref.shape[0]

    @pl.when(ki == 0)
    def _():
        m_ref[...] = jnp.full_like(m_ref, NEG_INF)
        acc_ref[...] = jnp.zeros_like(acc_ref)

    def step(on_diagonal):
        nsplit = min(FOX_ROW_SPLIT, tq // LANES)
        tr = tq // nsplit
        blocks = [slice(i * tr, (i + 1) * tr) for i in range(nsplit)]
        k = k_ref[...]
        vt = vt_ref[...]
        ss = [_dot_nt(k, q_ref[b, :]) for b in blocks]
        if on_diagonal:
            key = lax.broadcasted_iota(I32, (tk, tr), 0)
            qry = lax.broadcasted_iota(I32, (tk, tr), 1)
            ss = [jnp.where(key <= qry + i * tr, s, NEG_INF) for i, s in enumerate(ss)]
        m_olds = [m_ref[:, b] for b in blocks]
        m_news = [jnp.maximum(m, jnp.max(s, axis=0, keepdims=True)) for m, s in zip(m_olds, ss)]
        ps = [jnp.exp(s - m).astype(BF16) for s, m in zip(ss, m_news)]
        for b, m_old, m_new, p in zip(blocks, m_olds, m_news, ps):
            acc_ref[:, b] = acc_ref[:, b] * jnp.exp(m_old - m_new) + _dot(vt, p)
            m_ref[:, b] = m_new

    @pl.when(ki < qi)
    def _():
        step(False)

    @pl.when(ki == qi)
    def _():
        step(True)
        acc = acc_ref[...]
        out = acc / acc[HEAD_DIM:HEAD_DIM + 1, :]
        for j in range(tq // LANES):
            o_ref[j * LANES:(j + 1) * LANES, :] = out[:, j * LANES:(j + 1) * LANES].T.astype(o_ref.dtype)


def _fox_prompt(qa, ka, vat, nb, L, tq):
    nq = L // tq
    assert tq % LANES == 0
    pairs = [(i, j) for i in range(nq) for j in range(i + 1)]
    qt = jnp.asarray([p[0] for p in pairs], I32)
    kt = jnp.asarray([p[1] for p in pairs], I32)
    qspec = pl.BlockSpec((tq, LANES), lambda b, h, p, qt, kt: (b * nq + qt[p], h))
    kspec = pl.BlockSpec((tq, LANES), lambda b, h, p, qt, kt: (b * nq + kt[p], h))
    vspec = pl.BlockSpec((LANES, tq), lambda b, h, p, qt, kt: (h, b * nq + kt[p]))
    return pl.pallas_call(
        _fox_kernel,
        grid_spec=pltpu.PrefetchScalarGridSpec(
            num_scalar_prefetch=2,
            grid=(nb, FOX_HEADS, len(pairs)),
            in_specs=[qspec, kspec, vspec],
            out_specs=qspec,
            scratch_shapes=[pltpu.VMEM((1, tq), F32), pltpu.VMEM((LANES, tq), F32)]),
        out_shape=jax.ShapeDtypeStruct(qa.shape, BF16),
        compiler_params=_params(("arbitrary", "arbitrary", "arbitrary")),
        name="fox_prompt",
    )(qt, kt, qa, ka, vat)


def _outproj_kernel(og_ref, of_ref, x_ref, g1_ref, sh_ref, sc_ref, wg_ref, wf_ref, wq_ref, x1_ref, h2_ref, qp_ref):
    m = _dot(og_ref[...], wg_ref[...]) + _dot(of_ref[...], wf_ref[...])
    x1 = x_ref[...] + g1_ref[0] * m
    x1_ref[...] = x1
    h2 = _rms(x1) * (1.0 + sc_ref[0]) + sh_ref[0]
    h2_ref[...] = h2
    qp_ref[...] = _dot(h2.astype(BF16), wq_ref[...]).astype(BF16)


def _prep_outproj(w_out, peer_wq):
    wg = w_out[:GDN_WIDTH].astype(BF16)
    wf = w_out[GDN_WIDTH:].reshape(FOX_HEADS, HEAD_DIM, -1)
    wf = jnp.pad(wf, ((0, 0), (0, LANES - HEAD_DIM), (0, 0))).reshape(FOX_HEADS * LANES, -1).astype(BF16)
    return wg, wf, peer_wq.astype(BF16)


def _outproj(og, of, x, g1, sh2, sc2, wg, wf, wq, tm, tiles_per_seq):
    T, D = x.shape
    rm = g1.shape[1]
    nq = wq.shape[1]
    row = lambda w: pl.BlockSpec((tm, w), lambda i: (i, 0))
    const = lambda a: pl.BlockSpec(a.shape, lambda i: (0,) * a.ndim)
    seq = pl.BlockSpec((1, rm, D), lambda i: (i // tiles_per_seq, 0, 0))
    return pl.pallas_call(
        _outproj_kernel,
        grid=(T // tm,),
        in_specs=[row(og.shape[1]), row(of.shape[1]), row(D), seq, seq, seq, const(wg), const(wf), const(wq)],
        out_specs=[row(D), row(D), row(nq)],
        out_shape=[jax.ShapeDtypeStruct((T, D), F32), jax.ShapeDtypeStruct((T, D), F32),
                   jax.ShapeDtypeStruct((T, nq), BF16)],
        compiler_params=_params(("arbitrary",), 48),
        name="outproj",
    )(og, of, x, g1, sh2, sc2, wg, wf, wq)


def _topk_rows(ss, k):
    n = ss[0].shape[0]
    iota_n = lax.broadcasted_iota(I32, ss[0].shape, 0)
    vals = [[] for _ in ss]
    idxs = [[] for _ in ss]
    for _ in range(k):
        ms = [jnp.max(s, axis=0, keepdims=True) for s in ss]
        ids = [jnp.min(jnp.where(s == m, iota_n, n), axis=0, keepdims=True) for s, m in zip(ss, ms)]
        ss = [jnp.where(iota_n == i, NEG_INF, s) for s, i in zip(ss, ids)]
        for j, (m, i) in enumerate(zip(ms, ids)):
            vals[j].append(m)
            idxs[j].append(i)
    return [jnp.concatenate(v, axis=0) for v in vals], [jnp.concatenate(i, axis=0) for i in idxs]


def _pair_rows(a0, a1, op):
    k = PEER_TOPK
    rows = [op(a0[0:1], a1[0:SUBLANES]), op(a0[0:1], a1[SUBLANES:k])]
    rows += [op(a0[a:a + 1], a1[0:SUBLANES]) for a in range(1, SUBLANES)]
    rows.append(op(a0[SUBLANES:k], a1[0:1]))
    return jnp.concatenate(rows, axis=0)


def _topk_kernel(q_ref, sk_ref, r_ref, gx_ref, sv_ref, si_ref):
    hp = pl.program_id(1)
    tl = q_ref.shape[0]
    k = PEER_TOPK
    nsub = sk_ref.shape[0]
    dk = sk_ref.shape[2]
    ss = [_dot_nt(sk_ref[i], q_ref[:, i * dk:(i + 1) * dk]) for i in range(nsub)]
    vals, idxs = _topk_rows(ss, k)
    for i in range(nsub):
        sv_ref[hp * nsub + i] = vals[i]
        si_ref[hp * nsub + i] = idxs[i]

    @pl.when(hp == pl.num_programs(1) - 1)
    def _():
        nrow = 10 * SUBLANES
        ridx = lax.broadcasted_iota(I32, (nrow, tl), 0)
        blk = ridx // SUBLANES
        w = ridx % SUBLANES
        flat = jnp.where(blk == 0, w, jnp.where(blk == 1, SUBLANES + w,
                         jnp.where(blk <= SUBLANES, k * (blk - 1) + w, k * (SUBLANES + w))))
        heads = range(PEER_HEADS)
        cands = [_pair_rows(sv_ref[2 * h], sv_ref[2 * h + 1], lambda a, b: a + b) for h in heads]
        ecands = [_pair_rows(si_ref[2 * h], si_ref[2 * h + 1], lambda a, b: a * N_KEYS + b) for h in heads]
        fvs = [[] for _ in heads]
        es = [[] for _ in heads]
        for _ in range(k):
            ms = [jnp.max(c, axis=0, keepdims=True) for c in cands]
            fs = [jnp.min(jnp.where(c == m, flat, k * k), axis=0, keepdims=True) for c, m in zip(cands, ms)]
            sels = [flat == f for f in fs]
            for h in heads:
                es[h].append(jnp.max(jnp.where(sels[h], ecands[h], -1), axis=0, keepdims=True))
                fvs[h].append(ms[h])
            cands = [jnp.where(sel, NEG_INF, c) for sel, c in zip(sels, cands)]
        g_all = []
        for h in heads:
            fv = jnp.concatenate(fvs[h], axis=0)
            ex = jnp.exp(fv - fv[0:1])
            g_all.append(ex / jnp.sum(ex, axis=0, keepdims=True))
        e_mat = jnp.concatenate([jnp.concatenate(e, axis=0) for e in es], axis=0)
        g_mat = jnp.concatenate(g_all, axis=0)
        row_mat = (e_mat & (TABLE_ROWS - 1)).astype(F32)
        half_mat = e_mat >> TABLE_ROWS_LOG2
        for j in range(tl // LANES):
            tok = slice(j * LANES, (j + 1) * LANES)
            r_ref[tok, :] = row_mat[:, tok].T.astype(I32)
        par = lax.broadcasted_iota(I32, (TILE_ROWS, tl), 0) % 2
        for i in range(PEER_HEADS * k // SUBLANES):
            rows = [jnp.where(par == half_mat[n:n + 1], g_mat[n:n + 1], 0.0)
                    for n in range(i * SUBLANES, (i + 1) * SUBLANES)]
            blk = jnp.concatenate(rows, axis=0)
            for j in range(tl // LANES):
                tok = slice(j * LANES, (j + 1) * LANES)
                gx_ref[tok, i * LANES:(i + 1) * LANES] = blk[:, tok].T


def _topk(qp, subkeys, tl):
    T = qp.shape[0]
    nhp = subkeys.shape[0]
    ne = PEER_HEADS * PEER_TOPK
    out = lambda w: pl.BlockSpec((tl, w), lambda i, hp: (i, 0))
    return pl.pallas_call(
        _topk_kernel,
        grid=(T // tl, nhp // TOPK_SETS_PER_STEP),
        in_specs=[pl.BlockSpec((tl, TOPK_SETS_PER_STEP * PEER_DK_HALF), lambda i, hp: (i, hp)),
                  pl.BlockSpec((TOPK_SETS_PER_STEP, N_KEYS, PEER_DK_HALF), lambda i, hp: (hp, 0, 0))],
        out_specs=[out(ne), out(ne * TILE_ROWS)],
        out_shape=[jax.ShapeDtypeStruct((T, ne), I32), jax.ShapeDtypeStruct((T, ne * TILE_ROWS), F32)],
        scratch_shapes=[pltpu.VMEM((nhp, PEER_TOPK, tl), F32), pltpu.VMEM((nhp, PEER_TOPK, tl), I32)],
        compiler_params=_params(("arbitrary", "arbitrary")),
        name="peer_topk",
    )(qp, subkeys)


def _pack_table(w):
    E, D = w.shape
    assert E == 2 * TABLE_ROWS and D == (TILE_ROWS // 2) * LANES
    t = w.astype(BF16).reshape(2, TABLE_ROWS, D // LANES, LANES)
    return jnp.transpose(t, (1, 2, 0, 3)).reshape(TABLE_ROWS, TILE_ROWS, LANES)


def _gather_tiles(tab_ref, r_ref, t, c):
    return jnp.concatenate([tab_ref[r_ref[t, c * EXPERTS_PER_DOT + j]] for j in range(EXPERTS_PER_DOT)], axis=0)


def _fold(x, y, d, low):
    return jnp.where(low, x, y) + pltpu.roll(jnp.where(low, y, x), SUBLANES - d, axis=0)


def _sublane_sums(vs, lows):
    for d, low in zip((1, 2, 4), lows):
        vs = [_fold(vs[2 * i], vs[2 * i + 1], d, low) for i in range(len(vs) // 2)]
    return vs[0]


def _swap8(vs):
    sub = lax.broadcasted_iota(I32, (SUBLANES, LANES), 0)
    for d in (4, 2, 1):
        low = (sub & d) == 0
        out = list(vs)
        for i in range(SUBLANES):
            if i & d == 0:
                out[i] = jnp.where(low, vs[i], pltpu.roll(vs[i + d], d, axis=0))
                out[i + d] = jnp.where(low, pltpu.roll(vs[i], SUBLANES - d, axis=0), vs[i + d])
        vs = out
    return vs


def _segment_mask():
    cw = EXPERTS_PER_DOT * TILE_ROWS
    sub = lax.broadcasted_iota(I32, (SUBLANES, cw), 0)
    lane = lax.broadcasted_iota(I32, (SUBLANES, cw), 1)
    return (lane % TILE_ROWS) // 2 == sub


def _peer_u_kernel(r_ref, x_ref, gx_ref, tab_ref, act_ref):
    tb = x_ref.shape[0]
    ne = r_ref.shape[1]
    cw = EXPERTS_PER_DOT * TILE_ROWS
    sub = lax.broadcasted_iota(I32, (SUBLANES, LANES), 0)
    lows = [(sub & d) == 0 for d in (1, 2, 4)]
    lane = lax.broadcasted_iota(I32, (SUBLANES, LANES), 1)
    seg_mask = _segment_mask()

    def group(base):
        xbs = [x.astype(BF16) for x in
               _swap8([x_ref[pl.ds(base, SUBLANES), s * LANES:(s + 1) * LANES] for s in range(SUBLANES)])]
        cols = []
        for c in range(ne // EXPERTS_PER_DOT):
            zs = [jnp.where(seg_mask, _dot_nt(xbs[tt], _gather_tiles(tab_ref, r_ref, base + tt, c)), 0.0)
                  for tt in range(SUBLANES)]
            for v in range(cw // LANES):
                col = _sublane_sums([z[:, v * LANES:(v + 1) * LANES] for z in zs], lows)
                for dist in (2, 4, 8):
                    up = pltpu.roll(col, LANES - dist, axis=1)
                    dn = pltpu.roll(col, dist, axis=1)
                    col = col + jnp.where((lane & dist) == 0, up, dn)
                cols.append(col)
        d = jnp.concatenate(cols, axis=1)
        gelu = 0.5 * d * (1.0 + lax.erf(d * (2.0 ** -0.5)))
        act_ref[pl.ds(base, SUBLANES), :] = gelu * gx_ref[pl.ds(base, SUBLANES), :]

    def groups(gi, carry):
        for g in range(U_GROUPS_PER_ITER):
            group(pl.multiple_of((gi * U_GROUPS_PER_ITER + g) * SUBLANES, SUBLANES))
        return carry

    lax.fori_loop(0, tb // (SUBLANES * U_GROUPS_PER_ITER), groups, 0)


def _peer_u(rows, x, gx, tab, tb):
    T, ne = rows.shape
    assert x.shape[1] == SUBLANES * LANES
    wide = pl.BlockSpec((tb, gx.shape[1]), lambda i: (i, 0))
    return pl.pallas_call(
        _peer_u_kernel,
        grid=(T // tb,),
        in_specs=[pl.BlockSpec((tb, ne), lambda i: (i, 0), memory_space=pltpu.SMEM),
                  pl.BlockSpec((tb, x.shape[1]), lambda i: (i, 0)),
                  wide,
                  pl.BlockSpec(tab.shape, lambda i: (0, 0, 0), pipeline_mode=pl.Buffered(1))],
        out_specs=wide,
        out_shape=jax.ShapeDtypeStruct(gx.shape, F32),
        compiler_params=_params(("arbitrary",), 48),
        name="peer_u",
    )(rows, x, gx, tab)


def _peer_v_kernel(r_ref, a_ref, x1_ref, g2_ref, fw_ref, tab_ref, y_ref):
    tb = y_ref.shape[0]
    ne = r_ref.shape[1]
    cw = EXPERTS_PER_DOT * TILE_ROWS
    seg_mask = _segment_mask()
    per_row = g2_ref.shape[1] > 1

    def group(base):
        rows8 = pl.ds(base, SUBLANES)
        a8 = a_ref[rows8, :]
        accs = []
        for tt in range(SUBLANES):
            acc = jnp.zeros((SUBLANES, LANES), F32)
            for c in range(ne // EXPERTS_PER_DOT):
                lhs = jnp.where(seg_mask, a8[tt:tt + 1, c * cw:(c + 1) * cw], 0.0).astype(BF16)
                acc = acc + _dot(lhs, _gather_tiles(tab_ref, r_ref, base + tt, c))
            accs.append(acc)
        peer = jnp.concatenate(_swap8(accs), axis=1)
        g2 = g2_ref[0, rows8, :] if per_row else g2_ref[0]
        y_ref[rows8, :] = _rms(x1_ref[rows8, :] + g2 * peer) * fw_ref[...]

    def groups(gi, carry):
        for g in range(V_GROUPS_PER_ITER):
            group(pl.multiple_of((gi * V_GROUPS_PER_ITER + g) * SUBLANES, SUBLANES))
        return carry

    lax.fori_loop(0, tb // (SUBLANES * V_GROUPS_PER_ITER), groups, 0)


def _peer_v(rows, act, x1, g2, fw, tab, tb, tiles_per_seq):
    T, ne = rows.shape
    D = x1.shape[1]
    assert D == SUBLANES * LANES
    if g2.shape[1] == 1:
        g2_spec = pl.BlockSpec((1, 1, D), lambda i: (i // tiles_per_seq, 0, 0))
    else:
        g2_spec = pl.BlockSpec((1, tb, D), lambda i: (i // tiles_per_seq, i % tiles_per_seq, 0))
    row = pl.BlockSpec((tb, D), lambda i: (i, 0))
    return pl.pallas_call(
        _peer_v_kernel,
        grid=(T // tb,),
        in_specs=[pl.BlockSpec((tb, ne), lambda i: (i, 0), memory_space=pltpu.SMEM),
                  pl.BlockSpec((tb, act.shape[1]), lambda i: (i, 0)),
                  row, g2_spec, pl.BlockSpec((1, D), lambda i: (0, 0)),
                  pl.BlockSpec(tab.shape, lambda i: (0, 0, 0), pipeline_mode=pl.Buffered(1))],
        out_specs=row,
        out_shape=jax.ShapeDtypeStruct((T, D), F32),
        compiler_params=_params(("arbitrary",), 48),
        name="peer_v",
    )(rows, act, x1, g2, fw.reshape(1, D).astype(F32), tab)


TILES = (256, 1024, 256, 256, 64)


def _post_mixers(og, of, x, g1, sh2, sc2, g2, wts, tm, rows_per_seq, tl, tb):
    wg, wf, wq, subkeys, tab_u, tab_v, fw = wts
    x1, h2, qp = _outproj(og, of, x, g1, sh2, sc2, wg, wf, wq, tm, rows_per_seq // tm)
    rows, gx = _topk(qp, subkeys, tl)
    act = _peer_u(rows, h2, gx, tab_u, tb)
    return _peer_v(rows, act, x1, g2, fw, tab_v, tb, rows_per_seq // tb)


def _row_to_col(row, n):
    r = lax.broadcasted_iota(I32, (n, n), 0)
    c = lax.broadcasted_iota(I32, (n, n), 1)
    return jnp.sum(jnp.where(r == c, row, 0.0), axis=1, keepdims=True)


def _gdn_step_kernel(u_ref, z_ref, sm_ref, cs_ref, s0_ref, cw_ref, nw_ref, o_ref, cs_out_ref, s_out_ref):
    u = u_ref[0]
    st = cs_ref[0]
    cw = cw_ref[...]
    conv = cw[CONV_WIDTH - 1:CONV_WIDTH] * u
    for j in range(CONV_WIDTH - 1):
        conv = conv + cw[j:j + 1] * st[j:j + 1]
    cs_out_ref[0] = jnp.concatenate([st[1:CONV_WIDTH - 1], u], axis=0)
    a = _silu(conv)
    z = z_ref[0]
    small = sm_ref[0]
    low = lax.broadcasted_iota(I32, (1, LANES), 1) < HEAD_DIM
    nw = nw_ref[...][:, :HEAD_DIM]
    outs = []
    for h in range(GDN_HEADS):
        cq = _head_block(a, 0, h, low)[:, :HEAD_DIM]
        ck = _head_block(a, GDN_WIDTH, h, low)[:, :HEAD_DIM]
        v = _head_block(a, 2 * GDN_WIDTH, h, low)[:, :HEAD_DIM]
        q = cq * lax.rsqrt(jnp.sum(cq * cq, axis=1, keepdims=True) + 1e-6) * (HEAD_DIM ** -0.5)
        k = ck * lax.rsqrt(jnp.sum(ck * ck, axis=1, keepdims=True) + 1e-6)
        g = small[:, h:h + 1]
        beta = small[:, GDN_HEADS + h:GDN_HEADS + h + 1]
        eg = jnp.exp(g)
        s0 = s0_ref[0, h]
        w_col = _row_to_col(k * (beta * eg), HEAD_DIM)
        vn = v * beta - jnp.sum(w_col * s0, axis=0, keepdims=True)
        q_col = _row_to_col(q * eg, HEAD_DIM)
        o = jnp.sum(q_col * s0, axis=0, keepdims=True) + jnp.sum(q * k, axis=1, keepdims=True) * vn
        s_out_ref[0, h] = s0 * eg + _row_to_col(k, HEAD_DIM) * vn
        zh = _head_block(z, 0, h, low)[:, :HEAD_DIM]
        outs.append(_rms(o) * nw * _silu(zh))
    o_ref[0] = jnp.concatenate(outs, axis=1).astype(o_ref.dtype)


def _gdn_step(gqkv, z, small, state_conv, state_delta, conv_w, norm_w):
    nb = gqkv.shape[0]
    W3 = 3 * GDN_WIDTH
    nw = jnp.concatenate([norm_w.astype(F32), jnp.zeros((LANES - HEAD_DIM,), F32)]).reshape(1, LANES)
    per_b = lambda *shape: pl.BlockSpec((1,) + shape, lambda b: (b,) + (0,) * len(shape))
    return pl.pallas_call(
        _gdn_step_kernel,
        grid=(nb,),
        in_specs=[per_b(1, W3), per_b(1, GDN_WIDTH), per_b(1, SMALL_W), per_b(CONV_WIDTH - 1, W3),
                  per_b(GDN_HEADS, HEAD_DIM, HEAD_DIM),
                  pl.BlockSpec((CONV_WIDTH, W3), lambda b: (0, 0)), pl.BlockSpec((1, LANES), lambda b: (0, 0))],
        out_specs=[per_b(1, GDN_WIDTH), per_b(CONV_WIDTH - 1, W3), per_b(GDN_HEADS, HEAD_DIM, HEAD_DIM)],
        out_shape=[jax.ShapeDtypeStruct((nb, 1, GDN_WIDTH), BF16),
                   jax.ShapeDtypeStruct((nb, CONV_WIDTH - 1, W3), F32),
                   jax.ShapeDtypeStruct((nb, GDN_HEADS, HEAD_DIM, HEAD_DIM), F32)],
        compiler_params=_params(("arbitrary",)),
        name="gdn_step",
    )(gqkv.reshape(nb, 1, W3), z.reshape(nb, 1, GDN_WIDTH), small.reshape(nb, 1, SMALL_W), state_conv, state_delta,
      conv_w, nw)


def _fox_decode_kernel(pt_ref, q_ref, kn_ref, vn_ref, sm_ref, *refs):
    page_refs = refs[:-6]
    o_ref, q_s, m_ref, l_ref, acc_ref, carry_ref = refs[-6:]
    p = pl.program_id(1)
    H = FOX_HEADS
    W = FOX_WIDTH
    ps = page_refs[0].shape[3]
    diag = lax.broadcasted_iota(I32, (H, W), 1) // HEAD_DIM == lax.broadcasted_iota(I32, (H, W), 0)
    block_diag = lambda a: jnp.where(diag, jnp.concatenate([a] * H, axis=1), 0.0)

    @pl.when(p == 0)
    def _():
        q = (q_ref[0] * (HEAD_DIM ** -0.5)).astype(BF16)
        q_s[...] = block_diag(q.astype(F32)).astype(BF16)
        m_ref[...] = jnp.sum(q.astype(F32) * kn_ref[0].astype(BF16).astype(F32), axis=1, keepdims=True)
        l_ref[...] = jnp.ones_like(l_ref)
        acc_ref[...] = block_diag(vn_ref[0].astype(BF16).astype(F32))
        lane = lax.broadcasted_iota(I32, (H, SMALL_W), 1)
        row = lax.broadcasted_iota(I32, (H, SMALL_W), 0)
        carry_ref[...] = jnp.sum(jnp.where(lane == row + 2 * GDN_HEADS, sm_ref[0], 0.0), axis=1, keepdims=True)

    j = lax.broadcasted_iota(I32, (ps, ps), 0)
    pos = lax.broadcasted_iota(I32, (ps, ps), 1)
    later = (j > pos).astype(F32)
    npages = len(page_refs) // 3
    k_refs, v_refs, lf_refs = page_refs[:npages], page_refs[npages:2 * npages], page_refs[2 * npages:]
    suffix = [_dot(r[0], later, precision=HIGHEST) for r in lf_refs]
    carry = carry_ref[...]
    ss = []
    for i in range(npages):
        kt = k_refs[i][0].reshape(W, ps).astype(BF16)
        ss.append(_dot(q_s[...], kt) + (suffix[i] + carry))
        carry = carry + jnp.sum(lf_refs[i][0], axis=1, keepdims=True)
    carry_ref[...] = carry
    m_old = m_ref[...]
    m_new = m_old
    for s in ss:
        m_new = jnp.maximum(m_new, jnp.max(s, axis=1, keepdims=True))
    alpha = jnp.exp(m_old - m_new)
    pms = [jnp.exp(s - m_new) for s in ss]
    l_new = l_ref[...] * alpha
    acc = acc_ref[...] * alpha
    for i in range(npages):
        l_new = l_new + jnp.sum(pms[i], axis=1, keepdims=True)
        acc = acc + _dot_nt(pms[i].astype(BF16), v_refs[i][0].reshape(W, ps).astype(BF16))
    l_ref[...] = l_new
    acc_ref[...] = acc
    m_ref[...] = m_new

    @pl.when(p == pl.num_programs(1) - 1)
    def _():
        out = jnp.sum(jnp.where(diag, acc_ref[...] / l_ref[...], 0.0), axis=0, keepdims=True)
        o_ref[0] = _spread_heads(out).astype(o_ref.dtype)


def _fox_decode(page_table, fq, fk, fv, small, cache_k, cache_v, cache_logf):
    nb, npg = page_table.shape
    npool, ps = cache_k.shape[:2]
    H = FOX_HEADS
    W = FOX_WIDTH
    P = PAGES_PER_STEP
    k_t = jnp.transpose(cache_k, (0, 2, 3, 1))
    v_t = jnp.transpose(cache_v, (0, 2, 3, 1))
    lf_t = jnp.transpose(cache_logf, (0, 2, 1))
    assert npg % P == 0
    per_b = lambda *shape: pl.BlockSpec((1,) + shape, lambda b, p, pt: (b,) + (0,) * len(shape))

    def pages(*shape):
        return [pl.BlockSpec((1,) + shape, lambda b, p, pt, i=i: (pt[b, npg - 1 - (p * P + i)],) + (0,) * len(shape))
                for i in range(P)]

    heads = lambda a: a.reshape(nb, H, HEAD_DIM)
    return pl.pallas_call(
        _fox_decode_kernel,
        grid_spec=pltpu.PrefetchScalarGridSpec(
            num_scalar_prefetch=1,
            grid=(nb, npg // P),
            in_specs=[per_b(H, HEAD_DIM), per_b(H, HEAD_DIM), per_b(H, HEAD_DIM), per_b(1, SMALL_W)]
            + pages(H, HEAD_DIM, ps) + pages(H, HEAD_DIM, ps) + pages(H, ps),
            out_specs=per_b(1, H * LANES),
            scratch_shapes=[pltpu.VMEM((H, W), BF16), pltpu.VMEM((H, 1), F32), pltpu.VMEM((H, 1), F32),
                            pltpu.VMEM((H, W), F32), pltpu.VMEM((H, 1), F32)]),
        out_shape=jax.ShapeDtypeStruct((nb, 1, H * LANES), BF16),
        compiler_params=_params(("arbitrary", "arbitrary"), 48),
        name="fox_decode",
    )(page_table, heads(fq), heads(fk), heads(fv), small.reshape(nb, 1, SMALL_W),
      *([k_t] * P + [v_t] * P + [lf_t] * P))


def kernel(x_prompt, x_sample, cache_k, cache_v, cache_logf, state_conv, state_delta, page_table, c_prompt, c_sample, w_mod, b_mod, w_in, conv_w, a_log, dt_bias, gdn_norm_w, fox_fb, w_out, peer_wq, peer_subkeys, peer_u, peer_v, final_norm_w):
    assert w_mod.shape[0] == 1, "one layer"
    D = x_prompt.shape[-1]
    Bp, L = x_prompt.shape[:2]
    Bs = x_sample.shape[0]
    assert x_sample.shape[1] == 1 and L % GDN_CHUNK == 0 and Bs % SUBLANES == 0
    tm_in, tq, tm_out, tl, tb = (min(t, L) for t in TILES)

    c = jnp.concatenate([c_prompt, c_sample], axis=0)
    mod = _mod(c, w_mod[0], b_mod[0])
    mod_p = [mod[:Bp, None, j * D:(j + 1) * D] for j in range(N_MOD)]
    pad_s = (-Bs) % LANES
    mod_s = [mod[None, Bp:, j * D:(j + 1) * D] for j in range(N_MOD)]
    mod_s_pad = [jnp.pad(m, ((0, 0), (0, pad_s), (0, 0))) for m in mod_s]

    w_cat, padd, alog = _prep_inproj(w_in[0], a_log[0], dt_bias[0], fox_fb[0])
    wg, wf, wq = _prep_outproj(w_out[0], peer_wq[0])
    subkeys = peer_subkeys[0].reshape(PEER_HEADS * 2, N_KEYS, PEER_DK_HALF).astype(BF16)
    wts = (wg, wf, wq, subkeys, _pack_table(peer_u[0]), _pack_table(peer_v[0]), final_norm_w)

    xp = x_prompt.reshape(Bp * L, D)
    gqkv, z, small, fk, fv, qa, ka, vat = _inproj(xp, mod_p[0], mod_p[1], w_cat, padd, alog, tm_in, L // tm_in)
    og, delta_p = _gdn_prompt(gqkv, z, small, conv_w[0], gdn_norm_w[0], Bp, L)
    of = _fox_prompt(qa, ka, vat, Bp, L, tq)
    y_p = _post_mixers(og, of, xp, mod_p[2], mod_p[3], mod_p[4], mod_p[5], wts, tm_out, L, tl, tb)

    xs = x_sample.reshape(Bs, D)
    gqkv_s, z_s, small_s, fk_s, fv_s, fq_s = _inproj(xs, mod_s[0], mod_s[1], w_cat, padd, alog, Bs, 1, decode=True)
    og_s, conv_s, delta_s = _gdn_step(gqkv_s, z_s, small_s, state_conv[0], state_delta[0], conv_w[0], gdn_norm_w[0])
    of_s = _fox_decode(page_table, fq_s, fk_s, fv_s, small_s, cache_k[0], cache_v[0], cache_logf[0])
    rows = lambda a: jnp.pad(a.reshape(Bs, -1), ((0, pad_s), (0, 0)))
    ts = Bs + pad_s
    y_s = _post_mixers(rows(og_s), rows(of_s), rows(xs), mod_s_pad[2], mod_s_pad[3], mod_s_pad[4], mod_s_pad[5], wts,
                       ts, ts, ts, min(tb, ts))[:Bs]

    nf = 2 * GDN_HEADS
    return (y_p.reshape(Bp, L, D), y_s.reshape(Bs, 1, D),
            fk.reshape(1, Bp, L, FOX_HEADS, HEAD_DIM), fv.reshape(1, Bp, L, FOX_HEADS, HEAD_DIM),
            small[:, nf:nf + FOX_HEADS].reshape(1, Bp, L, FOX_HEADS),
            gqkv.reshape(Bp, L, -1)[None, :, L - (CONV_WIDTH - 1):], delta_p[None],
            fk_s.reshape(1, Bs, 1, FOX_HEADS, HEAD_DIM), fv_s.reshape(1, Bs, 1, FOX_HEADS, HEAD_DIM),
            small_s[:, nf:nf + FOX_HEADS].reshape(1, Bs, 1, FOX_HEADS), conv_s[None], delta_s[None])
```

```python
import functools

import numpy as np
import jax
import jax.numpy as jnp
from jax import lax
from jax.experimental import pallas as pl
from jax.experimental.pallas import tpu as pltpu

F32 = jnp.float32
BF16 = jnp.bfloat16
I32 = jnp.int32

HEAD_DIM = 64
GDN_HEADS = 8
FOX_HEADS = 8
GDN_WIDTH = GDN_HEADS * HEAD_DIM
FOX_WIDTH = FOX_HEADS * HEAD_DIM
CONV_WIDTH = 4
PEER_HEADS = 8
N_KEYS = 128
PEER_TOPK = 16
PEER_DK_HALF = 128
N_MOD = 6
EPS = 1e-6
LANES = 128
SUBLANES = 8
GDN_CHUNK = 128
SMALL_W = LANES
BF16_TILE_ROWS = 16
TABLE_ROWS_LOG2 = 13
TABLE_ROWS = 1 << TABLE_ROWS_LOG2
TILE_ROWS = BF16_TILE_ROWS
EXPERTS_PER_DOT = 16
U_GROUPS_PER_ITER = 4
V_GROUPS_PER_ITER = 4
PAGES_PER_STEP = 16
TOPK_SETS_PER_STEP = 4
FOX_ROW_SPLIT = 4
HIGHEST = lax.Precision.HIGHEST
NEG_INF = float("-inf")


def _dot(a, b, precision=None):
    return jnp.dot(a, b, preferred_element_type=F32, precision=precision)


def _dot_nt(a, b, precision=None):
    return lax.dot_general(a, b, (((1,), (1,)), ((), ())), preferred_element_type=F32, precision=precision)


def _bdot(a, b):
    return _dot(a.astype(BF16), b.astype(BF16))


def _bdot_nt(a, b):
    return _dot_nt(a.astype(BF16), b.astype(BF16))


def _split3(x):
    hi = x.astype(BF16)
    r1 = x - hi.astype(F32)
    mid = r1.astype(BF16)
    lo = (r1 - mid.astype(F32)).astype(BF16)
    return hi, mid, lo


def _dot3(a, b):
    ah = a.astype(BF16)
    al = (a - ah.astype(F32)).astype(BF16)
    bh = b.astype(BF16)
    bl = (b - bh.astype(F32)).astype(BF16)
    return _dot(ah, bh) + (_dot(ah, bl) + _dot(al, bh))


def _params(sem, vmem_mb=None):
    kw = dict(dimension_semantics=sem)
    if vmem_mb is not None:
        kw["vmem_limit_bytes"] = vmem_mb * 1024 * 1024
    return pltpu.CompilerParams(**kw)


def _rms(x):
    return x * lax.rsqrt(jnp.mean(x * x, axis=-1, keepdims=True) + EPS)


def _silu(x):
    return x * jax.nn.sigmoid(x)


def _mod_kernel(c_ref, w_ref, b_ref, o_ref):
    s = _silu(c_ref[...])
    o_ref[...] = _bdot(s, w_ref[...]) + b_ref[...]


def _mod(c, w, b):
    n, d = c.shape
    nout = w.shape[1]
    tn = 1024
    return pl.pallas_call(
        _mod_kernel,
        grid=(nout // tn,),
        in_specs=[pl.BlockSpec((n, d), lambda j: (0, 0)),
                  pl.BlockSpec((d, tn), lambda j: (0, j)),
                  pl.BlockSpec((1, tn), lambda j: (0, j))],
        out_specs=pl.BlockSpec((n, tn), lambda j: (0, j)),
        out_shape=jax.ShapeDtypeStruct((n, nout), F32),
        compiler_params=_params(("arbitrary",)),
        name="mod",
    )(c, w, b.reshape(1, nout))


def _spread_heads(a):
    tm = a.shape[0]
    lane = lax.broadcasted_iota(I32, (tm, LANES), 1)
    low = lane < HEAD_DIM
    out = []
    for j in range(a.shape[1] // LANES):
        blk = a[:, j * LANES:(j + 1) * LANES]
        out.append(jnp.where(low, blk, 0.0))
        out.append(jnp.where(low, pltpu.roll(blk, HEAD_DIM, axis=1), 0.0))
    return jnp.concatenate(out, axis=1)


def _inproj_kernel(tiles_per_seq, x_ref, sh_ref, sc_ref, w_ref, padd_ref, alog_ref, selq_ref, selk_ref, cq_ref,
                   ck_ref, cv_ref, gqkv_ref, z_ref, small_ref, fk_ref, fv_ref, *rest):
    carry_ref = rest[-1]
    i = pl.program_id(0)
    tm = x_ref.shape[0]
    h = _rms(x_ref[...]) * (1.0 + sc_ref[0]) + sh_ref[0]
    hb = h.astype(BF16)
    o = 0
    gqkv_ref[...] = _dot(hb, w_ref[:, o:o + 3 * GDN_WIDTH]); o += 3 * GDN_WIDTH
    z_ref[...] = _dot(hb, w_ref[:, o:o + GDN_WIDTH]); o += GDN_WIDTH
    sm = _dot(hb, w_ref[:, o:o + SMALL_W]); o += SMALL_W
    fq = _dot(hb, w_ref[:, o:o + FOX_WIDTH]); o += FOX_WIDTH
    fk = _dot(hb, w_ref[:, o:o + FOX_WIDTH]); o += FOX_WIDTH
    fv = _dot(hb, w_ref[:, o:o + FOX_WIDTH])
    if len(rest) == 2:
        fk_ref[...] = fk
        fv_ref[...] = fv
    else:
        for ref, val in ((fk_ref, fk), (fv_ref, fv)):
            for j in range(FOX_WIDTH // LANES):
                for t in range(tm // LANES):
                    ref[0, j * LANES:(j + 1) * LANES, t * LANES:(t + 1) * LANES] = (
                        val[t * LANES:(t + 1) * LANES, j * LANES:(j + 1) * LANES].T)

    lane = lax.broadcasted_iota(I32, (tm, SMALL_W), 1)
    y = sm + padd_ref[...]
    t = jnp.log1p(jnp.exp(-jnp.abs(y)))
    softplus = jnp.maximum(y, 0.0) + t
    logsig = jnp.minimum(y, 0.0) - t
    small = jnp.where(lane < GDN_HEADS, -jnp.exp(alog_ref[...]) * softplus,
                      jnp.where(lane < 2 * GDN_HEADS, jax.nn.sigmoid(sm),
                                jnp.where(lane < 2 * GDN_HEADS + FOX_HEADS, logsig, 0.0)))
    small_ref[...] = small
    if len(rest) == 2:
        rest[0][...] = fq
        return
    qa_ref, ka_ref, vat_ref = rest[:3]

    @pl.when(i % tiles_per_seq == 0)
    def _():
        carry_ref[...] = jnp.zeros_like(carry_ref)

    r = lax.broadcasted_iota(I32, (tm, tm), 0)
    c = lax.broadcasted_iota(I32, (tm, tm), 1)
    ltri = (c <= r).astype(F32)
    cum = _dot(ltri, small, precision=HIGHEST) + carry_ref[...]
    carry_ref[...] = cum[tm - 1:tm, :]
    hi, mid, lo = _split3(cum)
    parts = jnp.concatenate([hi, mid, lo], axis=1)
    qa_ref[...] = (_spread_heads(fq * (HEAD_DIM ** -0.5)) + _dot(parts, selq_ref[...]) + cq_ref[...]).astype(BF16)
    ka_ref[...] = (_spread_heads(fk) + _dot(parts, selk_ref[...]) + ck_ref[...]).astype(BF16)
    va = _spread_heads(fv) + cv_ref[...]
    for j in range(va.shape[1] // LANES):
        for t in range(tm // LANES):
            vat_ref[j * LANES:(j + 1) * LANES, t * LANES:(t + 1) * LANES] = (
                va[t * LANES:(t + 1) * LANES, j * LANES:(j + 1) * LANES].T.astype(BF16))


def _inproj_consts():
    selq = np.zeros((3 * SMALL_W, FOX_HEADS * LANES), np.float32)
    selk = np.zeros((3 * SMALL_W, FOX_HEADS * LANES), np.float32)
    cq = np.zeros((1, FOX_HEADS * LANES), np.float32)
    ck = np.zeros((1, FOX_HEADS * LANES), np.float32)
    cv = np.zeros((1, FOX_HEADS * LANES), np.float32)
    for h in range(FOX_HEADS):
        base = h * LANES + HEAD_DIM
        for p in range(3):
            src = p * SMALL_W + 2 * GDN_HEADS + h
            selq[src, base + p] = 1.0
            ck[0, base + p] = 1.0
            selk[src, base + 3 + p] = -1.0
            cq[0, base + 3 + p] = 1.0
        cv[0, base] = 1.0
    return (jnp.asarray(selq, BF16), jnp.asarray(selk, BF16), jnp.asarray(cq), jnp.asarray(ck), jnp.asarray(cv))


def _prep_inproj(w_in, a_log, dt_bias, fox_fb):
    o = np.cumsum((0, GDN_WIDTH, GDN_WIDTH, GDN_WIDTH, GDN_WIDTH, GDN_HEADS, GDN_HEADS, FOX_WIDTH, FOX_WIDTH,
                   FOX_WIDTH, FOX_HEADS)).tolist()
    d = w_in.shape[0]
    nsmall = 2 * GDN_HEADS + FOX_HEADS
    small = jnp.concatenate([w_in[:, o[4]:o[6]], w_in[:, o[9]:o[10]], jnp.zeros((d, SMALL_W - nsmall), w_in.dtype)], 1)
    w_cat = jnp.concatenate([w_in[:, :o[4]], small, w_in[:, o[6]:o[9]]], axis=1).astype(BF16)
    zero = jnp.zeros((GDN_HEADS,), F32)
    tail = jnp.zeros((SMALL_W - nsmall,), F32)
    padd = jnp.concatenate([dt_bias.astype(F32), zero, fox_fb.astype(F32), tail]).reshape(1, SMALL_W)
    alog = jnp.concatenate([a_log.astype(F32), zero, zero, tail]).reshape(1, SMALL_W)
    return w_cat, padd, alog


def _inproj(x, sh, sc, w_cat, padd, pmul, tm, tiles_per_seq, decode=False):
    T, D = x.shape
    rm = sh.shape[1]
    nw = w_cat.shape[1]
    selq, selk, cq, ck, cv = _inproj_consts()
    aw = FOX_HEADS * LANES
    row = lambda w: pl.BlockSpec((tm, w), lambda i: (i, 0))
    const = lambda a: pl.BlockSpec(a.shape, lambda i: (0,) * a.ndim)
    seq = pl.BlockSpec((1, rm, D), lambda i: (i // tiles_per_seq, 0, 0))
    outs = [(3 * GDN_WIDTH, F32), (GDN_WIDTH, F32), (SMALL_W, F32)]
    outs += [(FOX_WIDTH, F32)] * 3 if decode else [(aw, BF16), (aw, BF16)]
    out_specs = [row(w) for w, _ in outs]
    out_shape = [jax.ShapeDtypeStruct((T, w), dt) for w, dt in outs]
    if not decode:
        nseq = T // (tm * tiles_per_seq)
        kv_spec = pl.BlockSpec((1, FOX_WIDTH, tm), lambda i: (i // tiles_per_seq, 0, i % tiles_per_seq))
        kv_shape = jax.ShapeDtypeStruct((nseq, FOX_WIDTH, tm * tiles_per_seq), F32)
        out_specs[3:3] = [kv_spec, kv_spec]
        out_shape[3:3] = [kv_shape, kv_shape]
        out_specs.append(pl.BlockSpec((aw, tm), lambda i: (0, i)))
        out_shape.append(jax.ShapeDtypeStruct((aw, T), BF16))
    return pl.pallas_call(
        functools.partial(_inproj_kernel, tiles_per_seq),
        grid=(T // tm,),
        in_specs=[row(D), seq, seq, const(w_cat), const(padd), const(pmul), const(selq), const(selk), const(cq),
                  const(ck), const(cv)],
        out_specs=out_specs,
        out_shape=out_shape,
        scratch_shapes=[pltpu.VMEM((1, SMALL_W), F32)],
        compiler_params=_params(("arbitrary",), 48),
        name="inproj",
    )(x, sh, sc, w_cat, padd, pmul, selq, selk, cq, ck, cv)


def _head_block(a, base, h, low):
    j, odd = divmod(h, 2)
    blk = a[:, base + j * LANES:base + (j + 1) * LANES]
    if odd:
        blk = pltpu.roll(blk, HEAD_DIM, axis=1)
    return jnp.where(low, blk, 0.0)


def _join_heads(heads):
    return jnp.concatenate([heads[2 * j] + pltpu.roll(heads[2 * j + 1], HEAD_DIM, axis=1)
                            for j in range(len(heads) // 2)], axis=1)


def _unit_lower_inverses(a_mats, r, cc):
    n = a_mats[0].shape[0]
    s = SUBLANES
    same = (r // s) == (cc // s)
    eye = jnp.where(r == cc, 1.0, 0.0)
    bs = [jnp.where(same, -a, 0.0) for a in a_mats]
    ts = [eye + b for b in bs]
    b2s = [_dot3(b, b) for b in bs]
    ts = [t + _dot3(t, b2) for t, b2 in zip(ts, b2s)]
    b4s = [_dot3(b2, b2) for b2 in b2s]
    ts = [t + _dot3(t, b4) for t, b4 in zip(ts, b4s)]
    while s < n:
        same2 = (r // (2 * s)) == (cc // (2 * s))
        new = same2 & jnp.logical_not(same)
        xs = [_dot3(t, jnp.where(new, a, 0.0)) for t, a in zip(ts, a_mats)]
        ts = [t - _dot3(x, t) for t, x in zip(ts, xs)]
        same = same2
        s *= 2
    return ts


def _gdn_kernel(u_ref, z_ref, sm_ref, cw_ref, nw_ref, o_ref, s_out_ref, ubuf, s_ref):
    c = pl.program_id(1)
    C = u_ref.shape[0]
    W3 = 3 * GDN_WIDTH

    @pl.when(c == 0)
    def _():
        ubuf[0:SUBLANES, :] = jnp.zeros((SUBLANES, W3), F32)
        s_ref[...] = jnp.zeros_like(s_ref)

    ubuf[SUBLANES:SUBLANES + C, :] = u_ref[...]
    cw = cw_ref[...]
    conv = cw[CONV_WIDTH - 1:CONV_WIDTH] * ubuf[SUBLANES:SUBLANES + C, :]
    for j in range(1, CONV_WIDTH):
        conv = conv + cw[CONV_WIDTH - 1 - j:CONV_WIDTH - j] * ubuf[SUBLANES - j:SUBLANES - j + C, :]
    ubuf[0:SUBLANES, :] = ubuf[C:C + SUBLANES, :]
    a = _silu(conv)
    z = z_ref[...]
    small = sm_ref[...]

    lane = lax.broadcasted_iota(I32, (C, LANES), 1)
    low = lane < HEAD_DIM
    r = lax.broadcasted_iota(I32, (C, C), 0)
    cc = lax.broadcasted_iota(I32, (C, C), 1)
    causal = cc <= r
    strict = cc < r
    g_cum = _dot(causal.astype(F32), small, precision=HIGHEST)
    g_last = g_cum[C - 1:C, :]
    e_g = jnp.exp(g_cum)
    e_gl = jnp.exp(g_last - g_cum)
    e_last = jnp.exp(g_last)
    pieces = [p.astype(F32) for p in _split3(g_cum)]
    pos_parts = [pieces[0]] + [pltpu.roll(p, GDN_HEADS * i, axis=1) for i, p in enumerate(pieces) if i]
    neg_parts = [pltpu.roll(-p, HEAD_DIM + GDN_HEADS * i, axis=1) for i, p in enumerate(pieces)]

    def g_operands(h):
        p_mat = jnp.zeros((C, LANES), F32)
        q_mat = jnp.zeros((C, LANES), F32)
        for i in range(3):
            lo_lane = lane == h + GDN_HEADS * i
            hi_lane = lane == h + HEAD_DIM + GDN_HEADS * i
            p_mat = jnp.where(lo_lane, pos_parts[i], jnp.where(hi_lane, 1.0, p_mat))
            q_mat = jnp.where(lo_lane, 1.0, jnp.where(hi_lane, neg_parts[i], q_mat))
        return p_mat.astype(BF16), q_mat.astype(BF16)

    nw = nw_ref[...]
    heads = range(GDN_HEADS)
    col = lambda m, h: m[:, h:h + 1]
    cqs = [_head_block(a, 0, h, low) for h in heads]
    cks = [_head_block(a, GDN_WIDTH, h, low) for h in heads]
    vs = [_head_block(a, 2 * GDN_WIDTH, h, low) for h in heads]
    qs = [cq * lax.rsqrt(jnp.sum(cq * cq, axis=1, keepdims=True) + 1e-6) * (HEAD_DIM ** -0.5) for cq in cqs]
    ks = [ck * lax.rsqrt(jnp.sum(ck * ck, axis=1, keepdims=True) + 1e-6) for ck in cks]
    betas = [col(small, GDN_HEADS + h) for h in heads]
    g_diffs = [_dot_nt(*g_operands(h)) for h in heads]
    decays = [jnp.exp(jnp.where(causal, g, NEG_INF)) for g in g_diffs]
    a_mats = [jnp.where(strict, betas[h] * _bdot_nt(ks[h], ks[h]) * decays[h], 0.0) for h in heads]
    rhss = [vs[h] * betas[h] + pltpu.roll(ks[h] * (betas[h] * col(e_g, h)), HEAD_DIM, axis=1) for h in heads]
    invs = _unit_lower_inverses(a_mats, r, cc)
    xs = [_dot3(invs[h], rhss[h]) for h in heads]
    u_mats = [jnp.where(low, x, 0.0) for x in xs]
    w_mats = [jnp.where(low, pltpu.roll(x, HEAD_DIM, axis=1), 0.0) for x in xs]
    a_qks = [_bdot_nt(qs[h], ks[h]) * decays[h] for h in heads]
    s_olds = [s_ref[h] for h in heads]
    vns = [u_mats[h] - _bdot(w_mats[h], s_olds[h]) for h in heads]
    os_ = [_bdot(qs[h] * col(e_g, h), s_olds[h]) + _bdot(a_qks[h], vns[h]) for h in heads]
    for h in heads:
        s_ref[h] = s_olds[h] * col(e_last, h) + _bdot((ks[h] * col(e_gl, h)).T, vns[h])
    outs = []
    for h in heads:
        o = os_[h]
        ms = jnp.sum(o * o, axis=1, keepdims=True) * (1.0 / HEAD_DIM)
        outs.append(o * lax.rsqrt(ms + EPS) * nw * _silu(_head_block(z, 0, h, low)))
    o_ref[...] = _join_heads(outs).astype(o_ref.dtype)

    @pl.when(c == pl.num_programs(1) - 1)
    def _():
        s_out_ref[0] = s_ref[:, 0:HEAD_DIM, 0:HEAD_DIM]


def _gdn_prompt(gqkv, z, small, conv_w, norm_w, nb, L):
    C = GDN_CHUNK
    nch = L // C
    W3 = 3 * GDN_WIDTH
    nw = jnp.concatenate([norm_w.astype(F32), jnp.zeros((LANES - HEAD_DIM,), F32)]).reshape(1, LANES)
    row = lambda w: pl.BlockSpec((C, w), lambda b, c: (b * nch + c, 0))
    return pl.pallas_call(
        _gdn_kernel,
        grid=(nb, nch),
        in_specs=[row(W3), row(GDN_WIDTH), row(SMALL_W),
                  pl.BlockSpec((CONV_WIDTH, W3), lambda b, c: (0, 0)),
                  pl.BlockSpec((1, LANES), lambda b, c: (0, 0))],
        out_specs=[row(GDN_WIDTH),
                   pl.BlockSpec((1, GDN_HEADS, HEAD_DIM, HEAD_DIM), lambda b, c: (b, 0, 0, 0))],
        out_shape=[jax.ShapeDtypeStruct((nb * L, GDN_WIDTH), BF16),
                   jax.ShapeDtypeStruct((nb, GDN_HEADS, HEAD_DIM, HEAD_DIM), F32)],
        scratch_shapes=[pltpu.VMEM((SUBLANES + C, W3), F32), pltpu.VMEM((GDN_HEADS, LANES, LANES), F32)],
        compiler_params=_params(("arbitrary", "arbitrary")),
        name="gdn_prompt",
    )(gqkv, z, small, conv_w, nw)


def _fox_kernel(qt_ref, kt_ref, q_ref, k_ref, vt_ref, o_ref, m_ref, acc_ref):
    p = pl.program_id(2)
    qi = qt_ref[p]
    ki = kt_ref[p]
    tq = q_ref.shape[0]
    tk = k_ref.shape[0]

    @pl.when(ki == 0)
    def _():
        m_ref[...] = jnp.full_like(m_ref, NEG_INF)
        acc_ref[...] = jnp.zeros_like(acc_ref)

    def step(on_diagonal):
        nsplit = min(FOX_ROW_SPLIT, tq // LANES)
        tr = tq // nsplit
        blocks = [slice(i * tr, (i + 1) * tr) for i in range(nsplit)]
        k = k_ref[...]
        vt = vt_ref[...]
        ss = [_dot_nt(k, q_ref[b, :]) for b in blocks]
        if on_diagonal:
            key = lax.broadcasted_iota(I32, (tk, tr), 0)
            qry = lax.broadcasted_iota(I32, (tk, tr), 1)
            ss = [jnp.where(key <= qry + i * tr, s, NEG_INF) for i, s in enumerate(ss)]
        m_olds = [m_ref[:, b] for b in blocks]
        m_news = [jnp.maximum(m, jnp.max(s, axis=0, keepdims=True)) for m, s in zip(m_olds, ss)]
        ps = [jnp.exp(s - m).astype(BF16) for s, m in zip(ss, m_news)]
        for b, m_old, m_new, p in zip(blocks, m_olds, m_news, ps):
            acc_ref[:, b] = acc_ref[:, b] * jnp.exp(m_old - m_new) + _dot(vt, p)
            m_ref[:, b] = m_new

    @pl.when(ki < qi)
    def _():
        step(False)

    @pl.when(ki == qi)
    def _():
        step(True)
        acc = acc_ref[...]
        out = acc / acc[HEAD_DIM:HEAD_DIM + 1, :]
        for j in range(tq // LANES):
            o_ref[j * LANES:(j + 1) * LANES, :] = out[:, j * LANES:(j + 1) * LANES].T.astype(o_ref.dtype)


def _fox_prompt(qa, ka, vat, nb, L, tq):
    nq = L // tq
    assert tq % LANES == 0
    pairs = [(i, j) for i in range(nq) for j in range(i + 1)]
    qt = jnp.asarray([p[0] for p in pairs], I32)
    kt = jnp.asarray([p[1] for p in pairs], I32)
    qspec = pl.BlockSpec((tq, LANES), lambda b, h, p, qt, kt: (b * nq + qt[p], h))
    kspec = pl.BlockSpec((tq, LANES), lambda b, h, p, qt, kt: (b * nq + kt[p], h))
    vspec = pl.BlockSpec((LANES, tq), lambda b, h, p, qt, kt: (h, b * nq + kt[p]))
    return pl.pallas_call(
        _fox_kernel,
        grid_spec=pltpu.PrefetchScalarGridSpec(
            num_scalar_prefetch=2,
            grid=(nb, FOX_HEADS, len(pairs)),
            in_specs=[qspec, kspec, vspec],
            out_specs=qspec,
            scratch_shapes=[pltpu.VMEM((1, tq), F32), pltpu.VMEM((LANES, tq), F32)]),
        out_shape=jax.ShapeDtypeStruct(qa.shape, BF16),
        compiler_params=_params(("arbitrary", "arbitrary", "arbitrary")),
        name="fox_prompt",
    )(qt, kt, qa, ka, vat)


def _outproj_kernel(og_ref, of_ref, x_ref, g1_ref, sh_ref, sc_ref, wg_ref, wf_ref, wq_ref, x1_ref, h2_ref, qp_ref):
    m = _dot(og_ref[...], wg_ref[...]) + _dot(of_ref[...], wf_ref[...])
    x1 = x_ref[...] + g1_ref[0] * m
    x1_ref[...] = x1
    h2 = _rms(x1) * (1.0 + sc_ref[0]) + sh_ref[0]
    h2_ref[...] = h2
    qp_ref[...] = _dot(h2.astype(BF16), wq_ref[...]).astype(BF16)


def _prep_outproj(w_out, peer_wq):
    wg = w_out[:GDN_WIDTH].astype(BF16)
    wf = w_out[GDN_WIDTH:].reshape(FOX_HEADS, HEAD_DIM, -1)
    wf = jnp.pad(wf, ((0, 0), (0, LANES - HEAD_DIM), (0, 0))).reshape(FOX_HEADS * LANES, -1).astype(BF16)
    return wg, wf, peer_wq.astype(BF16)


def _outproj(og, of, x, g1, sh2, sc2, wg, wf, wq, tm, tiles_per_seq):
    T, D = x.shape
    rm = g1.shape[1]
    nq = wq.shape[1]
    row = lambda w: pl.BlockSpec((tm, w), lambda i: (i, 0))
    const = lambda a: pl.BlockSpec(a.shape, lambda i: (0,) * a.ndim)
    seq = pl.BlockSpec((1, rm, D), lambda i: (i // tiles_per_seq, 0, 0))
    return pl.pallas_call(
        _outproj_kernel,
        grid=(T // tm,),
        in_specs=[row(og.shape[1]), row(of.shape[1]), row(D), seq, seq, seq, const(wg), const(wf), const(wq)],
        out_specs=[row(D), row(D), row(nq)],
        out_shape=[jax.ShapeDtypeStruct((T, D), F32), jax.ShapeDtypeStruct((T, D), F32),
                   jax.ShapeDtypeStruct((T, nq), BF16)],
        compiler_params=_params(("arbitrary",), 48),
        name="outproj",
    )(og, of, x, g1, sh2, sc2, wg, wf, wq)


def _topk_rows(ss, k):
    n = ss[0].shape[0]
    iota_n = lax.broadcasted_iota(I32, ss[0].shape, 0).astype(F32)
    vals = [[] for _ in ss]
    idxs = [[] for _ in ss]
    for _ in range(k):
        ms = [jnp.max(s, axis=0, keepdims=True) for s in ss]
        ids = [jnp.min(jnp.where(s == m, iota_n, float(n)), axis=0, keepdims=True) for s, m in zip(ss, ms)]
        ss = [jnp.where(iota_n == i, NEG_INF, s) for s, i in zip(ss, ids)]
        for j, (m, i) in enumerate(zip(ms, ids)):
            vals[j].append(m)
            idxs[j].append(i)
    return [jnp.concatenate(v, axis=0) for v in vals], [jnp.concatenate(i, axis=0) for i in idxs]


def _pair_rows(a0, a1, op):
    k = PEER_TOPK
    rows = [op(a0[0:1], a1[0:SUBLANES]), op(a0[0:1], a1[SUBLANES:k])]
    rows += [op(a0[a:a + 1], a1[0:SUBLANES]) for a in range(1, SUBLANES)]
    rows.append(op(a0[SUBLANES:k], a1[0:1]))
    return jnp.concatenate(rows, axis=0)


def _topk_kernel(q_ref, sk_ref, r_ref, gx_ref, sv_ref, si_ref):
    hp = pl.program_id(1)
    tl = q_ref.shape[0]
    k = PEER_TOPK
    nsub = sk_ref.shape[0]
    dk = sk_ref.shape[2]
    ss = [_dot_nt(sk_ref[i], q_ref[:, i * dk:(i + 1) * dk]) for i in range(nsub)]
    vals, idxs = _topk_rows(ss, k)
    for i in range(nsub):
        sv_ref[hp * nsub + i] = vals[i]
        si_ref[hp * nsub + i] = idxs[i]

    @pl.when(hp == pl.num_programs(1) - 1)
    def _():
        nrow = 10 * SUBLANES
        ridx = lax.broadcasted_iota(I32, (nrow, tl), 0)
        blk = ridx // SUBLANES
        w = ridx % SUBLANES
        flat = jnp.where(blk == 0, w, jnp.where(blk == 1, SUBLANES + w,
                         jnp.where(blk <= SUBLANES, k * (blk - 1) + w, k * (SUBLANES + w)))).astype(F32)
        heads = range(PEER_HEADS)
        cands = [_pair_rows(sv_ref[2 * h], sv_ref[2 * h + 1], lambda a, b: a + b) for h in heads]
        ecands = [_pair_rows(si_ref[2 * h], si_ref[2 * h + 1], lambda a, b: a * float(N_KEYS) + b) for h in heads]
        fvs = [[] for _ in heads]
        es = [[] for _ in heads]
        for _ in range(k):
            ms = [jnp.max(c, axis=0, keepdims=True) for c in cands]
            fs = [jnp.min(jnp.where(c == m, flat, float(k * k)), axis=0, keepdims=True) for c, m in zip(cands, ms)]
            sels = [flat == f for f in fs]
            for h in heads:
                es[h].append(jnp.max(jnp.where(sels[h], ecands[h], -1.0), axis=0, keepdims=True))
                fvs[h].append(ms[h])
            cands = [jnp.where(sel, NEG_INF, c) for sel, c in zip(sels, cands)]
        g_all = []
        for h in heads:
            fv = jnp.concatenate(fvs[h], axis=0)
            ex = jnp.exp(fv - fv[0:1])
            g_all.append(ex / jnp.sum(ex, axis=0, keepdims=True))
        e_mat = jnp.concatenate([jnp.concatenate(e, axis=0) for e in es], axis=0).astype(I32)
        g_mat = jnp.concatenate(g_all, axis=0)
        row_mat = (e_mat & (TABLE_ROWS - 1)).astype(F32)
        half_mat = e_mat >> TABLE_ROWS_LOG2
        for j in range(tl // LANES):
            tok = slice(j * LANES, (j + 1) * LANES)
            r_ref[tok, :] = row_mat[:, tok].T.astype(I32)
        par = lax.broadcasted_iota(I32, (TILE_ROWS, tl), 0) % 2
        for i in range(PEER_HEADS * k // SUBLANES):
            rows = [jnp.where(par == half_mat[n:n + 1], g_mat[n:n + 1], 0.0)
                    for n in range(i * SUBLANES, (i + 1) * SUBLANES)]
            blk = jnp.concatenate(rows, axis=0)
            for j in range(tl // LANES):
                tok = slice(j * LANES, (j + 1) * LANES)
                gx_ref[tok, i * LANES:(i + 1) * LANES] = blk[:, tok].T


def _topk(qp, subkeys, tl):
    T = qp.shape[0]
    nhp = subkeys.shape[0]
    ne = PEER_HEADS * PEER_TOPK
    out = lambda w: pl.BlockSpec((tl, w), lambda i, hp: (i, 0))
    return pl.pallas_call(
        _topk_kernel,
        grid=(T // tl, nhp // TOPK_SETS_PER_STEP),
        in_specs=[pl.BlockSpec((tl, TOPK_SETS_PER_STEP * PEER_DK_HALF), lambda i, hp: (i, hp)),
                  pl.BlockSpec((TOPK_SETS_PER_STEP, N_KEYS, PEER_DK_HALF), lambda i, hp: (hp, 0, 0))],
        out_specs=[out(ne), out(ne * TILE_ROWS)],
        out_shape=[jax.ShapeDtypeStruct((T, ne), I32), jax.ShapeDtypeStruct((T, ne * TILE_ROWS), F32)],
        scratch_shapes=[pltpu.VMEM((nhp, PEER_TOPK, tl), F32), pltpu.VMEM((nhp, PEER_TOPK, tl), F32)],
        compiler_params=_params(("arbitrary", "arbitrary")),
        name="peer_topk",
    )(qp, subkeys)


def _pack_table(w):
    E, D = w.shape
    assert E == 2 * TABLE_ROWS and D == (TILE_ROWS // 2) * LANES
    t = w.astype(BF16).reshape(2, TABLE_ROWS, D // LANES, LANES)
    return jnp.transpose(t, (1, 2, 0, 3)).reshape(TABLE_ROWS, TILE_ROWS, LANES)


def _gather_tiles(tab_ref, r_ref, t, c):
    return jnp.concatenate([tab_ref[r_ref[t, c * EXPERTS_PER_DOT + j]] for j in range(EXPERTS_PER_DOT)], axis=0)


def _fold(x, y, d, low):
    return jnp.where(low, x, y) + pltpu.roll(jnp.where(low, y, x), SUBLANES - d, axis=0)


def _sublane_sums(vs, lows):
    for d, low in zip((1, 2, 4), lows):
        vs = [_fold(vs[2 * i], vs[2 * i + 1], d, low) for i in range(len(vs) // 2)]
    return vs[0]


def _swap8(vs):
    sub = lax.broadcasted_iota(I32, (SUBLANES, LANES), 0)
    for d in (4, 2, 1):
        low = (sub & d) == 0
        out = list(vs)
        for i in range(SUBLANES):
            if i & d == 0:
                out[i] = jnp.where(low, vs[i], pltpu.roll(vs[i + d], d, axis=0))
                out[i + d] = jnp.where(low, pltpu.roll(vs[i], SUBLANES - d, axis=0), vs[i + d])
        vs = out
    return vs


def _segment_mask():
    cw = EXPERTS_PER_DOT * TILE_ROWS
    sub = lax.broadcasted_iota(I32, (SUBLANES, cw), 0)
    lane = lax.broadcasted_iota(I32, (SUBLANES, cw), 1)
    return (lane % TILE_ROWS) // 2 == sub


def _peer_u_kernel(r_ref, x_ref, gx_ref, tab_ref, act_ref):
    tb = x_ref.shape[0]
    ne = r_ref.shape[1]
    cw = EXPERTS_PER_DOT * TILE_ROWS
    sub = lax.broadcasted_iota(I32, (SUBLANES, LANES), 0)
    lows = [(sub & d) == 0 for d in (1, 2, 4)]
    lane = lax.broadcasted_iota(I32, (SUBLANES, LANES), 1)
    seg_mask = _segment_mask()

    def group(base):
        xbs = [x.astype(BF16) for x in
               _swap8([x_ref[pl.ds(base, SUBLANES), s * LANES:(s + 1) * LANES] for s in range(SUBLANES)])]
        cols = []
        for c in range(ne // EXPERTS_PER_DOT):
            zs = [jnp.where(seg_mask, _dot_nt(xbs[tt], _gather_tiles(tab_ref, r_ref, base + tt, c)), 0.0)
                  for tt in range(SUBLANES)]
            for v in range(cw // LANES):
                col = _sublane_sums([z[:, v * LANES:(v + 1) * LANES] for z in zs], lows)
                for dist in (2, 4, 8):
                    up = pltpu.roll(col, LANES - dist, axis=1)
                    dn = pltpu.roll(col, dist, axis=1)
                    col = col + jnp.where((lane & dist) == 0, up, dn)
                cols.append(col)
        d = jnp.concatenate(cols, axis=1)
        gelu = 0.5 * d * (1.0 + lax.erf(d * (2.0 ** -0.5)))
        act_ref[pl.ds(base, SUBLANES), :] = gelu * gx_ref[pl.ds(base, SUBLANES), :]

    def groups(gi, carry):
        for g in range(U_GROUPS_PER_ITER):
            group(pl.multiple_of((gi * U_GROUPS_PER_ITER + g) * SUBLANES, SUBLANES))
        return carry

    lax.fori_loop(0, tb // (SUBLANES * U_GROUPS_PER_ITER), groups, 0)


def _peer_u(rows, x, gx, tab, tb):
    T, ne = rows.shape
    assert x.shape[1] == SUBLANES * LANES
    wide = pl.BlockSpec((tb, gx.shape[1]), lambda i: (i, 0))
    return pl.pallas_call(
        _peer_u_kernel,
        grid=(T // tb,),
        in_specs=[pl.BlockSpec((tb, ne), lambda i: (i, 0), memory_space=pltpu.SMEM),
                  pl.BlockSpec((tb, x.shape[1]), lambda i: (i, 0)),
                  wide,
                  pl.BlockSpec(tab.shape, lambda i: (0, 0, 0), pipeline_mode=pl.Buffered(1))],
        out_specs=wide,
        out_shape=jax.ShapeDtypeStruct(gx.shape, F32),
        compiler_params=_params(("arbitrary",), 48),
        name="peer_u",
    )(rows, x, gx, tab)


def _peer_v_kernel(r_ref, a_ref, x1_ref, g2_ref, fw_ref, tab_ref, y_ref):
    tb = y_ref.shape[0]
    ne = r_ref.shape[1]
    cw = EXPERTS_PER_DOT * TILE_ROWS
    seg_mask = _segment_mask()
    per_row = g2_ref.shape[1] > 1

    def group(base):
        rows8 = pl.ds(base, SUBLANES)
        a8 = a_ref[rows8, :]
        accs = []
        for tt in range(SUBLANES):
            acc = jnp.zeros((SUBLANES, LANES), F32)
            for c in range(ne // EXPERTS_PER_DOT):
                lhs = jnp.where(seg_mask, a8[tt:tt + 1, c * cw:(c + 1) * cw], 0.0).astype(BF16)
                acc = acc + _dot(lhs, _gather_tiles(tab_ref, r_ref, base + tt, c))
            accs.append(acc)
        peer = jnp.concatenate(_swap8(accs), axis=1)
        g2 = g2_ref[0, rows8, :] if per_row else g2_ref[0]
        y_ref[rows8, :] = _rms(x1_ref[rows8, :] + g2 * peer) * fw_ref[...]

    def groups(gi, carry):
        for g in range(V_GROUPS_PER_ITER):
            group(pl.multiple_of((gi * V_GROUPS_PER_ITER + g) * SUBLANES, SUBLANES))
        return carry

    lax.fori_loop(0, tb // (SUBLANES * V_GROUPS_PER_ITER), groups, 0)


def _peer_v(rows, act, x1, g2, fw, tab, tb, tiles_per_seq):
    T, ne = rows.shape
    D = x1.shape[1]
    assert D == SUBLANES * LANES
    if g2.shape[1] == 1:
        g2_spec = pl.BlockSpec((1, 1, D), lambda i: (i // tiles_per_seq, 0, 0))
    else:
        g2_spec = pl.BlockSpec((1, tb, D), lambda i: (i // tiles_per_seq, i % tiles_per_seq, 0))
    row = pl.BlockSpec((tb, D), lambda i: (i, 0))
    return pl.pallas_call(
        _peer_v_kernel,
        grid=(T // tb,),
        in_specs=[pl.BlockSpec((tb, ne), lambda i: (i, 0), memory_space=pltpu.SMEM),
                  pl.BlockSpec((tb, act.shape[1]), lambda i: (i, 0)),
                  row, g2_spec, pl.BlockSpec((1, D), lambda i: (0, 0)),
                  pl.BlockSpec(tab.shape, lambda i: (0, 0, 0), pipeline_mode=pl.Buffered(1))],
        out_specs=row,
        out_shape=jax.ShapeDtypeStruct((T, D), F32),
        compiler_params=_params(("arbitrary",), 48),
        name="peer_v",
    )(rows, act, x1, g2, fw.reshape(1, D).astype(F32), tab)


TILES = (256, 1024, 256, 256, 128)


def _post_mixers(og, of, x, g1, sh2, sc2, g2, wts, tm, rows_per_seq, tl, tb):
    wg, wf, wq, subkeys, tab_u, tab_v, fw = wts
    x1, h2, qp = _outproj(og, of, x, g1, sh2, sc2, wg, wf, wq, tm, rows_per_seq // tm)
    rows, gx = _topk(qp, subkeys, tl)
    act = _peer_u(rows, h2, gx, tab_u, tb)
    return _peer_v(rows, act, x1, g2, fw, tab_v, tb, rows_per_seq // tb)


def _row_to_col(row, n):
    r = lax.broadcasted_iota(I32, (n, n), 0)
    c = lax.broadcasted_iota(I32, (n, n), 1)
    return jnp.sum(jnp.where(r == c, row, 0.0), axis=1, keepdims=True)


def _gdn_step_kernel(u_ref, z_ref, sm_ref, cs_ref, s0_ref, cw_ref, nw_ref, o_ref, cs_out_ref, s_out_ref):
    u = u_ref[0]
    st = cs_ref[0]
    cw = cw_ref[...]
    conv = cw[CONV_WIDTH - 1:CONV_WIDTH] * u
    for j in range(CONV_WIDTH - 1):
        conv = conv + cw[j:j + 1] * st[j:j + 1]
    cs_out_ref[0] = jnp.concatenate([st[1:CONV_WIDTH - 1], u], axis=0)
    a = _silu(conv)
    z = z_ref[0]
    small = sm_ref[0]
    low = lax.broadcasted_iota(I32, (1, LANES), 1) < HEAD_DIM
    nw = nw_ref[...][:, :HEAD_DIM]
    outs = []
    for h in range(GDN_HEADS):
        cq = _head_block(a, 0, h, low)[:, :HEAD_DIM]
        ck = _head_block(a, GDN_WIDTH, h, low)[:, :HEAD_DIM]
        v = _head_block(a, 2 * GDN_WIDTH, h, low)[:, :HEAD_DIM]
        q = cq * lax.rsqrt(jnp.sum(cq * cq, axis=1, keepdims=True) + 1e-6) * (HEAD_DIM ** -0.5)
        k = ck * lax.rsqrt(jnp.sum(ck * ck, axis=1, keepdims=True) + 1e-6)
        g = small[:, h:h + 1]
        beta = small[:, GDN_HEADS + h:GDN_HEADS + h + 1]
        eg = jnp.exp(g)
        s0 = s0_ref[0, h]
        w_col = _row_to_col(k * (beta * eg), HEAD_DIM)
        vn = v * beta - jnp.sum(w_col * s0, axis=0, keepdims=True)
        q_col = _row_to_col(q * eg, HEAD_DIM)
        o = jnp.sum(q_col * s0, axis=0, keepdims=True) + jnp.sum(q * k, axis=1, keepdims=True) * vn
        s_out_ref[0, h] = s0 * eg + _row_to_col(k, HEAD_DIM) * vn
        zh = _head_block(z, 0, h, low)[:, :HEAD_DIM]
        outs.append(_rms(o) * nw * _silu(zh))
    o_ref[0] = jnp.concatenate(outs, axis=1).astype(o_ref.dtype)


def _gdn_step(gqkv, z, small, state_conv, state_delta, conv_w, norm_w):
    nb = gqkv.shape[0]
    W3 = 3 * GDN_WIDTH
    nw = jnp.concatenate([norm_w.astype(F32), jnp.zeros((LANES - HEAD_DIM,), F32)]).reshape(1, LANES)
    per_b = lambda *shape: pl.BlockSpec((1,) + shape, lambda b: (b,) + (0,) * len(shape))
    return pl.pallas_call(
        _gdn_step_kernel,
        grid=(nb,),
        in_specs=[per_b(1, W3), per_b(1, GDN_WIDTH), per_b(1, SMALL_W), per_b(CONV_WIDTH - 1, W3),
                  per_b(GDN_HEADS, HEAD_DIM, HEAD_DIM),
                  pl.BlockSpec((CONV_WIDTH, W3), lambda b: (0, 0)), pl.BlockSpec((1, LANES), lambda b: (0, 0))],
        out_specs=[per_b(1, GDN_WIDTH), per_b(CONV_WIDTH - 1, W3), per_b(GDN_HEADS, HEAD_DIM, HEAD_DIM)],
        out_shape=[jax.ShapeDtypeStruct((nb, 1, GDN_WIDTH), BF16),
                   jax.ShapeDtypeStruct((nb, CONV_WIDTH - 1, W3), F32),
                   jax.ShapeDtypeStruct((nb, GDN_HEADS, HEAD_DIM, HEAD_DIM), F32)],
        compiler_params=_params(("arbitrary",)),
        name="gdn_step",
    )(gqkv.reshape(nb, 1, W3), z.reshape(nb, 1, GDN_WIDTH), small.reshape(nb, 1, SMALL_W), state_conv, state_delta,
      conv_w, nw)


def _fox_decode_kernel(pt_ref, q_ref, kn_ref, vn_ref, sm_ref, *refs):
    page_refs = refs[:-6]
    o_ref, q_s, m_ref, l_ref, acc_ref, carry_ref = refs[-6:]
    p = pl.program_id(1)
    H = FOX_HEADS
    W = FOX_WIDTH
    ps = page_refs[0].shape[3]
    diag = lax.broadcasted_iota(I32, (H, W), 1) // HEAD_DIM == lax.broadcasted_iota(I32, (H, W), 0)
    block_diag = lambda a: jnp.where(diag, jnp.concatenate([a] * H, axis=1), 0.0)

    @pl.when(p == 0)
    def _():
        q = (q_ref[0] * (HEAD_DIM ** -0.5)).astype(BF16)
        q_s[...] = block_diag(q.astype(F32)).astype(BF16)
        m_ref[...] = jnp.sum(q.astype(F32) * kn_ref[0].astype(BF16).astype(F32), axis=1, keepdims=True)
        l_ref[...] = jnp.ones_like(l_ref)
        acc_ref[...] = block_diag(vn_ref[0].astype(BF16).astype(F32))
        lane = lax.broadcasted_iota(I32, (H, SMALL_W), 1)
        row = lax.broadcasted_iota(I32, (H, SMALL_W), 0)
        carry_ref[...] = jnp.sum(jnp.where(lane == row + 2 * GDN_HEADS, sm_ref[0], 0.0), axis=1, keepdims=True)

    j = lax.broadcasted_iota(I32, (ps, ps), 0)
    pos = lax.broadcasted_iota(I32, (ps, ps), 1)
    later = (j > pos).astype(F32)
    npages = len(page_refs) // 3
    k_refs, v_refs, lf_refs = page_refs[:npages], page_refs[npages:2 * npages], page_refs[2 * npages:]
    suffix = [_dot(r[0], later, precision=HIGHEST) for r in lf_refs]
    carry = carry_ref[...]
    ss = []
    for i in range(npages):
        kt = k_refs[i][0].reshape(W, ps).astype(BF16)
        ss.append(_dot(q_s[...], kt) + (suffix[i] + carry))
        carry = carry + jnp.sum(lf_refs[i][0], axis=1, keepdims=True)
    carry_ref[...] = carry
    m_old = m_ref[...]
    m_new = m_old
    for s in ss:
        m_new = jnp.maximum(m_new, jnp.max(s, axis=1, keepdims=True))
    alpha = jnp.exp(m_old - m_new)
    pms = [jnp.exp(s - m_new) for s in ss]
    l_new = l_ref[...] * alpha
    acc = acc_ref[...] * alpha
    for i in range(npages):
        l_new = l_new + jnp.sum(pms[i], axis=1, keepdims=True)
        acc = acc + _dot_nt(pms[i].astype(BF16), v_refs[i][0].reshape(W, ps).astype(BF16))
    l_ref[...] = l_new
    acc_ref[...] = acc
    m_ref[...] = m_new

    @pl.when(p == pl.num_programs(1) - 1)
    def _():
        out = jnp.sum(jnp.where(diag, acc_ref[...] / l_ref[...], 0.0), axis=0, keepdims=True)
        o_ref[0] = _spread_heads(out).astype(o_ref.dtype)


def _fox_decode(page_table, fq, fk, fv, small, cache_k, cache_v, cache_logf):
    nb, npg = page_table.shape
    npool, ps = cache_k.shape[:2]
    H = FOX_HEADS
    W = FOX_WIDTH
    P = PAGES_PER_STEP
    k_t = jnp.transpose(cache_k, (0, 2, 3, 1))
    v_t = jnp.transpose(cache_v, (0, 2, 3, 1))
    lf_t = jnp.transpose(cache_logf, (0, 2, 1))
    assert npg % P == 0
    per_b = lambda *shape: pl.BlockSpec((1,) + shape, lambda b, p, pt: (b,) + (0,) * len(shape))

    def pages(*shape):
        return [pl.BlockSpec((1,) + shape, lambda b, p, pt, i=i: (pt[b, npg - 1 - (p * P + i)],) + (0,) * len(shape))
                for i in range(P)]

    heads = lambda a: a.reshape(nb, H, HEAD_DIM)
    return pl.pallas_call(
        _fox_decode_kernel,
        grid_spec=pltpu.PrefetchScalarGridSpec(
            num_scalar_prefetch=1,
            grid=(nb, npg // P),
            in_specs=[per_b(H, HEAD_DIM), per_b(H, HEAD_DIM), per_b(H, HEAD_DIM), per_b(1, SMALL_W)]
            + pages(H, HEAD_DIM, ps) + pages(H, HEAD_DIM, ps) + pages(H, ps),
            out_specs=per_b(1, H * LANES),
            scratch_shapes=[pltpu.VMEM((H, W), BF16), pltpu.VMEM((H, 1), F32), pltpu.VMEM((H, 1), F32),
                            pltpu.VMEM((H, W), F32), pltpu.VMEM((H, 1), F32)]),
        out_shape=jax.ShapeDtypeStruct((nb, 1, H * LANES), BF16),
        compiler_params=_params(("arbitrary", "arbitrary"), 48),
        name="fox_decode",
    )(page_table, heads(fq), heads(fk), heads(fv), small.reshape(nb, 1, SMALL_W),
      *([k_t] * P + [v_t] * P + [lf_t] * P))


def kernel(x_prompt, x_sample, cache_k, cache_v, cache_logf, state_conv, state_delta, page_table, c_prompt, c_sample, w_mod, b_mod, w_in, conv_w, a_log, dt_bias, gdn_norm_w, fox_fb, w_out, peer_wq, peer_subkeys, peer_u, peer_v, final_norm_w):
    assert w_mod.shape[0] == 1, "one layer"
    D = x_prompt.shape[-1]
    Bp, L = x_prompt.shape[:2]
    Bs = x_sample.shape[0]
    assert x_sample.shape[1] == 1 and L % GDN_CHUNK == 0 and Bs % SUBLANES == 0
    tm_in, tq, tm_out, tl, tb = (min(t, L) for t in TILES)

    c = jnp.concatenate([c_prompt, c_sample], axis=0)
    mod = _mod(c, w_mod[0], b_mod[0])
    mod_p = [mod[:Bp, None, j * D:(j + 1) * D] for j in range(N_MOD)]
    pad_s = (-Bs) % LANES
    mod_s = [mod[None, Bp:, j * D:(j + 1) * D] for j in range(N_MOD)]
    mod_s_pad = [jnp.pad(m, ((0, 0), (0, pad_s), (0, 0))) for m in mod_s]

    w_cat, padd, alog = _prep_inproj(w_in[0], a_log[0], dt_bias[0], fox_fb[0])
    wg, wf, wq = _prep_outproj(w_out[0], peer_wq[0])
    subkeys = peer_subkeys[0].reshape(PEER_HEADS * 2, N_KEYS, PEER_DK_HALF).astype(BF16)
    wts = (wg, wf, wq, subkeys, _pack_table(peer_u[0]), _pack_table(peer_v[0]), final_norm_w)

    xp = x_prompt.reshape(Bp * L, D)
    gqkv, z, small, fkt, fvt, qa, ka, vat = _inproj(xp, mod_p[0], mod_p[1], w_cat, padd, alog, tm_in, L // tm_in)
    og, delta_p = _gdn_prompt(gqkv, z, small, conv_w[0], gdn_norm_w[0], Bp, L)
    of = _fox_prompt(qa, ka, vat, Bp, L, tq)
    y_p = _post_mixers(og, of, xp, mod_p[2], mod_p[3], mod_p[4], mod_p[5], wts, tm_out, L, tl, tb)

    xs = x_sample.reshape(Bs, D)
    gqkv_s, z_s, small_s, fk_s, fv_s, fq_s = _inproj(xs, mod_s[0], mod_s[1], w_cat, padd, alog, Bs, 1, decode=True)
    og_s, conv_s, delta_s = _gdn_step(gqkv_s, z_s, small_s, state_conv[0], state_delta[0], conv_w[0], gdn_norm_w[0])
    of_s = _fox_decode(page_table, fq_s, fk_s, fv_s, small_s, cache_k[0], cache_v[0], cache_logf[0])
    rows = lambda a: jnp.pad(a.reshape(Bs, -1), ((0, pad_s), (0, 0)))
    ts = Bs + pad_s
    y_s = _post_mixers(rows(og_s), rows(of_s), rows(xs), mod_s_pad[2], mod_s_pad[3], mod_s_pad[4], mod_s_pad[5], wts,
                       ts, ts, ts, min(tb, ts))[:Bs]

    nf = 2 * GDN_HEADS
    cache_rows = lambda t: jnp.transpose(t.reshape(Bp, FOX_HEADS, HEAD_DIM, L), (0, 3, 1, 2))[None]
    return (y_p.reshape(Bp, L, D), y_s.reshape(Bs, 1, D), cache_rows(fkt), cache_rows(fvt),
            small[:, nf:nf + FOX_HEADS].reshape(1, Bp, L, FOX_HEADS),
            gqkv.reshape(Bp, L, -1)[None, :, L - (CONV_WIDTH - 1):], delta_p[None],
            fk_s.reshape(1, Bs, 1, FOX_HEADS, HEAD_DIM), fv_s.reshape(1, Bs, 1, FOX_HEADS, HEAD_DIM),
            small_s[:, nf:nf + FOX_HEADS].reshape(1, Bs, 1, FOX_HEADS), conv_s[None], delta_s[None])
```

```python
import functools

import numpy as np
import jax
import jax.numpy as jnp
from jax import lax
from jax.experimental import pallas as pl
from jax.experimental.pallas import tpu as pltpu

F32 = jnp.float32
BF16 = jnp.bfloat16
I32 = jnp.int32

HEAD_DIM = 64
GDN_HEADS = 8
FOX_HEADS = 8
GDN_WIDTH = GDN_HEADS * HEAD_DIM
FOX_WIDTH = FOX_HEADS * HEAD_DIM
CONV_WIDTH = 4
PEER_HEADS = 8
N_KEYS = 128
PEER_TOPK = 16
PEER_DK_HALF = 128
N_MOD = 6
EPS = 1e-6
LANES = 128
SUBLANES = 8
GDN_CHUNK = 128
SMALL_W = LANES
BF16_TILE_ROWS = 16
TABLE_ROWS_LOG2 = 13
TABLE_ROWS = 1 << TABLE_ROWS_LOG2
TILE_ROWS = BF16_TILE_ROWS
EXPERTS_PER_DOT = 16
U_GROUPS_PER_ITER = 8
V_GROUPS_PER_ITER = 8
PAGES_PER_STEP = 16
TOPK_SETS_PER_STEP = 4
FOX_ROW_SPLIT = 4
HIGHEST = lax.Precision.HIGHEST
NEG_INF = float("-inf")


def _dot(a, b, precision=None):
    return jnp.dot(a, b, preferred_element_type=F32, precision=precision)


def _dot_nt(a, b, precision=None):
    return lax.dot_general(a, b, (((1,), (1,)), ((), ())), preferred_element_type=F32, precision=precision)


def _bdot(a, b):
    return _dot(a.astype(BF16), b.astype(BF16))


def _bdot_nt(a, b):
    return _dot_nt(a.astype(BF16), b.astype(BF16))


def _split3(x):
    hi = x.astype(BF16)
    r1 = x - hi.astype(F32)
    mid = r1.astype(BF16)
    lo = (r1 - mid.astype(F32)).astype(BF16)
    return hi, mid, lo


def _dot3(a, b):
    ah = a.astype(BF16)
    al = (a - ah.astype(F32)).astype(BF16)
    bh = b.astype(BF16)
    bl = (b - bh.astype(F32)).astype(BF16)
    return _dot(ah, bh) + (_dot(ah, bl) + _dot(al, bh))


def _params(sem, vmem_mb=None):
    kw = dict(dimension_semantics=sem)
    if vmem_mb is not None:
        kw["vmem_limit_bytes"] = vmem_mb * 1024 * 1024
    return pltpu.CompilerParams(**kw)


def _rms(x):
    return x * lax.rsqrt(jnp.mean(x * x, axis=-1, keepdims=True) + EPS)


def _silu(x):
    return x * jax.nn.sigmoid(x)


def _mod_kernel(c_ref, w_ref, b_ref, o_ref):
    s = _silu(c_ref[...])
    o_ref[...] = _bdot(s, w_ref[...]) + b_ref[...]


def _mod(c, w, b):
    n, d = c.shape
    nout = w.shape[1]
    tn = 1024
    return pl.pallas_call(
        _mod_kernel,
        grid=(nout // tn,),
        in_specs=[pl.BlockSpec((n, d), lambda j: (0, 0)),
                  pl.BlockSpec((d, tn), lambda j: (0, j)),
                  pl.BlockSpec((1, tn), lambda j: (0, j))],
        out_specs=pl.BlockSpec((n, tn), lambda j: (0, j)),
        out_shape=jax.ShapeDtypeStruct((n, nout), F32),
        compiler_params=_params(("arbitrary",)),
        name="mod",
    )(c, w, b.reshape(1, nout))


def _spread_heads(a):
    tm = a.shape[0]
    lane = lax.broadcasted_iota(I32, (tm, LANES), 1)
    low = lane < HEAD_DIM
    out = []
    for j in range(a.shape[1] // LANES):
        blk = a[:, j * LANES:(j + 1) * LANES]
        out.append(jnp.where(low, blk, 0.0))
        out.append(jnp.where(low, pltpu.roll(blk, HEAD_DIM, axis=1), 0.0))
    return jnp.concatenate(out, axis=1)


def _inproj_kernel(tiles_per_seq, x_ref, sh_ref, sc_ref, w_ref, padd_ref, alog_ref, selq_ref, selk_ref, cq_ref,
                   ck_ref, cv_ref, gqkv_ref, z_ref, small_ref, fk_ref, fv_ref, *rest):
    carry_ref = rest[-1]
    i = pl.program_id(0)
    tm = x_ref.shape[0]
    h = _rms(x_ref[...]) * (1.0 + sc_ref[0]) + sh_ref[0]
    hb = h.astype(BF16)
    o = 0
    gqkv_ref[...] = _dot(hb, w_ref[:, o:o + 3 * GDN_WIDTH]); o += 3 * GDN_WIDTH
    z_ref[...] = _dot(hb, w_ref[:, o:o + GDN_WIDTH]); o += GDN_WIDTH
    sm = _dot(hb, w_ref[:, o:o + SMALL_W]); o += SMALL_W
    fq = _dot(hb, w_ref[:, o:o + FOX_WIDTH]); o += FOX_WIDTH
    fk = _dot(hb, w_ref[:, o:o + FOX_WIDTH]); o += FOX_WIDTH
    fv = _dot(hb, w_ref[:, o:o + FOX_WIDTH])
    if len(rest) == 2:
        fk_ref[...] = fk
        fv_ref[...] = fv
    else:
        for ref, val in ((fk_ref, fk), (fv_ref, fv)):
            for j in range(FOX_WIDTH // LANES):
                for t in range(tm // LANES):
                    ref[0, j * LANES:(j + 1) * LANES, t * LANES:(t + 1) * LANES] = (
                        val[t * LANES:(t + 1) * LANES, j * LANES:(j + 1) * LANES].T)

    lane = lax.broadcasted_iota(I32, (tm, SMALL_W), 1)
    y = sm + padd_ref[...]
    t = jnp.log1p(jnp.exp(-jnp.abs(y)))
    softplus = jnp.maximum(y, 0.0) + t
    logsig = jnp.minimum(y, 0.0) - t
    small = jnp.where(lane < GDN_HEADS, -jnp.exp(alog_ref[...]) * softplus,
                      jnp.where(lane < 2 * GDN_HEADS, jax.nn.sigmoid(sm),
                                jnp.where(lane < 2 * GDN_HEADS + FOX_HEADS, logsig, 0.0)))
    small_ref[...] = small
    if len(rest) == 2:
        rest[0][...] = fq
        return
    qa_ref, ka_ref, vat_ref = rest[:3]

    @pl.when(i % tiles_per_seq == 0)
    def _():
        carry_ref[...] = jnp.zeros_like(carry_ref)

    r = lax.broadcasted_iota(I32, (tm, tm), 0)
    c = lax.broadcasted_iota(I32, (tm, tm), 1)
    ltri = (c <= r).astype(F32)
    cum = _dot(ltri, small, precision=HIGHEST) + carry_ref[...]
    carry_ref[...] = cum[tm - 1:tm, :]
    hi, mid, lo = _split3(cum)
    parts = jnp.concatenate([hi, mid, lo], axis=1)
    qa_ref[...] = (_spread_heads(fq * (HEAD_DIM ** -0.5)) + _dot(parts, selq_ref[...]) + cq_ref[...]).astype(BF16)
    ka_ref[...] = (_spread_heads(fk) + _dot(parts, selk_ref[...]) + ck_ref[...]).astype(BF16)
    va = _spread_heads(fv) + cv_ref[...]
    for j in range(va.shape[1] // LANES):
        for t in range(tm // LANES):
            vat_ref[j * LANES:(j + 1) * LANES, t * LANES:(t + 1) * LANES] = (
                va[t * LANES:(t + 1) * LANES, j * LANES:(j + 1) * LANES].T.astype(BF16))


def _inproj_consts():
    selq = np.zeros((3 * SMALL_W, FOX_HEADS * LANES), np.float32)
    selk = np.zeros((3 * SMALL_W, FOX_HEADS * LANES), np.float32)
    cq = np.zeros((1, FOX_HEADS * LANES), np.float32)
    ck = np.zeros((1, FOX_HEADS * LANES), np.float32)
    cv = np.zeros((1, FOX_HEADS * LANES), np.float32)
    for h in range(FOX_HEADS):
        base = h * LANES + HEAD_DIM
        for p in range(3):
            src = p * SMALL_W + 2 * GDN_HEADS + h
            selq[src, base + p] = 1.0
            ck[0, base + p] = 1.0
            selk[src, base + 3 + p] = -1.0
            cq[0, base + 3 + p] = 1.0
        cv[0, base] = 1.0
    return (jnp.asarray(selq, BF16), jnp.asarray(selk, BF16), jnp.asarray(cq), jnp.asarray(ck), jnp.asarray(cv))


def _prep_inproj(w_in, a_log, dt_bias, fox_fb):
    o = np.cumsum((0, GDN_WIDTH, GDN_WIDTH, GDN_WIDTH, GDN_WIDTH, GDN_HEADS, GDN_HEADS, FOX_WIDTH, FOX_WIDTH,
                   FOX_WIDTH, FOX_HEADS)).tolist()
    d = w_in.shape[0]
    nsmall = 2 * GDN_HEADS + FOX_HEADS
    small = jnp.concatenate([w_in[:, o[4]:o[6]], w_in[:, o[9]:o[10]], jnp.zeros((d, SMALL_W - nsmall), w_in.dtype)], 1)
    w_cat = jnp.concatenate([w_in[:, :o[4]], small, w_in[:, o[6]:o[9]]], axis=1).astype(BF16)
    zero = jnp.zeros((GDN_HEADS,), F32)
    tail = jnp.zeros((SMALL_W - nsmall,), F32)
    padd = jnp.concatenate([dt_bias.astype(F32), zero, fox_fb.astype(F32), tail]).reshape(1, SMALL_W)
    alog = jnp.concatenate([a_log.astype(F32), zero, zero, tail]).reshape(1, SMALL_W)
    return w_cat, padd, alog


def _inproj(x, sh, sc, w_cat, padd, pmul, tm, tiles_per_seq, decode=False):
    T, D = x.shape
    rm = sh.shape[1]
    nw = w_cat.shape[1]
    selq, selk, cq, ck, cv = _inproj_consts()
    aw = FOX_HEADS * LANES
    row = lambda w: pl.BlockSpec((tm, w), lambda i: (i, 0))
    const = lambda a: pl.BlockSpec(a.shape, lambda i: (0,) * a.ndim)
    seq = pl.BlockSpec((1, rm, D), lambda i: (i // tiles_per_seq, 0, 0))
    outs = [(3 * GDN_WIDTH, F32), (GDN_WIDTH, F32), (SMALL_W, F32)]
    outs += [(FOX_WIDTH, F32)] * 3 if decode else [(aw, BF16), (aw, BF16)]
    out_specs = [row(w) for w, _ in outs]
    out_shape = [jax.ShapeDtypeStruct((T, w), dt) for w, dt in outs]
    if not decode:
        nseq = T // (tm * tiles_per_seq)
        kv_spec = pl.BlockSpec((1, FOX_WIDTH, tm), lambda i: (i // tiles_per_seq, 0, i % tiles_per_seq))
        kv_shape = jax.ShapeDtypeStruct((nseq, FOX_WIDTH, tm * tiles_per_seq), F32)
        out_specs[3:3] = [kv_spec, kv_spec]
        out_shape[3:3] = [kv_shape, kv_shape]
        out_specs.append(pl.BlockSpec((aw, tm), lambda i: (0, i)))
        out_shape.append(jax.ShapeDtypeStruct((aw, T), BF16))
    return pl.pallas_call(
        functools.partial(_inproj_kernel, tiles_per_seq),
        grid=(T // tm,),
        in_specs=[row(D), seq, seq, const(w_cat), const(padd), const(pmul), const(selq), const(selk), const(cq),
                  const(ck), const(cv)],
        out_specs=out_specs,
        out_shape=out_shape,
        scratch_shapes=[pltpu.VMEM((1, SMALL_W), F32)],
        compiler_params=_params(("arbitrary",), 48),
        name="inproj",
    )(x, sh, sc, w_cat, padd, pmul, selq, selk, cq, ck, cv)


def _head_block(a, base, h, low):
    j, odd = divmod(h, 2)
    blk = a[:, base + j * LANES:base + (j + 1) * LANES]
    if odd:
        blk = pltpu.roll(blk, HEAD_DIM, axis=1)
    return jnp.where(low, blk, 0.0)


def _join_heads(heads):
    return jnp.concatenate([heads[2 * j] + pltpu.roll(heads[2 * j + 1], HEAD_DIM, axis=1)
                            for j in range(len(heads) // 2)], axis=1)


def _unit_lower_inverses(a_mats, r, cc):
    n = a_mats[0].shape[0]
    s = SUBLANES
    same = (r // s) == (cc // s)
    eye = jnp.where(r == cc, 1.0, 0.0)
    bs = [jnp.where(same, -a, 0.0) for a in a_mats]
    ts = [eye + b for b in bs]
    b2s = [_dot3(b, b) for b in bs]
    ts = [t + _dot3(t, b2) for t, b2 in zip(ts, b2s)]
    b4s = [_dot3(b2, b2) for b2 in b2s]
    ts = [t + _dot3(t, b4) for t, b4 in zip(ts, b4s)]
    while s < n:
        same2 = (r // (2 * s)) == (cc // (2 * s))
        new = same2 & jnp.logical_not(same)
        xs = [_dot3(t, jnp.where(new, a, 0.0)) for t, a in zip(ts, a_mats)]
        ts = [t - _dot3(x, t) for t, x in zip(ts, xs)]
        same = same2
        s *= 2
    return ts


def _gdn_kernel(u_ref, z_ref, sm_ref, cw_ref, nw_ref, o_ref, s_out_ref, ubuf, s_ref):
    c = pl.program_id(1)
    C = u_ref.shape[0]
    W3 = 3 * GDN_WIDTH

    @pl.when(c == 0)
    def _():
        ubuf[0:SUBLANES, :] = jnp.zeros((SUBLANES, W3), F32)
        s_ref[...] = jnp.zeros_like(s_ref)

    ubuf[SUBLANES:SUBLANES + C, :] = u_ref[...]
    cw = cw_ref[...]
    conv = cw[CONV_WIDTH - 1:CONV_WIDTH] * ubuf[SUBLANES:SUBLANES + C, :]
    for j in range(1, CONV_WIDTH):
        conv = conv + cw[CONV_WIDTH - 1 - j:CONV_WIDTH - j] * ubuf[SUBLANES - j:SUBLANES - j + C, :]
    ubuf[0:SUBLANES, :] = ubuf[C:C + SUBLANES, :]
    a = _silu(conv)
    z = z_ref[...]
    small = sm_ref[...]

    lane = lax.broadcasted_iota(I32, (C, LANES), 1)
    low = lane < HEAD_DIM
    r = lax.broadcasted_iota(I32, (C, C), 0)
    cc = lax.broadcasted_iota(I32, (C, C), 1)
    causal = cc <= r
    strict = cc < r
    g_cum = _dot(causal.astype(F32), small, precision=HIGHEST)
    g_last = g_cum[C - 1:C, :]
    e_g = jnp.exp(g_cum)
    e_gl = jnp.exp(g_last - g_cum)
    e_last = jnp.exp(g_last)
    pieces = [p.astype(F32) for p in _split3(g_cum)]
    pos_parts = [pieces[0]] + [pltpu.roll(p, GDN_HEADS * i, axis=1) for i, p in enumerate(pieces) if i]
    neg_parts = [pltpu.roll(-p, HEAD_DIM + GDN_HEADS * i, axis=1) for i, p in enumerate(pieces)]

    def g_operands(h):
        p_mat = jnp.zeros((C, LANES), F32)
        q_mat = jnp.zeros((C, LANES), F32)
        for i in range(3):
            lo_lane = lane == h + GDN_HEADS * i
            hi_lane = lane == h + HEAD_DIM + GDN_HEADS * i
            p_mat = jnp.where(lo_lane, pos_parts[i], jnp.where(hi_lane, 1.0, p_mat))
            q_mat = jnp.where(lo_lane, 1.0, jnp.where(hi_lane, neg_parts[i], q_mat))
        return p_mat.astype(BF16), q_mat.astype(BF16)

    nw = nw_ref[...]
    heads = range(GDN_HEADS)
    col = lambda m, h: m[:, h:h + 1]
    cqs = [_head_block(a, 0, h, low) for h in heads]
    cks = [_head_block(a, GDN_WIDTH, h, low) for h in heads]
    vs = [_head_block(a, 2 * GDN_WIDTH, h, low) for h in heads]
    qs = [cq * lax.rsqrt(jnp.sum(cq * cq, axis=1, keepdims=True) + 1e-6) * (HEAD_DIM ** -0.5) for cq in cqs]
    ks = [ck * lax.rsqrt(jnp.sum(ck * ck, axis=1, keepdims=True) + 1e-6) for ck in cks]
    betas = [col(small, GDN_HEADS + h) for h in heads]
    g_diffs = [_dot_nt(*g_operands(h)) for h in heads]
    decays = [jnp.exp(jnp.where(causal, g, NEG_INF)) for g in g_diffs]
    a_mats = [jnp.where(strict, betas[h] * _bdot_nt(ks[h], ks[h]) * decays[h], 0.0) for h in heads]
    rhss = [vs[h] * betas[h] + pltpu.roll(ks[h] * (betas[h] * col(e_g, h)), HEAD_DIM, axis=1) for h in heads]
    invs = _unit_lower_inverses(a_mats, r, cc)
    xs = [_dot3(invs[h], rhss[h]) for h in heads]
    u_mats = [jnp.where(low, x, 0.0) for x in xs]
    w_mats = [jnp.where(low, pltpu.roll(x, HEAD_DIM, axis=1), 0.0) for x in xs]
    a_qks = [_bdot_nt(qs[h], ks[h]) * decays[h] for h in heads]
    s_olds = [s_ref[h] for h in heads]
    vns = [u_mats[h] - _bdot(w_mats[h], s_olds[h]) for h in heads]
    os_ = [_bdot(qs[h] * col(e_g, h), s_olds[h]) + _bdot(a_qks[h], vns[h]) for h in heads]
    for h in heads:
        s_ref[h] = s_olds[h] * col(e_last, h) + _bdot((ks[h] * col(e_gl, h)).T, vns[h])
    outs = []
    for h in heads:
        o = os_[h]
        ms = jnp.sum(o * o, axis=1, keepdims=True) * (1.0 / HEAD_DIM)
        outs.append(o * lax.rsqrt(ms + EPS) * nw * _silu(_head_block(z, 0, h, low)))
    o_ref[...] = _join_heads(outs).astype(o_ref.dtype)

    @pl.when(c == pl.num_programs(1) - 1)
    def _():
        s_out_ref[0] = s_ref[:, 0:HEAD_DIM, 0:HEAD_DIM]


def _gdn_prompt(gqkv, z, small, conv_w, norm_w, nb, L):
    C = GDN_CHUNK
    nch = L // C
    W3 = 3 * GDN_WIDTH
    nw = jnp.concatenate([norm_w.astype(F32), jnp.zeros((LANES - HEAD_DIM,), F32)]).reshape(1, LANES)
    row = lambda w: pl.BlockSpec((C, w), lambda b, c: (b * nch + c, 0))
    return pl.pallas_call(
        _gdn_kernel,
        grid=(nb, nch),
        in_specs=[row(W3), row(GDN_WIDTH), row(SMALL_W),
                  pl.BlockSpec((CONV_WIDTH, W3), lambda b, c: (0, 0)),
                  pl.BlockSpec((1, LANES), lambda b, c: (0, 0))],
        out_specs=[row(GDN_WIDTH),
                   pl.BlockSpec((1, GDN_HEADS, HEAD_DIM, HEAD_DIM), lambda b, c: (b, 0, 0, 0))],
        out_shape=[jax.ShapeDtypeStruct((nb * L, GDN_WIDTH), BF16),
                   jax.ShapeDtypeStruct((nb, GDN_HEADS, HEAD_DIM, HEAD_DIM), F32)],
        scratch_shapes=[pltpu.VMEM((SUBLANES + C, W3), F32), pltpu.VMEM((GDN_HEADS, LANES, LANES), F32)],
        compiler_params=_params(("arbitrary", "arbitrary")),
        name="gdn_prompt",
    )(gqkv, z, small, conv_w, nw)


def _fox_kernel(qt_ref, kt_ref, q_ref, k_ref, vt_ref, o_ref, m_ref, acc_ref):
    p = pl.program_id(2)
    qi = qt_ref[p]
    ki = kt_ref[p]
    tq = q_ref.shape[0]
    tk = k_ref.shape[0]

    @pl.when(ki == 0)
    def _():
        m_ref[...] = jnp.full_like(m_ref, NEG_INF)
        acc_ref[...] = jnp.zeros_like(acc_ref)

    def step(on_diagonal):
        nsplit = min(FOX_ROW_SPLIT, tq // LANES)
        tr = tq // nsplit
        blocks = [slice(i * tr, (i + 1) * tr) for i in range(nsplit)]
        k = k_ref[...]
        vt = vt_ref[...]
        ss = [_dot_nt(k, q_ref[b, :]) for b in blocks]
        if on_diagonal:
            key = lax.broadcasted_iota(I32, (tk, tr), 0)
            qry = lax.broadcasted_iota(I32, (tk, tr), 1)
            ss = [jnp.where(key <= qry + i * tr, s, NEG_INF) for i, s in enumerate(ss)]
        m_olds = [m_ref[:, b] for b in blocks]
        m_news = [jnp.maximum(m, jnp.max(s, axis=0, keepdims=True)) for m, s in zip(m_olds, ss)]
        ps = [jnp.exp(s - m).astype(BF16) for s, m in zip(ss, m_news)]
        for b, m_old, m_new, p in zip(blocks, m_olds, m_news, ps):
            acc_ref[:, b] = acc_ref[:, b] * jnp.exp(m_old - m_new) + _dot(vt, p)
            m_ref[:, b] = m_new

    @pl.when(ki < qi)
    def _():
        step(False)

    @pl.when(ki == qi)
    def _():
        step(True)
        acc = acc_ref[...]
        out = acc / acc[HEAD_DIM:HEAD_DIM + 1, :]
        for j in range(tq // LANES):
            o_ref[j * LANES:(j + 1) * LANES, :] = out[:, j * LANES:(j + 1) * LANES].T.astype(o_ref.dtype)


def _fox_prompt(qa, ka, vat, nb, L, tq):
    nq = L // tq
    assert tq % LANES == 0
    pairs = [(i, j) for i in range(nq) for j in range(i + 1)]
    qt = jnp.asarray([p[0] for p in pairs], I32)
    kt = jnp.asarray([p[1] for p in pairs], I32)
    qspec = pl.BlockSpec((tq, LANES), lambda b, h, p, qt, kt: (b * nq + qt[p], h))
    kspec = pl.BlockSpec((tq, LANES), lambda b, h, p, qt, kt: (b * nq + kt[p], h))
    vspec = pl.BlockSpec((LANES, tq), lambda b, h, p, qt, kt: (h, b * nq + kt[p]))
    return pl.pallas_call(
        _fox_kernel,
        grid_spec=pltpu.PrefetchScalarGridSpec(
            num_scalar_prefetch=2,
            grid=(nb, FOX_HEADS, len(pairs)),
            in_specs=[qspec, kspec, vspec],
            out_specs=qspec,
            scratch_shapes=[pltpu.VMEM((1, tq), F32), pltpu.VMEM((LANES, tq), F32)]),
        out_shape=jax.ShapeDtypeStruct(qa.shape, BF16),
        compiler_params=_params(("arbitrary", "arbitrary", "arbitrary")),
        name="fox_prompt",
    )(qt, kt, qa, ka, vat)


def _outproj_kernel(og_ref, of_ref, x_ref, g1_ref, sh_ref, sc_ref, wg_ref, wf_ref, wq_ref, x1_ref, h2_ref, qp_ref):
    m = _dot(og_ref[...], wg_ref[...]) + _dot(of_ref[...], wf_ref[...])
    x1 = x_ref[...] + g1_ref[0] * m
    x1_ref[...] = x1
    h2 = _rms(x1) * (1.0 + sc_ref[0]) + sh_ref[0]
    h2_ref[...] = h2
    qp_ref[...] = _dot(h2.astype(BF16), wq_ref[...]).astype(BF16)


def _prep_outproj(w_out, peer_wq):
    wg = w_out[:GDN_WIDTH].astype(BF16)
    wf = w_out[GDN_WIDTH:].reshape(FOX_HEADS, HEAD_DIM, -1)
    wf = jnp.pad(wf, ((0, 0), (0, LANES - HEAD_DIM), (0, 0))).reshape(FOX_HEADS * LANES, -1).astype(BF16)
    return wg, wf, peer_wq.astype(BF16)


def _outproj(og, of, x, g1, sh2, sc2, wg, wf, wq, tm, tiles_per_seq):
    T, D = x.shape
    rm = g1.shape[1]
    nq = wq.shape[1]
    row = lambda w: pl.BlockSpec((tm, w), lambda i: (i, 0))
    const = lambda a: pl.BlockSpec(a.shape, lambda i: (0,) * a.ndim)
    seq = pl.BlockSpec((1, rm, D), lambda i: (i // tiles_per_seq, 0, 0))
    return pl.pallas_call(
        _outproj_kernel,
        grid=(T // tm,),
        in_specs=[row(og.shape[1]), row(of.shape[1]), row(D), seq, seq, seq, const(wg), const(wf), const(wq)],
        out_specs=[row(D), row(D), row(nq)],
        out_shape=[jax.ShapeDtypeStruct((T, D), F32), jax.ShapeDtypeStruct((T, D), F32),
                   jax.ShapeDtypeStruct((T, nq), BF16)],
        compiler_params=_params(("arbitrary",), 48),
        name="outproj",
    )(og, of, x, g1, sh2, sc2, wg, wf, wq)


def _topk_rows(ss, k):
    n = ss[0].shape[0]
    iota_n = lax.broadcasted_iota(I32, ss[0].shape, 0).astype(F32)
    vals = [[] for _ in ss]
    idxs = [[] for _ in ss]
    for _ in range(k):
        ms = [jnp.max(s, axis=0, keepdims=True) for s in ss]
        ids = [jnp.min(jnp.where(s == m, iota_n, float(n)), axis=0, keepdims=True) for s, m in zip(ss, ms)]
        ss = [jnp.where(iota_n == i, NEG_INF, s) for s, i in zip(ss, ids)]
        for j, (m, i) in enumerate(zip(ms, ids)):
            vals[j].append(m)
            idxs[j].append(i)
    return [jnp.concatenate(v, axis=0) for v in vals], [jnp.concatenate(i, axis=0) for i in idxs]


def _pair_rows(a0, a1, op):
    k = PEER_TOPK
    rows = [op(a0[0:1], a1[0:SUBLANES]), op(a0[0:1], a1[SUBLANES:k])]
    rows += [op(a0[a:a + 1], a1[0:SUBLANES]) for a in range(1, SUBLANES)]
    rows.append(op(a0[SUBLANES:k], a1[0:1]))
    return jnp.concatenate(rows, axis=0)


def _topk_kernel(q_ref, sk_ref, r_ref, gx_ref, sv_ref, si_ref):
    hp = pl.program_id(1)
    tl = q_ref.shape[0]
    k = PEER_TOPK
    nsub = sk_ref.shape[0]
    dk = sk_ref.shape[2]
    ss = [_dot_nt(sk_ref[i], q_ref[:, i * dk:(i + 1) * dk]) for i in range(nsub)]
    vals, idxs = _topk_rows(ss, k)
    for i in range(nsub):
        sv_ref[hp * nsub + i] = vals[i]
        si_ref[hp * nsub + i] = idxs[i]

    @pl.when(hp == pl.num_programs(1) - 1)
    def _():
        nrow = 10 * SUBLANES
        ridx = lax.broadcasted_iota(I32, (nrow, tl), 0)
        blk = ridx // SUBLANES
        w = ridx % SUBLANES
        flat = jnp.where(blk == 0, w, jnp.where(blk == 1, SUBLANES + w,
                         jnp.where(blk <= SUBLANES, k * (blk - 1) + w, k * (SUBLANES + w)))).astype(F32)
        heads = range(PEER_HEADS)
        cands = [_pair_rows(sv_ref[2 * h], sv_ref[2 * h + 1], lambda a, b: a + b) for h in heads]
        ecands = [_pair_rows(si_ref[2 * h], si_ref[2 * h + 1], lambda a, b: a * float(N_KEYS) + b) for h in heads]
        fvs = [[] for _ in heads]
        es = [[] for _ in heads]
        for _ in range(k):
            ms = [jnp.max(c, axis=0, keepdims=True) for c in cands]
            fs = [jnp.min(jnp.where(c == m, flat, float(k * k)), axis=0, keepdims=True) for c, m in zip(cands, ms)]
            sels = [flat == f for f in fs]
            for h in heads:
                es[h].append(jnp.max(jnp.where(sels[h], ecands[h], -1.0), axis=0, keepdims=True))
                fvs[h].append(ms[h])
            cands = [jnp.where(sel, NEG_INF, c) for sel, c in zip(sels, cands)]
        g_all = []
        for h in heads:
            fv = jnp.concatenate(fvs[h], axis=0)
            ex = jnp.exp(fv - fv[0:1])
            g_all.append(ex / jnp.sum(ex, axis=0, keepdims=True))
        e_mat = jnp.concatenate([jnp.concatenate(e, axis=0) for e in es], axis=0).astype(I32)
        g_mat = jnp.concatenate(g_all, axis=0)
        row_mat = (e_mat & (TABLE_ROWS - 1)).astype(F32)
        half_mat = e_mat >> TABLE_ROWS_LOG2
        for j in range(tl // LANES):
            tok = slice(j * LANES, (j + 1) * LANES)
            r_ref[tok, :] = row_mat[:, tok].T.astype(I32)
        par = lax.broadcasted_iota(I32, (TILE_ROWS, tl), 0) % 2
        for i in range(PEER_HEADS * k // SUBLANES):
            rows = [jnp.where(par == half_mat[n:n + 1], g_mat[n:n + 1], 0.0)
                    for n in range(i * SUBLANES, (i + 1) * SUBLANES)]
            blk = jnp.concatenate(rows, axis=0)
            for j in range(tl // LANES):
                tok = slice(j * LANES, (j + 1) * LANES)
                gx_ref[tok, i * LANES:(i + 1) * LANES] = blk[:, tok].T


def _topk(qp, subkeys, tl):
    T = qp.shape[0]
    nhp = subkeys.shape[0]
    ne = PEER_HEADS * PEER_TOPK
    out = lambda w: pl.BlockSpec((tl, w), lambda i, hp: (i, 0))
    return pl.pallas_call(
        _topk_kernel,
        grid=(T // tl, nhp // TOPK_SETS_PER_STEP),
        in_specs=[pl.BlockSpec((tl, TOPK_SETS_PER_STEP * PEER_DK_HALF), lambda i, hp: (i, hp)),
                  pl.BlockSpec((TOPK_SETS_PER_STEP, N_KEYS, PEER_DK_HALF), lambda i, hp: (hp, 0, 0))],
        out_specs=[out(ne), out(ne * TILE_ROWS)],
        out_shape=[jax.ShapeDtypeStruct((T, ne), I32), jax.ShapeDtypeStruct((T, ne * TILE_ROWS), F32)],
        scratch_shapes=[pltpu.VMEM((nhp, PEER_TOPK, tl), F32), pltpu.VMEM((nhp, PEER_TOPK, tl), F32)],
        compiler_params=_params(("arbitrary", "arbitrary")),
        name="peer_topk",
    )(qp, subkeys)


def _pack_table(w):
    E, D = w.shape
    assert E == 2 * TABLE_ROWS and D == (TILE_ROWS // 2) * LANES
    t = w.astype(BF16).reshape(2, TABLE_ROWS, D // LANES, LANES)
    return jnp.transpose(t, (1, 2, 0, 3)).reshape(TABLE_ROWS, TILE_ROWS, LANES)


def _gather_tiles(tab_ref, r_ref, t, c):
    return jnp.concatenate([tab_ref[r_ref[t, c * EXPERTS_PER_DOT + j]] for j in range(EXPERTS_PER_DOT)], axis=0)


def _fold(x, y, d, low):
    return jnp.where(low, x, y) + pltpu.roll(jnp.where(low, y, x), SUBLANES - d, axis=0)


def _sublane_sums(vs, lows):
    for d, low in zip((1, 2, 4), lows):
        vs = [_fold(vs[2 * i], vs[2 * i + 1], d, low) for i in range(len(vs) // 2)]
    return vs[0]


def _swap8(vs):
    sub = lax.broadcasted_iota(I32, (SUBLANES, LANES), 0)
    for d in (4, 2, 1):
        low = (sub & d) == 0
        out = list(vs)
        for i in range(SUBLANES):
            if i & d == 0:
                out[i] = jnp.where(low, vs[i], pltpu.roll(vs[i + d], d, axis=0))
                out[i + d] = jnp.where(low, pltpu.roll(vs[i], SUBLANES - d, axis=0), vs[i + d])
        vs = out
    return vs


def _segment_mask():
    cw = EXPERTS_PER_DOT * TILE_ROWS
    sub = lax.broadcasted_iota(I32, (SUBLANES, cw), 0)
    lane = lax.broadcasted_iota(I32, (SUBLANES, cw), 1)
    return (lane % TILE_ROWS) // 2 == sub


def _peer_u_kernel(r_ref, x_ref, gx_ref, tab_ref, act_ref):
    tb = x_ref.shape[0]
    ne = r_ref.shape[1]
    cw = EXPERTS_PER_DOT * TILE_ROWS
    sub = lax.broadcasted_iota(I32, (SUBLANES, LANES), 0)
    lows = [(sub & d) == 0 for d in (1, 2, 4)]
    lane = lax.broadcasted_iota(I32, (SUBLANES, LANES), 1)
    seg_mask = _segment_mask()

    def group(base):
        xbs = [x.astype(BF16) for x in
               _swap8([x_ref[pl.ds(base, SUBLANES), s * LANES:(s + 1) * LANES] for s in range(SUBLANES)])]
        cols = []
        for c in range(ne // EXPERTS_PER_DOT):
            zs = [jnp.where(seg_mask, _dot_nt(xbs[tt], _gather_tiles(tab_ref, r_ref, base + tt, c)), 0.0)
                  for tt in range(SUBLANES)]
            for v in range(cw // LANES):
                col = _sublane_sums([z[:, v * LANES:(v + 1) * LANES] for z in zs], lows)
                for dist in (2, 4, 8):
                    up = pltpu.roll(col, LANES - dist, axis=1)
                    dn = pltpu.roll(col, dist, axis=1)
                    col = col + jnp.where((lane & dist) == 0, up, dn)
                cols.append(col)
        d = jnp.concatenate(cols, axis=1)
        gelu = 0.5 * d * (1.0 + lax.erf(d * (2.0 ** -0.5)))
        act_ref[pl.ds(base, SUBLANES), :] = gelu * gx_ref[pl.ds(base, SUBLANES), :]

    def groups(gi, carry):
        for g in range(U_GROUPS_PER_ITER):
            group(pl.multiple_of((gi * U_GROUPS_PER_ITER + g) * SUBLANES, SUBLANES))
        return carry

    lax.fori_loop(0, tb // (SUBLANES * U_GROUPS_PER_ITER), groups, 0)


def _peer_u(rows, x, gx, tab, tb):
    T, ne = rows.shape
    assert x.shape[1] == SUBLANES * LANES
    wide = pl.BlockSpec((tb, gx.shape[1]), lambda i: (i, 0))
    return pl.pallas_call(
        _peer_u_kernel,
        grid=(T // tb,),
        in_specs=[pl.BlockSpec((tb, ne), lambda i: (i, 0), memory_space=pltpu.SMEM),
                  pl.BlockSpec((tb, x.shape[1]), lambda i: (i, 0)),
                  wide,
                  pl.BlockSpec(tab.shape, lambda i: (0, 0, 0), pipeline_mode=pl.Buffered(1))],
        out_specs=wide,
        out_shape=jax.ShapeDtypeStruct(gx.shape, F32),
        compiler_params=_params(("arbitrary",), 48),
        name="peer_u",
    )(rows, x, gx, tab)


def _peer_v_kernel(r_ref, a_ref, x1_ref, g2_ref, fw_ref, tab_ref, y_ref):
    tb = y_ref.shape[0]
    ne = r_ref.shape[1]
    cw = EXPERTS_PER_DOT * TILE_ROWS
    seg_mask = _segment_mask()
    per_row = g2_ref.shape[1] > 1

    def group(base):
        rows8 = pl.ds(base, SUBLANES)
        a8 = a_ref[rows8, :]
        accs = []
        for tt in range(SUBLANES):
            acc = jnp.zeros((SUBLANES, LANES), F32)
            for c in range(ne // EXPERTS_PER_DOT):
                lhs = jnp.where(seg_mask, a8[tt:tt + 1, c * cw:(c + 1) * cw], 0.0).astype(BF16)
                acc = acc + _dot(lhs, _gather_tiles(tab_ref, r_ref, base + tt, c))
            accs.append(acc)
        peer = jnp.concatenate(_swap8(accs), axis=1)
        g2 = g2_ref[0, rows8, :] if per_row else g2_ref[0]
        y_ref[rows8, :] = _rms(x1_ref[rows8, :] + g2 * peer) * fw_ref[...]

    def groups(gi, carry):
        for g in range(V_GROUPS_PER_ITER):
            group(pl.multiple_of((gi * V_GROUPS_PER_ITER + g) * SUBLANES, SUBLANES))
        return carry

    lax.fori_loop(0, tb // (SUBLANES * V_GROUPS_PER_ITER), groups, 0)


def _peer_v(rows, act, x1, g2, fw, tab, tb, tiles_per_seq):
    T, ne = rows.shape
    D = x1.shape[1]
    assert D == SUBLANES * LANES
    if g2.shape[1] == 1:
        g2_spec = pl.BlockSpec((1, 1, D), lambda i: (i // tiles_per_seq, 0, 0))
    else:
        g2_spec = pl.BlockSpec((1, tb, D), lambda i: (i // tiles_per_seq, i % tiles_per_seq, 0))
    row = pl.BlockSpec((tb, D), lambda i: (i, 0))
    return pl.pallas_call(
        _peer_v_kernel,
        grid=(T // tb,),
        in_specs=[pl.BlockSpec((tb, ne), lambda i: (i, 0), memory_space=pltpu.SMEM),
                  pl.BlockSpec((tb, act.shape[1]), lambda i: (i, 0)),
                  row, g2_spec, pl.BlockSpec((1, D), lambda i: (0, 0)),
                  pl.BlockSpec(tab.shape, lambda i: (0, 0, 0), pipeline_mode=pl.Buffered(1))],
        out_specs=row,
        out_shape=jax.ShapeDtypeStruct((T, D), F32),
        compiler_params=_params(("arbitrary",), 48),
        name="peer_v",
    )(rows, act, x1, g2, fw.reshape(1, D).astype(F32), tab)


TILES = (256, 1024, 256, 256, 128)


def _post_mixers(og, of, x, g1, sh2, sc2, g2, wts, tm, rows_per_seq, tl, tb):
    wg, wf, wq, subkeys, tab_u, tab_v, fw = wts
    x1, h2, qp = _outproj(og, of, x, g1, sh2, sc2, wg, wf, wq, tm, rows_per_seq // tm)
    rows, gx = _topk(qp, subkeys, tl)
    act = _peer_u(rows, h2, gx, tab_u, tb)
    return _peer_v(rows, act, x1, g2, fw, tab_v, tb, rows_per_seq // tb)


def _row_to_col(row, n):
    r = lax.broadcasted_iota(I32, (n, n), 0)
    c = lax.broadcasted_iota(I32, (n, n), 1)
    return jnp.sum(jnp.where(r == c, row, 0.0), axis=1, keepdims=True)


def _gdn_step_kernel(u_ref, z_ref, sm_ref, cs_ref, s0_ref, cw_ref, nw_ref, o_ref, cs_out_ref, s_out_ref):
    u = u_ref[0]
    st = cs_ref[0]
    cw = cw_ref[...]
    conv = cw[CONV_WIDTH - 1:CONV_WIDTH] * u
    for j in range(CONV_WIDTH - 1):
        conv = conv + cw[j:j + 1] * st[j:j + 1]
    cs_out_ref[0] = jnp.concatenate([st[1:CONV_WIDTH - 1], u], axis=0)
    a = _silu(conv)
    z = z_ref[0]
    small = sm_ref[0]
    low = lax.broadcasted_iota(I32, (1, LANES), 1) < HEAD_DIM
    nw = nw_ref[...][:, :HEAD_DIM]
    outs = []
    for h in range(GDN_HEADS):
        cq = _head_block(a, 0, h, low)[:, :HEAD_DIM]
        ck = _head_block(a, GDN_WIDTH, h, low)[:, :HEAD_DIM]
        v = _head_block(a, 2 * GDN_WIDTH, h, low)[:, :HEAD_DIM]
        q = cq * lax.rsqrt(jnp.sum(cq * cq, axis=1, keepdims=True) + 1e-6) * (HEAD_DIM ** -0.5)
        k = ck * lax.rsqrt(jnp.sum(ck * ck, axis=1, keepdims=True) + 1e-6)
        g = small[:, h:h + 1]
        beta = small[:, GDN_HEADS + h:GDN_HEADS + h + 1]
        eg = jnp.exp(g)
        s0 = s0_ref[0, h]
        w_col = _row_to_col(k * (beta * eg), HEAD_DIM)
        vn = v * beta - jnp.sum(w_col * s0, axis=0, keepdims=True)
        q_col = _row_to_col(q * eg, HEAD_DIM)
        o = jnp.sum(q_col * s0, axis=0, keepdims=True) + jnp.sum(q * k, axis=1, keepdims=True) * vn
        s_out_ref[0, h] = s0 * eg + _row_to_col(k, HEAD_DIM) * vn
        zh = _head_block(z, 0, h, low)[:, :HEAD_DIM]
        outs.append(_rms(o) * nw * _silu(zh))
    o_ref[0] = jnp.concatenate(outs, axis=1).astype(o_ref.dtype)


def _gdn_step(gqkv, z, small, state_conv, state_delta, conv_w, norm_w):
    nb = gqkv.shape[0]
    W3 = 3 * GDN_WIDTH
    nw = jnp.concatenate([norm_w.astype(F32), jnp.zeros((LANES - HEAD_DIM,), F32)]).reshape(1, LANES)
    per_b = lambda *shape: pl.BlockSpec((1,) + shape, lambda b: (b,) + (0,) * len(shape))
    return pl.pallas_call(
        _gdn_step_kernel,
        grid=(nb,),
        in_specs=[per_b(1, W3), per_b(1, GDN_WIDTH), per_b(1, SMALL_W), per_b(CONV_WIDTH - 1, W3),
                  per_b(GDN_HEADS, HEAD_DIM, HEAD_DIM),
                  pl.BlockSpec((CONV_WIDTH, W3), lambda b: (0, 0)), pl.BlockSpec((1, LANES), lambda b: (0, 0))],
        out_specs=[per_b(1, GDN_WIDTH), per_b(CONV_WIDTH - 1, W3), per_b(GDN_HEADS, HEAD_DIM, HEAD_DIM)],
        out_shape=[jax.ShapeDtypeStruct((nb, 1, GDN_WIDTH), BF16),
                   jax.ShapeDtypeStruct((nb, CONV_WIDTH - 1, W3), F32),
                   jax.ShapeDtypeStruct((nb, GDN_HEADS, HEAD_DIM, HEAD_DIM), F32)],
        compiler_params=_params(("arbitrary",)),
        name="gdn_step",
    )(gqkv.reshape(nb, 1, W3), z.reshape(nb, 1, GDN_WIDTH), small.reshape(nb, 1, SMALL_W), state_conv, state_delta,
      conv_w, nw)


def _fox_decode_kernel(pt_ref, q_ref, kn_ref, vn_ref, sm_ref, *refs):
    page_refs = refs[:-6]
    o_ref, q_s, m_ref, l_ref, acc_ref, carry_ref = refs[-6:]
    p = pl.program_id(1)
    H = FOX_HEADS
    W = FOX_WIDTH
    ps = page_refs[0].shape[3]
    diag = lax.broadcasted_iota(I32, (H, W), 1) // HEAD_DIM == lax.broadcasted_iota(I32, (H, W), 0)
    block_diag = lambda a: jnp.where(diag, jnp.concatenate([a] * H, axis=1), 0.0)

    @pl.when(p == 0)
    def _():
        q = (q_ref[0] * (HEAD_DIM ** -0.5)).astype(BF16)
        q_s[...] = block_diag(q.astype(F32)).astype(BF16)
        m_ref[...] = jnp.sum(q.astype(F32) * kn_ref[0].astype(BF16).astype(F32), axis=1, keepdims=True)
        l_ref[...] = jnp.ones_like(l_ref)
        acc_ref[...] = block_diag(vn_ref[0].astype(BF16).astype(F32))
        lane = lax.broadcasted_iota(I32, (H, SMALL_W), 1)
        row = lax.broadcasted_iota(I32, (H, SMALL_W), 0)
        carry_ref[...] = jnp.sum(jnp.where(lane == row + 2 * GDN_HEADS, sm_ref[0], 0.0), axis=1, keepdims=True)

    j = lax.broadcasted_iota(I32, (ps, ps), 0)
    pos = lax.broadcasted_iota(I32, (ps, ps), 1)
    later = (j > pos).astype(F32)
    npages = len(page_refs) // 3
    k_refs, v_refs, lf_refs = page_refs[:npages], page_refs[npages:2 * npages], page_refs[2 * npages:]
    suffix = [_dot(r[0], later, precision=HIGHEST) for r in lf_refs]
    carry = carry_ref[...]
    ss = []
    for i in range(npages):
        kt = k_refs[i][0].reshape(W, ps).astype(BF16)
        ss.append(_dot(q_s[...], kt) + (suffix[i] + carry))
        carry = carry + jnp.sum(lf_refs[i][0], axis=1, keepdims=True)
    carry_ref[...] = carry
    m_old = m_ref[...]
    m_new = m_old
    for s in ss:
        m_new = jnp.maximum(m_new, jnp.max(s, axis=1, keepdims=True))
    alpha = jnp.exp(m_old - m_new)
    pms = [jnp.exp(s - m_new) for s in ss]
    l_new = l_ref[...] * alpha
    acc = acc_ref[...] * alpha
    for i in range(npages):
        l_new = l_new + jnp.sum(pms[i], axis=1, keepdims=True)
        acc = acc + _dot_nt(pms[i].astype(BF16), v_refs[i][0].reshape(W, ps).astype(BF16))
    l_ref[...] = l_new
    acc_ref[...] = acc
    m_ref[...] = m_new

    @pl.when(p == pl.num_programs(1) - 1)
    def _():
        out = jnp.sum(jnp.where(diag, acc_ref[...] / l_ref[...], 0.0), axis=0, keepdims=True)
        o_ref[0] = _spread_heads(out).astype(o_ref.dtype)


def _fox_decode(page_table, fq, fk, fv, small, cache_k, cache_v, cache_logf):
    nb, npg = page_table.shape
    npool, ps = cache_k.shape[:2]
    H = FOX_HEADS
    W = FOX_WIDTH
    P = PAGES_PER_STEP
    k_t = jnp.transpose(cache_k, (0, 2, 3, 1))
    v_t = jnp.transpose(cache_v, (0, 2, 3, 1))
    lf_t = jnp.transpose(cache_logf, (0, 2, 1))
    assert npg % P == 0
    per_b = lambda *shape: pl.BlockSpec((1,) + shape, lambda b, p, pt: (b,) + (0,) * len(shape))

    def pages(*shape):
        return [pl.BlockSpec((1,) + shape, lambda b, p, pt, i=i: (pt[b, npg - 1 - (p * P + i)],) + (0,) * len(shape))
                for i in range(P)]

    heads = lambda a: a.reshape(nb, H, HEAD_DIM)
    return pl.pallas_call(
        _fox_decode_kernel,
        grid_spec=pltpu.PrefetchScalarGridSpec(
            num_scalar_prefetch=1,
            grid=(nb, npg // P),
            in_specs=[per_b(H, HEAD_DIM), per_b(H, HEAD_DIM), per_b(H, HEAD_DIM), per_b(1, SMALL_W)]
            + pages(H, HEAD_DIM, ps) + pages(H, HEAD_DIM, ps) + pages(H, ps),
            out_specs=per_b(1, H * LANES),
            scratch_shapes=[pltpu.VMEM((H, W), BF16), pltpu.VMEM((H, 1), F32), pltpu.VMEM((H, 1), F32),
                            pltpu.VMEM((H, W), F32), pltpu.VMEM((H, 1), F32)]),
        out_shape=jax.ShapeDtypeStruct((nb, 1, H * LANES), BF16),
        compiler_params=_params(("arbitrary", "arbitrary"), 48),
        name="fox_decode",
    )(page_table, heads(fq), heads(fk), heads(fv), small.reshape(nb, 1, SMALL_W),
      *([k_t] * P + [v_t] * P + [lf_t] * P))


def kernel(x_prompt, x_sample, cache_k, cache_v, cache_logf, state_conv, state_delta, page_table, c_prompt, c_sample, w_mod, b_mod, w_in, conv_w, a_log, dt_bias, gdn_norm_w, fox_fb, w_out, peer_wq, peer_subkeys, peer_u, peer_v, final_norm_w):
    assert w_mod.shape[0] == 1, "one layer"
    D = x_prompt.shape[-1]
    Bp, L = x_prompt.shape[:2]
    Bs = x_sample.shape[0]
    assert x_sample.shape[1] == 1 and L % GDN_CHUNK == 0 and Bs % SUBLANES == 0
    tm_in, tq, tm_out, tl, tb = (min(t, L) for t in TILES)

    c = jnp.concatenate([c_prompt, c_sample], axis=0)
    mod = _mod(c, w_mod[0], b_mod[0])
    mod_p = [mod[:Bp, None, j * D:(j + 1) * D] for j in range(N_MOD)]
    pad_s = (-Bs) % LANES
    mod_s = [mod[None, Bp:, j * D:(j + 1) * D] for j in range(N_MOD)]
    mod_s_pad = [jnp.pad(m, ((0, 0), (0, pad_s), (0, 0))) for m in mod_s]

    w_cat, padd, alog = _prep_inproj(w_in[0], a_log[0], dt_bias[0], fox_fb[0])
    wg, wf, wq = _prep_outproj(w_out[0], peer_wq[0])
    subkeys = peer_subkeys[0].reshape(PEER_HEADS * 2, N_KEYS, PEER_DK_HALF).astype(BF16)
    wts = (wg, wf, wq, subkeys, _pack_table(peer_u[0]), _pack_table(peer_v[0]), final_norm_w)

    xp = x_prompt.reshape(Bp * L, D)
    gqkv, z, small, fkt, fvt, qa, ka, vat = _inproj(xp, mod_p[0], mod_p[1], w_cat, padd, alog, tm_in, L // tm_in)
    og, delta_p = _gdn_prompt(gqkv, z, small, conv_w[0], gdn_norm_w[0], Bp, L)
    of = _fox_prompt(qa, ka, vat, Bp, L, tq)
    y_p = _post_mixers(og, of, xp, mod_p[2], mod_p[3], mod_p[4], mod_p[5], wts, tm_out, L, tl, tb)

    xs = x_sample.reshape(Bs, D)
    gqkv_s, z_s, small_s, fk_s, fv_s, fq_s = _inproj(xs, mod_s[0], mod_s[1], w_cat, padd, alog, Bs, 1, decode=True)
    og_s, conv_s, delta_s = _gdn_step(gqkv_s, z_s, small_s, state_conv[0], state_delta[0], conv_w[0], gdn_norm_w[0])
    of_s = _fox_decode(page_table, fq_s, fk_s, fv_s, small_s, cache_k[0], cache_v[0], cache_logf[0])
    rows = lambda a: jnp.pad(a.reshape(Bs, -1), ((0, pad_s), (0, 0)))
    ts = Bs + pad_s
    y_s = _post_mixers(rows(og_s), rows(of_s), rows(xs), mod_s_pad[2], mod_s_pad[3], mod_s_pad[4], mod_s_pad[5], wts,
                       ts, ts, ts, min(tb, ts))[:Bs]

    nf = 2 * GDN_HEADS
    cache_rows = lambda t: jnp.transpose(t.reshape(Bp, FOX_HEADS, HEAD_DIM, L), (0, 3, 1, 2))[None]
    return (y_p.reshape(Bp, L, D), y_s.reshape(Bs, 1, D), cache_rows(fkt), cache_rows(fvt),
            small[:, nf:nf + FOX_HEADS].reshape(1, Bp, L, FOX_HEADS),
            gqkv.reshape(Bp, L, -1)[None, :, L - (CONV_WIDTH - 1):], delta_p[None],
            fk_s.reshape(1, Bs, 1, FOX_HEADS, HEAD_DIM), fv_s.reshape(1, Bs, 1, FOX_HEADS, HEAD_DIM),
            small_s[:, nf:nf + FOX_HEADS].reshape(1, Bs, 1, FOX_HEADS), conv_s[None], delta_s[None])
```

```python
import functools

import numpy as np
import jax
import jax.numpy as jnp
from jax import lax
from jax.experimental import pallas as pl
from jax.experimental.pallas import tpu as pltpu

F32 = jnp.float32
BF16 = jnp.bfloat16
I32 = jnp.int32

HEAD_DIM = 64
GDN_HEADS = 8
FOX_HEADS = 8
GDN_WIDTH = GDN_HEADS * HEAD_DIM
FOX_WIDTH = FOX_HEADS * HEAD_DIM
CONV_WIDTH = 4
PEER_HEADS = 8
N_KEYS = 128
PEER_TOPK = 16
PEER_DK_HALF = 128
N_MOD = 6
EPS = 1e-6
LANES = 128
SUBLANES = 8
GDN_CHUNK = 128
SMALL_W = LANES
BF16_TILE_ROWS = 16
TABLE_ROWS_LOG2 = 13
TABLE_ROWS = 1 << TABLE_ROWS_LOG2
TILE_ROWS = BF16_TILE_ROWS
EXPERTS_PER_DOT = 16
U_GROUPS_PER_ITER = 16
V_GROUPS_PER_ITER = 16
PAGES_PER_STEP = 16
TOPK_SETS_PER_STEP = 4
FOX_ROW_SPLIT = 4
HIGHEST = lax.Precision.HIGHEST
NEG_INF = float("-inf")


def _dot(a, b, precision=None):
    return jnp.dot(a, b, preferred_element_type=F32, precision=precision)


def _dot_nt(a, b, precision=None):
    return lax.dot_general(a, b, (((1,), (1,)), ((), ())), preferred_element_type=F32, precision=precision)


def _bdot(a, b):
    return _dot(a.astype(BF16), b.astype(BF16))


def _bdot_nt(a, b):
    return _dot_nt(a.astype(BF16), b.astype(BF16))


def _split3(x):
    hi = x.astype(BF16)
    r1 = x - hi.astype(F32)
    mid = r1.astype(BF16)
    lo = (r1 - mid.astype(F32)).astype(BF16)
    return hi, mid, lo


def _dot3(a, b):
    ah = a.astype(BF16)
    al = (a - ah.astype(F32)).astype(BF16)
    bh = b.astype(BF16)
    bl = (b - bh.astype(F32)).astype(BF16)
    return _dot(ah, bh) + (_dot(ah, bl) + _dot(al, bh))


def _params(sem, vmem_mb=None):
    kw = dict(dimension_semantics=sem)
    if vmem_mb is not None:
        kw["vmem_limit_bytes"] = vmem_mb * 1024 * 1024
    return pltpu.CompilerParams(**kw)


def _rms(x):
    return x * lax.rsqrt(jnp.mean(x * x, axis=-1, keepdims=True) + EPS)


def _silu(x):
    return x * jax.nn.sigmoid(x)


def _mod_kernel(c_ref, w_ref, b_ref, o_ref):
    s = _silu(c_ref[...])
    o_ref[...] = _bdot(s, w_ref[...]) + b_ref[...]


def _mod(c, w, b):
    n, d = c.shape
    nout = w.shape[1]
    tn = 1024
    return pl.pallas_call(
        _mod_kernel,
        grid=(nout // tn,),
        in_specs=[pl.BlockSpec((n, d), lambda j: (0, 0)),
                  pl.BlockSpec((d, tn), lambda j: (0, j)),
                  pl.BlockSpec((1, tn), lambda j: (0, j))],
        out_specs=pl.BlockSpec((n, tn), lambda j: (0, j)),
        out_shape=jax.ShapeDtypeStruct((n, nout), F32),
        compiler_params=_params(("arbitrary",)),
        name="mod",
    )(c, w, b.reshape(1, nout))


def _spread_heads(a):
    tm = a.shape[0]
    lane = lax.broadcasted_iota(I32, (tm, LANES), 1)
    low = lane < HEAD_DIM
    out = []
    for j in range(a.shape[1] // LANES):
        blk = a[:, j * LANES:(j + 1) * LANES]
        out.append(jnp.where(low, blk, 0.0))
        out.append(jnp.where(low, pltpu.roll(blk, HEAD_DIM, axis=1), 0.0))
    return jnp.concatenate(out, axis=1)


def _inproj_kernel(tiles_per_seq, x_ref, sh_ref, sc_ref, w_ref, padd_ref, alog_ref, selq_ref, selk_ref, cq_ref,
                   ck_ref, cv_ref, gqkv_ref, z_ref, small_ref, fk_ref, fv_ref, *rest):
    carry_ref = rest[-1]
    i = pl.program_id(0)
    tm = x_ref.shape[0]
    h = _rms(x_ref[...]) * (1.0 + sc_ref[0]) + sh_ref[0]
    hb = h.astype(BF16)
    o = 0
    gqkv_ref[...] = _dot(hb, w_ref[:, o:o + 3 * GDN_WIDTH]); o += 3 * GDN_WIDTH
    z_ref[...] = _dot(hb, w_ref[:, o:o + GDN_WIDTH]); o += GDN_WIDTH
    sm = _dot(hb, w_ref[:, o:o + SMALL_W]); o += SMALL_W
    fq = _dot(hb, w_ref[:, o:o + FOX_WIDTH]); o += FOX_WIDTH
    fk = _dot(hb, w_ref[:, o:o + FOX_WIDTH]); o += FOX_WIDTH
    fv = _dot(hb, w_ref[:, o:o + FOX_WIDTH])
    if len(rest) == 2:
        fk_ref[...] = fk
        fv_ref[...] = fv
    else:
        for ref, val in ((fk_ref, fk), (fv_ref, fv)):
            for j in range(FOX_WIDTH // LANES):
                for t in range(tm // LANES):
                    ref[0, j * LANES:(j + 1) * LANES, t * LANES:(t + 1) * LANES] = (
                        val[t * LANES:(t + 1) * LANES, j * LANES:(j + 1) * LANES].T)

    lane = lax.broadcasted_iota(I32, (tm, SMALL_W), 1)
    y = sm + padd_ref[...]
    t = jnp.log1p(jnp.exp(-jnp.abs(y)))
    softplus = jnp.maximum(y, 0.0) + t
    logsig = jnp.minimum(y, 0.0) - t
    small = jnp.where(lane < GDN_HEADS, -jnp.exp(alog_ref[...]) * softplus,
                      jnp.where(lane < 2 * GDN_HEADS, jax.nn.sigmoid(sm),
                                jnp.where(lane < 2 * GDN_HEADS + FOX_HEADS, logsig, 0.0)))
    small_ref[...] = small
    if len(rest) == 2:
        rest[0][...] = fq
        return
    qa_ref, ka_ref, vat_ref = rest[:3]

    @pl.when(i % tiles_per_seq == 0)
    def _():
        carry_ref[...] = jnp.zeros_like(carry_ref)

    r = lax.broadcasted_iota(I32, (tm, tm), 0)
    c = lax.broadcasted_iota(I32, (tm, tm), 1)
    ltri = (c <= r).astype(F32)
    cum = _dot(ltri, small, precision=HIGHEST) + carry_ref[...]
    carry_ref[...] = cum[tm - 1:tm, :]
    hi, mid, lo = _split3(cum)
    parts = jnp.concatenate([hi, mid, lo], axis=1)
    qa_ref[...] = (_spread_heads(fq * (HEAD_DIM ** -0.5)) + _dot(parts, selq_ref[...]) + cq_ref[...]).astype(BF16)
    ka_ref[...] = (_spread_heads(fk) + _dot(parts, selk_ref[...]) + ck_ref[...]).astype(BF16)
    va = _spread_heads(fv) + cv_ref[...]
    for j in range(va.shape[1] // LANES):
        for t in range(tm // LANES):
            vat_ref[j * LANES:(j + 1) * LANES, t * LANES:(t + 1) * LANES] = (
                va[t * LANES:(t + 1) * LANES, j * LANES:(j + 1) * LANES].T.astype(BF16))


def _inproj_consts():
    selq = np.zeros((3 * SMALL_W, FOX_HEADS * LANES), np.float32)
    selk = np.zeros((3 * SMALL_W, FOX_HEADS * LANES), np.float32)
    cq = np.zeros((1, FOX_HEADS * LANES), np.float32)
    ck = np.zeros((1, FOX_HEADS * LANES), np.float32)
    cv = np.zeros((1, FOX_HEADS * LANES), np.float32)
    for h in range(FOX_HEADS):
        base = h * LANES + HEAD_DIM
        for p in range(3):
            src = p * SMALL_W + 2 * GDN_HEADS + h
            selq[src, base + p] = 1.0
            ck[0, base + p] = 1.0
            selk[src, base + 3 + p] = -1.0
            cq[0, base + 3 + p] = 1.0
        cv[0, base] = 1.0
    return (jnp.asarray(selq, BF16), jnp.asarray(selk, BF16), jnp.asarray(cq), jnp.asarray(ck), jnp.asarray(cv))


def _prep_inproj(w_in, a_log, dt_bias, fox_fb):
    o = np.cumsum((0, GDN_WIDTH, GDN_WIDTH, GDN_WIDTH, GDN_WIDTH, GDN_HEADS, GDN_HEADS, FOX_WIDTH, FOX_WIDTH,
                   FOX_WIDTH, FOX_HEADS)).tolist()
    d = w_in.shape[0]
    nsmall = 2 * GDN_HEADS + FOX_HEADS
    small = jnp.concatenate([w_in[:, o[4]:o[6]], w_in[:, o[9]:o[10]], jnp.zeros((d, SMALL_W - nsmall), w_in.dtype)], 1)
    w_cat = jnp.concatenate([w_in[:, :o[4]], small, w_in[:, o[6]:o[9]]], axis=1).astype(BF16)
    zero = jnp.zeros((GDN_HEADS,), F32)
    tail = jnp.zeros((SMALL_W - nsmall,), F32)
    padd = jnp.concatenate([dt_bias.astype(F32), zero, fox_fb.astype(F32), tail]).reshape(1, SMALL_W)
    alog = jnp.concatenate([a_log.astype(F32), zero, zero, tail]).reshape(1, SMALL_W)
    return w_cat, padd, alog


def _inproj(x, sh, sc, w_cat, padd, pmul, tm, tiles_per_seq, decode=False):
    T, D = x.shape
    rm = sh.shape[1]
    nw = w_cat.shape[1]
    selq, selk, cq, ck, cv = _inproj_consts()
    aw = FOX_HEADS * LANES
    row = lambda w: pl.BlockSpec((tm, w), lambda i: (i, 0))
    const = lambda a: pl.BlockSpec(a.shape, lambda i: (0,) * a.ndim)
    seq = pl.BlockSpec((1, rm, D), lambda i: (i // tiles_per_seq, 0, 0))
    outs = [(3 * GDN_WIDTH, F32), (GDN_WIDTH, F32), (SMALL_W, F32)]
    outs += [(FOX_WIDTH, F32)] * 3 if decode else [(aw, BF16), (aw, BF16)]
    out_specs = [row(w) for w, _ in outs]
    out_shape = [jax.ShapeDtypeStruct((T, w), dt) for w, dt in outs]
    if not decode:
        nseq = T // (tm * tiles_per_seq)
        kv_spec = pl.BlockSpec((1, FOX_WIDTH, tm), lambda i: (i // tiles_per_seq, 0, i % tiles_per_seq))
        kv_shape = jax.ShapeDtypeStruct((nseq, FOX_WIDTH, tm * tiles_per_seq), F32)
        out_specs[3:3] = [kv_spec, kv_spec]
        out_shape[3:3] = [kv_shape, kv_shape]
        out_specs.append(pl.BlockSpec((aw, tm), lambda i: (0, i)))
        out_shape.append(jax.ShapeDtypeStruct((aw, T), BF16))
    return pl.pallas_call(
        functools.partial(_inproj_kernel, tiles_per_seq),
        grid=(T // tm,),
        in_specs=[row(D), seq, seq, const(w_cat), const(padd), const(pmul), const(selq), const(selk), const(cq),
                  const(ck), const(cv)],
        out_specs=out_specs,
        out_shape=out_shape,
        scratch_shapes=[pltpu.VMEM((1, SMALL_W), F32)],
        compiler_params=_params(("arbitrary",), 48),
        name="inproj",
    )(x, sh, sc, w_cat, padd, pmul, selq, selk, cq, ck, cv)


def _head_block(a, base, h, low):
    j, odd = divmod(h, 2)
    blk = a[:, base + j * LANES:base + (j + 1) * LANES]
    if odd:
        blk = pltpu.roll(blk, HEAD_DIM, axis=1)
    return jnp.where(low, blk, 0.0)


def _join_heads(heads):
    return jnp.concatenate([heads[2 * j] + pltpu.roll(heads[2 * j + 1], HEAD_DIM, axis=1)
                            for j in range(len(heads) // 2)], axis=1)


def _unit_lower_inverses(a_mats, r, cc):
    n = a_mats[0].shape[0]
    s = SUBLANES
    same = (r // s) == (cc // s)
    eye = jnp.where(r == cc, 1.0, 0.0)
    bs = [jnp.where(same, -a, 0.0) for a in a_mats]
    ts = [eye + b for b in bs]
    b2s = [_dot3(b, b) for b in bs]
    ts = [t + _dot3(t, b2) for t, b2 in zip(ts, b2s)]
    b4s = [_dot3(b2, b2) for b2 in b2s]
    ts = [t + _dot3(t, b4) for t, b4 in zip(ts, b4s)]
    while s < n:
        same2 = (r // (2 * s)) == (cc // (2 * s))
        new = same2 & jnp.logical_not(same)
        xs = [_dot3(t, jnp.where(new, a, 0.0)) for t, a in zip(ts, a_mats)]
        ts = [t - _dot3(x, t) for t, x in zip(ts, xs)]
        same = same2
        s *= 2
    return ts


def _gdn_kernel(u_ref, z_ref, sm_ref, cw_ref, nw_ref, o_ref, s_out_ref, ubuf, s_ref):
    c = pl.program_id(1)
    C = u_ref.shape[0]
    W3 = 3 * GDN_WIDTH

    @pl.when(c == 0)
    def _():
        ubuf[0:SUBLANES, :] = jnp.zeros((SUBLANES, W3), F32)
        s_ref[...] = jnp.zeros_like(s_ref)

    ubuf[SUBLANES:SUBLANES + C, :] = u_ref[...]
    cw = cw_ref[...]
    conv = cw[CONV_WIDTH - 1:CONV_WIDTH] * ubuf[SUBLANES:SUBLANES + C, :]
    for j in range(1, CONV_WIDTH):
        conv = conv + cw[CONV_WIDTH - 1 - j:CONV_WIDTH - j] * ubuf[SUBLANES - j:SUBLANES - j + C, :]
    ubuf[0:SUBLANES, :] = ubuf[C:C + SUBLANES, :]
    a = _silu(conv)
    z = z_ref[...]
    small = sm_ref[...]

    lane = lax.broadcasted_iota(I32, (C, LANES), 1)
    low = lane < HEAD_DIM
    r = lax.broadcasted_iota(I32, (C, C), 0)
    cc = lax.broadcasted_iota(I32, (C, C), 1)
    causal = cc <= r
    strict = cc < r
    g_cum = _dot(causal.astype(F32), small, precision=HIGHEST)
    g_last = g_cum[C - 1:C, :]
    e_g = jnp.exp(g_cum)
    e_gl = jnp.exp(g_last - g_cum)
    e_last = jnp.exp(g_last)
    pieces = [p.astype(F32) for p in _split3(g_cum)]
    pos_parts = [pieces[0]] + [pltpu.roll(p, GDN_HEADS * i, axis=1) for i, p in enumerate(pieces) if i]
    neg_parts = [pltpu.roll(-p, HEAD_DIM + GDN_HEADS * i, axis=1) for i, p in enumerate(pieces)]

    def g_operands(h):
        p_mat = jnp.zeros((C, LANES), F32)
        q_mat = jnp.zeros((C, LANES), F32)
        for i in range(3):
            lo_lane = lane == h + GDN_HEADS * i
            hi_lane = lane == h + HEAD_DIM + GDN_HEADS * i
            p_mat = jnp.where(lo_lane, pos_parts[i], jnp.where(hi_lane, 1.0, p_mat))
            q_mat = jnp.where(lo_lane, 1.0, jnp.where(hi_lane, neg_parts[i], q_mat))
        return p_mat.astype(BF16), q_mat.astype(BF16)

    nw = nw_ref[...]
    heads = range(GDN_HEADS)
    col = lambda m, h: m[:, h:h + 1]
    cqs = [_head_block(a, 0, h, low) for h in heads]
    cks = [_head_block(a, GDN_WIDTH, h, low) for h in heads]
    vs = [_head_block(a, 2 * GDN_WIDTH, h, low) for h in heads]
    qs = [cq * lax.rsqrt(jnp.sum(cq * cq, axis=1, keepdims=True) + 1e-6) * (HEAD_DIM ** -0.5) for cq in cqs]
    ks = [ck * lax.rsqrt(jnp.sum(ck * ck, axis=1, keepdims=True) + 1e-6) for ck in cks]
    betas = [col(small, GDN_HEADS + h) for h in heads]
    g_diffs = [_dot_nt(*g_operands(h)) for h in heads]
    decays = [jnp.exp(jnp.where(causal, g, NEG_INF)) for g in g_diffs]
    a_mats = [jnp.where(strict, betas[h] * _bdot_nt(ks[h], ks[h]) * decays[h], 0.0) for h in heads]
    rhss = [vs[h] * betas[h] + pltpu.roll(ks[h] * (betas[h] * col(e_g, h)), HEAD_DIM, axis=1) for h in heads]
    invs = _unit_lower_inverses(a_mats, r, cc)
    xs = [_dot3(invs[h], rhss[h]) for h in heads]
    u_mats = [jnp.where(low, x, 0.0) for x in xs]
    w_mats = [jnp.where(low, pltpu.roll(x, HEAD_DIM, axis=1), 0.0) for x in xs]
    a_qks = [_bdot_nt(qs[h], ks[h]) * decays[h] for h in heads]
    s_olds = [s_ref[h] for h in heads]
    vns = [u_mats[h] - _bdot(w_mats[h], s_olds[h]) for h in heads]
    os_ = [_bdot(qs[h] * col(e_g, h), s_olds[h]) + _bdot(a_qks[h], vns[h]) for h in heads]
    for h in heads:
        s_ref[h] = s_olds[h] * col(e_last, h) + _bdot((ks[h] * col(e_gl, h)).T, vns[h])
    outs = []
    for h in heads:
        o = os_[h]
        ms = jnp.sum(o * o, axis=1, keepdims=True) * (1.0 / HEAD_DIM)
        outs.append(o * lax.rsqrt(ms + EPS) * nw * _silu(_head_block(z, 0, h, low)))
    o_ref[...] = _join_heads(outs).astype(o_ref.dtype)

    @pl.when(c == pl.num_programs(1) - 1)
    def _():
        s_out_ref[0] = s_ref[:, 0:HEAD_DIM, 0:HEAD_DIM]


def _gdn_prompt(gqkv, z, small, conv_w, norm_w, nb, L):
    C = GDN_CHUNK
    nch = L // C
    W3 = 3 * GDN_WIDTH
    nw = jnp.concatenate([norm_w.astype(F32), jnp.zeros((LANES - HEAD_DIM,), F32)]).reshape(1, LANES)
    row = lambda w: pl.BlockSpec((C, w), lambda b, c: (b * nch + c, 0))
    return pl.pallas_call(
        _gdn_kernel,
        grid=(nb, nch),
        in_specs=[row(W3), row(GDN_WIDTH), row(SMALL_W),
                  pl.BlockSpec((CONV_WIDTH, W3), lambda b, c: (0, 0)),
                  pl.BlockSpec((1, LANES), lambda b, c: (0, 0))],
        out_specs=[row(GDN_WIDTH),
                   pl.BlockSpec((1, GDN_HEADS, HEAD_DIM, HEAD_DIM), lambda b, c: (b, 0, 0, 0))],
        out_shape=[jax.ShapeDtypeStruct((nb * L, GDN_WIDTH), BF16),
                   jax.ShapeDtypeStruct((nb, GDN_HEADS, HEAD_DIM, HEAD_DIM), F32)],
        scratch_shapes=[pltpu.VMEM((SUBLANES + C, W3), F32), pltpu.VMEM((GDN_HEADS, LANES, LANES), F32)],
        compiler_params=_params(("arbitrary", "arbitrary")),
        name="gdn_prompt",
    )(gqkv, z, small, conv_w, nw)


def _fox_kernel(qt_ref, kt_ref, q_ref, k_ref, vt_ref, o_ref, m_ref, acc_ref):
    p = pl.program_id(2)
    qi = qt_ref[p]
    ki = kt_ref[p]
    tq = q_ref.shape[0]
    tk = k_ref.shape[0]

    @pl.when(ki == 0)
    def _():
        m_ref[...] = jnp.full_like(m_ref, NEG_INF)
        acc_ref[...] = jnp.zeros_like(acc_ref)

    def step(on_diagonal):
        nsplit = min(FOX_ROW_SPLIT, tq // LANES)
        tr = tq // nsplit
        blocks = [slice(i * tr, (i + 1) * tr) for i in range(nsplit)]
        k = k_ref[...]
        vt = vt_ref[...]
        ss = [_dot_nt(k, q_ref[b, :]) for b in blocks]
        if on_diagonal:
            key = lax.broadcasted_iota(I32, (tk, tr), 0)
            qry = lax.broadcasted_iota(I32, (tk, tr), 1)
            ss = [jnp.where(key <= qry + i * tr, s, NEG_INF) for i, s in enumerate(ss)]
        m_olds = [m_ref[:, b] for b in blocks]
        m_news = [jnp.maximum(m, jnp.max(s, axis=0, keepdims=True)) for m, s in zip(m_olds, ss)]
        ps = [jnp.exp(s - m).astype(BF16) for s, m in zip(ss, m_news)]
        for b, m_old, m_new, p in zip(blocks, m_olds, m_news, ps):
            acc_ref[:, b] = acc_ref[:, b] * jnp.exp(m_old - m_new) + _dot(vt, p)
            m_ref[:, b] = m_new

    @pl.when(ki < qi)
    def _():
        step(False)

    @pl.when(ki == qi)
    def _():
        step(True)
        acc = acc_ref[...]
        out = acc / acc[HEAD_DIM:HEAD_DIM + 1, :]
        for j in range(tq // LANES):
            o_ref[j * LANES:(j + 1) * LANES, :] = out[:, j * LANES:(j + 1) * LANES].T.astype(o_ref.dtype)


def _fox_prompt(qa, ka, vat, nb, L, tq):
    nq = L // tq
    assert tq % LANES == 0
    pairs = [(i, j) for i in range(nq) for j in range(i + 1)]
    qt = jnp.asarray([p[0] for p in pairs], I32)
    kt = jnp.asarray([p[1] for p in pairs], I32)
    qspec = pl.BlockSpec((tq, LANES), lambda b, h, p, qt, kt: (b * nq + qt[p], h))
    kspec = pl.BlockSpec((tq, LANES), lambda b, h, p, qt, kt: (b * nq + kt[p], h))
    vspec = pl.BlockSpec((LANES, tq), lambda b, h, p, qt, kt: (h, b * nq + kt[p]))
    return pl.pallas_call(
        _fox_kernel,
        grid_spec=pltpu.PrefetchScalarGridSpec(
            num_scalar_prefetch=2,
            grid=(nb, FOX_HEADS, len(pairs)),
            in_specs=[qspec, kspec, vspec],
            out_specs=qspec,
            scratch_shapes=[pltpu.VMEM((1, tq), F32), pltpu.VMEM((LANES, tq), F32)]),
        out_shape=jax.ShapeDtypeStruct(qa.shape, BF16),
        compiler_params=_params(("arbitrary", "arbitrary", "arbitrary")),
        name="fox_prompt",
    )(qt, kt, qa, ka, vat)


def _outproj_kernel(og_ref, of_ref, x_ref, g1_ref, sh_ref, sc_ref, wg_ref, wf_ref, wq_ref, x1_ref, h2_ref, qp_ref):
    m = _dot(og_ref[...], wg_ref[...]) + _dot(of_ref[...], wf_ref[...])
    x1 = x_ref[...] + g1_ref[0] * m
    x1_ref[...] = x1
    h2 = _rms(x1) * (1.0 + sc_ref[0]) + sh_ref[0]
    h2_ref[...] = h2
    qp_ref[...] = _dot(h2.astype(BF16), wq_ref[...]).astype(BF16)


def _prep_outproj(w_out, peer_wq):
    wg = w_out[:GDN_WIDTH].astype(BF16)
    wf = w_out[GDN_WIDTH:].reshape(FOX_HEADS, HEAD_DIM, -1)
    wf = jnp.pad(wf, ((0, 0), (0, LANES - HEAD_DIM), (0, 0))).reshape(FOX_HEADS * LANES, -1).astype(BF16)
    return wg, wf, peer_wq.astype(BF16)


def _outproj(og, of, x, g1, sh2, sc2, wg, wf, wq, tm, tiles_per_seq):
    T, D = x.shape
    rm = g1.shape[1]
    nq = wq.shape[1]
    row = lambda w: pl.BlockSpec((tm, w), lambda i: (i, 0))
    const = lambda a: pl.BlockSpec(a.shape, lambda i: (0,) * a.ndim)
    seq = pl.BlockSpec((1, rm, D), lambda i: (i // tiles_per_seq, 0, 0))
    return pl.pallas_call(
        _outproj_kernel,
        grid=(T // tm,),
        in_specs=[row(og.shape[1]), row(of.shape[1]), row(D), seq, seq, seq, const(wg), const(wf), const(wq)],
        out_specs=[row(D), row(D), row(nq)],
        out_shape=[jax.ShapeDtypeStruct((T, D), F32), jax.ShapeDtypeStruct((T, D), F32),
                   jax.ShapeDtypeStruct((T, nq), BF16)],
        compiler_params=_params(("arbitrary",), 48),
        name="outproj",
    )(og, of, x, g1, sh2, sc2, wg, wf, wq)


def _topk_rows(ss, k):
    n = ss[0].shape[0]
    iota_n = lax.broadcasted_iota(I32, ss[0].shape, 0).astype(F32)
    vals = [[] for _ in ss]
    idxs = [[] for _ in ss]
    for _ in range(k):
        ms = [jnp.max(s, axis=0, keepdims=True) for s in ss]
        ids = [jnp.min(jnp.where(s == m, iota_n, float(n)), axis=0, keepdims=True) for s, m in zip(ss, ms)]
        ss = [jnp.where(iota_n == i, NEG_INF, s) for s, i in zip(ss, ids)]
        for j, (m, i) in enumerate(zip(ms, ids)):
            vals[j].append(m)
            idxs[j].append(i)
    return [jnp.concatenate(v, axis=0) for v in vals], [jnp.concatenate(i, axis=0) for i in idxs]


def _pair_rows(a0, a1, op):
    k = PEER_TOPK
    rows = [op(a0[0:1], a1[0:SUBLANES]), op(a0[0:1], a1[SUBLANES:k])]
    rows += [op(a0[a:a + 1], a1[0:SUBLANES]) for a in range(1, SUBLANES)]
    rows.append(op(a0[SUBLANES:k], a1[0:1]))
    return jnp.concatenate(rows, axis=0)


def _topk_kernel(q_ref, sk_ref, r_ref, gx_ref, sv_ref, si_ref):
    hp = pl.program_id(1)
    tl = q_ref.shape[0]
    k = PEER_TOPK
    nsub = sk_ref.shape[0]
    dk = sk_ref.shape[2]
    ss = [_dot_nt(sk_ref[i], q_ref[:, i * dk:(i + 1) * dk]) for i in range(nsub)]
    vals, idxs = _topk_rows(ss, k)
    for i in range(nsub):
        sv_ref[hp * nsub + i] = vals[i]
        si_ref[hp * nsub + i] = idxs[i]

    @pl.when(hp == pl.num_programs(1) - 1)
    def _():
        nrow = 10 * SUBLANES
        ridx = lax.broadcasted_iota(I32, (nrow, tl), 0)
        blk = ridx // SUBLANES
        w = ridx % SUBLANES
        flat = jnp.where(blk == 0, w, jnp.where(blk == 1, SUBLANES + w,
                         jnp.where(blk <= SUBLANES, k * (blk - 1) + w, k * (SUBLANES + w)))).astype(F32)
        heads = range(PEER_HEADS)
        cands = [_pair_rows(sv_ref[2 * h], sv_ref[2 * h + 1], lambda a, b: a + b) for h in heads]
        ecands = [_pair_rows(si_ref[2 * h], si_ref[2 * h + 1], lambda a, b: a * float(N_KEYS) + b) for h in heads]
        fvs = [[] for _ in heads]
        es = [[] for _ in heads]
        for _ in range(k):
            ms = [jnp.max(c, axis=0, keepdims=True) for c in cands]
            fs = [jnp.min(jnp.where(c == m, flat, float(k * k)), axis=0, keepdims=True) for c, m in zip(cands, ms)]
            sels = [flat == f for f in fs]
            for h in heads:
                es[h].append(jnp.max(jnp.where(sels[h], ecands[h], -1.0), axis=0, keepdims=True))
                fvs[h].append(ms[h])
            cands = [jnp.where(sel, NEG_INF, c) for sel, c in zip(sels, cands)]
        g_all = []
        for h in heads:
            fv = jnp.concatenate(fvs[h], axis=0)
            ex = jnp.exp(fv - fv[0:1])
            g_all.append(ex / jnp.sum(ex, axis=0, keepdims=True))
        e_mat = jnp.concatenate([jnp.concatenate(e, axis=0) for e in es], axis=0).astype(I32)
        g_mat = jnp.concatenate(g_all, axis=0)
        row_mat = (e_mat & (TABLE_ROWS - 1)).astype(F32)
        half_mat = e_mat >> TABLE_ROWS_LOG2
        for j in range(tl // LANES):
            tok = slice(j * LANES, (j + 1) * LANES)
            r_ref[tok, :] = row_mat[:, tok].T.astype(I32)
        par = lax.broadcasted_iota(I32, (TILE_ROWS, tl), 0) % 2
        for i in range(PEER_HEADS * k // SUBLANES):
            rows = [jnp.where(par == half_mat[n:n + 1], g_mat[n:n + 1], 0.0)
                    for n in range(i * SUBLANES, (i + 1) * SUBLANES)]
            blk = jnp.concatenate(rows, axis=0)
            for j in range(tl // LANES):
                tok = slice(j * LANES, (j + 1) * LANES)
                gx_ref[tok, i * LANES:(i + 1) * LANES] = blk[:, tok].T


def _topk(qp, subkeys, tl):
    T = qp.shape[0]
    nhp = subkeys.shape[0]
    ne = PEER_HEADS * PEER_TOPK
    out = lambda w: pl.BlockSpec((tl, w), lambda i, hp: (i, 0))
    return pl.pallas_call(
        _topk_kernel,
        grid=(T // tl, nhp // TOPK_SETS_PER_STEP),
        in_specs=[pl.BlockSpec((tl, TOPK_SETS_PER_STEP * PEER_DK_HALF), lambda i, hp: (i, hp)),
                  pl.BlockSpec((TOPK_SETS_PER_STEP, N_KEYS, PEER_DK_HALF), lambda i, hp: (hp, 0, 0))],
        out_specs=[out(ne), out(ne * TILE_ROWS)],
        out_shape=[jax.ShapeDtypeStruct((T, ne), I32), jax.ShapeDtypeStruct((T, ne * TILE_ROWS), F32)],
        scratch_shapes=[pltpu.VMEM((nhp, PEER_TOPK, tl), F32), pltpu.VMEM((nhp, PEER_TOPK, tl), F32)],
        compiler_params=_params(("arbitrary", "arbitrary")),
        name="peer_topk",
    )(qp, subkeys)


def _pack_table(w):
    E, D = w.shape
    assert E == 2 * TABLE_ROWS and D == (TILE_ROWS // 2) * LANES
    t = w.astype(BF16).reshape(2, TABLE_ROWS, D // LANES, LANES)
    return jnp.transpose(t, (1, 2, 0, 3)).reshape(TABLE_ROWS, TILE_ROWS, LANES)


def _gather_tiles(tab_ref, r_ref, t, c):
    return jnp.concatenate([tab_ref[r_ref[t, c * EXPERTS_PER_DOT + j]] for j in range(EXPERTS_PER_DOT)], axis=0)


def _fold(x, y, d, low):
    return jnp.where(low, x, y) + pltpu.roll(jnp.where(low, y, x), SUBLANES - d, axis=0)


def _sublane_sums(vs, lows):
    for d, low in zip((1, 2, 4), lows):
        vs = [_fold(vs[2 * i], vs[2 * i + 1], d, low) for i in range(len(vs) // 2)]
    return vs[0]


def _swap8(vs):
    sub = lax.broadcasted_iota(I32, (SUBLANES, LANES), 0)
    for d in (4, 2, 1):
        low = (sub & d) == 0
        out = list(vs)
        for i in range(SUBLANES):
            if i & d == 0:
                out[i] = jnp.where(low, vs[i], pltpu.roll(vs[i + d], d, axis=0))
                out[i + d] = jnp.where(low, pltpu.roll(vs[i], SUBLANES - d, axis=0), vs[i + d])
        vs = out
    return vs


def _segment_mask():
    cw = EXPERTS_PER_DOT * TILE_ROWS
    sub = lax.broadcasted_iota(I32, (SUBLANES, cw), 0)
    lane = lax.broadcasted_iota(I32, (SUBLANES, cw), 1)
    return (lane % TILE_ROWS) // 2 == sub


def _peer_u_kernel(r_ref, x_ref, gx_ref, tab_ref, act_ref):
    tb = x_ref.shape[0]
    ne = r_ref.shape[1]
    cw = EXPERTS_PER_DOT * TILE_ROWS
    sub = lax.broadcasted_iota(I32, (SUBLANES, LANES), 0)
    lows = [(sub & d) == 0 for d in (1, 2, 4)]
    lane = lax.broadcasted_iota(I32, (SUBLANES, LANES), 1)
    seg_mask = _segment_mask()

    def group(base):
        xbs = [x.astype(BF16) for x in
               _swap8([x_ref[pl.ds(base, SUBLANES), s * LANES:(s + 1) * LANES] for s in range(SUBLANES)])]
        cols = []
        for c in range(ne // EXPERTS_PER_DOT):
            zs = [jnp.where(seg_mask, _dot_nt(xbs[tt], _gather_tiles(tab_ref, r_ref, base + tt, c)), 0.0)
                  for tt in range(SUBLANES)]
            for v in range(cw // LANES):
                col = _sublane_sums([z[:, v * LANES:(v + 1) * LANES] for z in zs], lows)
                for dist in (2, 4, 8):
                    up = pltpu.roll(col, LANES - dist, axis=1)
                    dn = pltpu.roll(col, dist, axis=1)
                    col = col + jnp.where((lane & dist) == 0, up, dn)
                cols.append(col)
        d = jnp.concatenate(cols, axis=1)
        gelu = 0.5 * d * (1.0 + lax.erf(d * (2.0 ** -0.5)))
        act_ref[pl.ds(base, SUBLANES), :] = gelu * gx_ref[pl.ds(base, SUBLANES), :]

    def groups(gi, carry):
        for g in range(U_GROUPS_PER_ITER):
            group(pl.multiple_of((gi * U_GROUPS_PER_ITER + g) * SUBLANES, SUBLANES))
        return carry

    lax.fori_loop(0, tb // (SUBLANES * U_GROUPS_PER_ITER), groups, 0)


def _peer_u(rows, x, gx, tab, tb):
    T, ne = rows.shape
    assert x.shape[1] == SUBLANES * LANES
    wide = pl.BlockSpec((tb, gx.shape[1]), lambda i: (i, 0))
    return pl.pallas_call(
        _peer_u_kernel,
        grid=(T // tb,),
        in_specs=[pl.BlockSpec((tb, ne), lambda i: (i, 0), memory_space=pltpu.SMEM),
                  pl.BlockSpec((tb, x.shape[1]), lambda i: (i, 0)),
                  wide,
                  pl.BlockSpec(tab.shape, lambda i: (0, 0, 0), pipeline_mode=pl.Buffered(1))],
        out_specs=wide,
        out_shape=jax.ShapeDtypeStruct(gx.shape, F32),
        compiler_params=_params(("arbitrary",), 48),
        name="peer_u",
    )(rows, x, gx, tab)


def _peer_v_kernel(r_ref, a_ref, x1_ref, g2_ref, fw_ref, tab_ref, y_ref):
    tb = y_ref.shape[0]
    ne = r_ref.shape[1]
    cw = EXPERTS_PER_DOT * TILE_ROWS
    seg_mask = _segment_mask()
    per_row = g2_ref.shape[1] > 1

    def group(base):
        rows8 = pl.ds(base, SUBLANES)
        a8 = a_ref[rows8, :]
        accs = []
        for tt in range(SUBLANES):
            acc = jnp.zeros((SUBLANES, LANES), F32)
            for c in range(ne // EXPERTS_PER_DOT):
                lhs = jnp.where(seg_mask, a8[tt:tt + 1, c * cw:(c + 1) * cw], 0.0).astype(BF16)
                acc = acc + _dot(lhs, _gather_tiles(tab_ref, r_ref, base + tt, c))
            accs.append(acc)
        peer = jnp.concatenate(_swap8(accs), axis=1)
        g2 = g2_ref[0, rows8, :] if per_row else g2_ref[0]
        y_ref[rows8, :] = _rms(x1_ref[rows8, :] + g2 * peer) * fw_ref[...]

    def groups(gi, carry):
        for g in range(V_GROUPS_PER_ITER):
            group(pl.multiple_of((gi * V_GROUPS_PER_ITER + g) * SUBLANES, SUBLANES))
        return carry

    lax.fori_loop(0, tb // (SUBLANES * V_GROUPS_PER_ITER), groups, 0)


def _peer_v(rows, act, x1, g2, fw, tab, tb, tiles_per_seq):
    T, ne = rows.shape
    D = x1.shape[1]
    assert D == SUBLANES * LANES
    if g2.shape[1] == 1:
        g2_spec = pl.BlockSpec((1, 1, D), lambda i: (i // tiles_per_seq, 0, 0))
    else:
        g2_spec = pl.BlockSpec((1, tb, D), lambda i: (i // tiles_per_seq, i % tiles_per_seq, 0))
    row = pl.BlockSpec((tb, D), lambda i: (i, 0))
    return pl.pallas_call(
        _peer_v_kernel,
        grid=(T // tb,),
        in_specs=[pl.BlockSpec((tb, ne), lambda i: (i, 0), memory_space=pltpu.SMEM),
                  pl.BlockSpec((tb, act.shape[1]), lambda i: (i, 0)),
                  row, g2_spec, pl.BlockSpec((1, D), lambda i: (0, 0)),
                  pl.BlockSpec(tab.shape, lambda i: (0, 0, 0), pipeline_mode=pl.Buffered(1))],
        out_specs=row,
        out_shape=jax.ShapeDtypeStruct((T, D), F32),
        compiler_params=_params(("arbitrary",), 48),
        name="peer_v",
    )(rows, act, x1, g2, fw.reshape(1, D).astype(F32), tab)


TILES = (256, 1024, 256, 256, 128)


def _post_mixers(og, of, x, g1, sh2, sc2, g2, wts, tm, rows_per_seq, tl, tb):
    wg, wf, wq, subkeys, tab_u, tab_v, fw = wts
    x1, h2, qp = _outproj(og, of, x, g1, sh2, sc2, wg, wf, wq, tm, rows_per_seq // tm)
    rows, gx = _topk(qp, subkeys, tl)
    act = _peer_u(rows, h2, gx, tab_u, tb)
    return _peer_v(rows, act, x1, g2, fw, tab_v, tb, rows_per_seq // tb)


def _row_to_col(row, n):
    r = lax.broadcasted_iota(I32, (n, n), 0)
    c = lax.broadcasted_iota(I32, (n, n), 1)
    return jnp.sum(jnp.where(r == c, row, 0.0), axis=1, keepdims=True)


def _gdn_step_kernel(u_ref, z_ref, sm_ref, cs_ref, s0_ref, cw_ref, nw_ref, o_ref, cs_out_ref, s_out_ref):
    u = u_ref[0]
    st = cs_ref[0]
    cw = cw_ref[...]
    conv = cw[CONV_WIDTH - 1:CONV_WIDTH] * u
    for j in range(CONV_WIDTH - 1):
        conv = conv + cw[j:j + 1] * st[j:j + 1]
    cs_out_ref[0] = jnp.concatenate([st[1:CONV_WIDTH - 1], u], axis=0)
    a = _silu(conv)
    z = z_ref[0]
    small = sm_ref[0]
    low = lax.broadcasted_iota(I32, (1, LANES), 1) < HEAD_DIM
    nw = nw_ref[...][:, :HEAD_DIM]
    outs = []
    for h in range(GDN_HEADS):
        cq = _head_block(a, 0, h, low)[:, :HEAD_DIM]
        ck = _head_block(a, GDN_WIDTH, h, low)[:, :HEAD_DIM]
        v = _head_block(a, 2 * GDN_WIDTH, h, low)[:, :HEAD_DIM]
        q = cq * lax.rsqrt(jnp.sum(cq * cq, axis=1, keepdims=True) + 1e-6) * (HEAD_DIM ** -0.5)
        k = ck * lax.rsqrt(jnp.sum(ck * ck, axis=1, keepdims=True) + 1e-6)
        g = small[:, h:h + 1]
        beta = small[:, GDN_HEADS + h:GDN_HEADS + h + 1]
        eg = jnp.exp(g)
        s0 = s0_ref[0, h]
        w_col = _row_to_col(k * (beta * eg), HEAD_DIM)
        vn = v * beta - jnp.sum(w_col * s0, axis=0, keepdims=True)
        q_col = _row_to_col(q * eg, HEAD_DIM)
        o = jnp.sum(q_col * s0, axis=0, keepdims=True) + jnp.sum(q * k, axis=1, keepdims=True) * vn
        s_out_ref[0, h] = s0 * eg + _row_to_col(k, HEAD_DIM) * vn
        zh = _head_block(z, 0, h, low)[:, :HEAD_DIM]
        outs.append(_rms(o) * nw * _silu(zh))
    o_ref[0] = jnp.concatenate(outs, axis=1).astype(o_ref.dtype)


def _gdn_step(gqkv, z, small, state_conv, state_delta, conv_w, norm_w):
    nb = gqkv.shape[0]
    W3 = 3 * GDN_WIDTH
    nw = jnp.concatenate([norm_w.astype(F32), jnp.zeros((LANES - HEAD_DIM,), F32)]).reshape(1, LANES)
    per_b = lambda *shape: pl.BlockSpec((1,) + shape, lambda b: (b,) + (0,) * len(shape))
    return pl.pallas_call(
        _gdn_step_kernel,
        grid=(nb,),
        in_specs=[per_b(1, W3), per_b(1, GDN_WIDTH), per_b(1, SMALL_W), per_b(CONV_WIDTH - 1, W3),
                  per_b(GDN_HEADS, HEAD_DIM, HEAD_DIM),
                  pl.BlockSpec((CONV_WIDTH, W3), lambda b: (0, 0)), pl.BlockSpec((1, LANES), lambda b: (0, 0))],
        out_specs=[per_b(1, GDN_WIDTH), per_b(CONV_WIDTH - 1, W3), per_b(GDN_HEADS, HEAD_DIM, HEAD_DIM)],
        out_shape=[jax.ShapeDtypeStruct((nb, 1, GDN_WIDTH), BF16),
                   jax.ShapeDtypeStruct((nb, CONV_WIDTH - 1, W3), F32),
                   jax.ShapeDtypeStruct((nb, GDN_HEADS, HEAD_DIM, HEAD_DIM), F32)],
        compiler_params=_params(("arbitrary",)),
        name="gdn_step",
    )(gqkv.reshape(nb, 1, W3), z.reshape(nb, 1, GDN_WIDTH), small.reshape(nb, 1, SMALL_W), state_conv, state_delta,
      conv_w, nw)


def _fox_decode_kernel(pt_ref, q_ref, kn_ref, vn_ref, sm_ref, *refs):
    page_refs = refs[:-6]
    o_ref, q_s, m_ref, l_ref, acc_ref, carry_ref = refs[-6:]
    p = pl.program_id(1)
    H = FOX_HEADS
    W = FOX_WIDTH
    ps = page_refs[0].shape[3]
    diag = lax.broadcasted_iota(I32, (H, W), 1) // HEAD_DIM == lax.broadcasted_iota(I32, (H, W), 0)
    block_diag = lambda a: jnp.where(diag, jnp.concatenate([a] * H, axis=1), 0.0)

    @pl.when(p == 0)
    def _():
        q = (q_ref[0] * (HEAD_DIM ** -0.5)).astype(BF16)
        q_s[...] = block_diag(q.astype(F32)).astype(BF16)
        m_ref[...] = jnp.sum(q.astype(F32) * kn_ref[0].astype(BF16).astype(F32), axis=1, keepdims=True)
        l_ref[...] = jnp.ones_like(l_ref)
        acc_ref[...] = block_diag(vn_ref[0].astype(BF16).astype(F32))
        lane = lax.broadcasted_iota(I32, (H, SMALL_W), 1)
        row = lax.broadcasted_iota(I32, (H, SMALL_W), 0)
        carry_ref[...] = jnp.sum(jnp.where(lane == row + 2 * GDN_HEADS, sm_ref[0], 0.0), axis=1, keepdims=True)

    j = lax.broadcasted_iota(I32, (ps, ps), 0)
    pos = lax.broadcasted_iota(I32, (ps, ps), 1)
    later = (j > pos).astype(F32)
    npages = len(page_refs) // 3
    k_refs, v_refs, lf_refs = page_refs[:npages], page_refs[npages:2 * npages], page_refs[2 * npages:]
    suffix = [_dot(r[0], later, precision=HIGHEST) for r in lf_refs]
    carry = carry_ref[...]
    ss = []
    for i in range(npages):
        kt = k_refs[i][0].reshape(W, ps).astype(BF16)
        ss.append(_dot(q_s[...], kt) + (suffix[i] + carry))
        carry = carry + jnp.sum(lf_refs[i][0], axis=1, keepdims=True)
    carry_ref[...] = carry
    m_old = m_ref[...]
    m_new = m_old
    for s in ss:
        m_new = jnp.maximum(m_new, jnp.max(s, axis=1, keepdims=True))
    alpha = jnp.exp(m_old - m_new)
    pms = [jnp.exp(s - m_new) for s in ss]
    l_new = l_ref[...] * alpha
    acc = acc_ref[...] * alpha
    for i in range(npages):
        l_new = l_new + jnp.sum(pms[i], axis=1, keepdims=True)
        acc = acc + _dot_nt(pms[i].astype(BF16), v_refs[i][0].reshape(W, ps).astype(BF16))
    l_ref[...] = l_new
    acc_ref[...] = acc
    m_ref[...] = m_new

    @pl.when(p == pl.num_programs(1) - 1)
    def _():
        out = jnp.sum(jnp.where(diag, acc_ref[...] / l_ref[...], 0.0), axis=0, keepdims=True)
        o_ref[0] = _spread_heads(out).astype(o_ref.dtype)


def _fox_decode(page_table, fq, fk, fv, small, cache_k, cache_v, cache_logf):
    nb, npg = page_table.shape
    npool, ps = cache_k.shape[:2]
    H = FOX_HEADS
    W = FOX_WIDTH
    P = PAGES_PER_STEP
    k_t = jnp.transpose(cache_k, (0, 2, 3, 1))
    v_t = jnp.transpose(cache_v, (0, 2, 3, 1))
    lf_t = jnp.transpose(cache_logf, (0, 2, 1))
    assert npg % P == 0
    per_b = lambda *shape: pl.BlockSpec((1,) + shape, lambda b, p, pt: (b,) + (0,) * len(shape))

    def pages(*shape):
        return [pl.BlockSpec((1,) + shape, lambda b, p, pt, i=i: (pt[b, npg - 1 - (p * P + i)],) + (0,) * len(shape))
                for i in range(P)]

    heads = lambda a: a.reshape(nb, H, HEAD_DIM)
    return pl.pallas_call(
        _fox_decode_kernel,
        grid_spec=pltpu.PrefetchScalarGridSpec(
            num_scalar_prefetch=1,
            grid=(nb, npg // P),
            in_specs=[per_b(H, HEAD_DIM), per_b(H, HEAD_DIM), per_b(H, HEAD_DIM), per_b(1, SMALL_W)]
            + pages(H, HEAD_DIM, ps) + pages(H, HEAD_DIM, ps) + pages(H, ps),
            out_specs=per_b(1, H * LANES),
            scratch_shapes=[pltpu.VMEM((H, W), BF16), pltpu.VMEM((H, 1), F32), pltpu.VMEM((H, 1), F32),
                            pltpu.VMEM((H, W), F32), pltpu.VMEM((H, 1), F32)]),
        out_shape=jax.ShapeDtypeStruct((nb, 1, H * LANES), BF16),
        compiler_params=_params(("arbitrary", "arbitrary"), 48),
        name="fox_decode",
    )(page_table, heads(fq), heads(fk), heads(fv), small.reshape(nb, 1, SMALL_W),
      *([k_t] * P + [v_t] * P + [lf_t] * P))


def kernel(x_prompt, x_sample, cache_k, cache_v, cache_logf, state_conv, state_delta, page_table, c_prompt, c_sample, w_mod, b_mod, w_in, conv_w, a_log, dt_bias, gdn_norm_w, fox_fb, w_out, peer_wq, peer_subkeys, peer_u, peer_v, final_norm_w):
    assert w_mod.shape[0] == 1, "one layer"
    D = x_prompt.shape[-1]
    Bp, L = x_prompt.shape[:2]
    Bs = x_sample.shape[0]
    assert x_sample.shape[1] == 1 and L % GDN_CHUNK == 0 and Bs % SUBLANES == 0
    tm_in, tq, tm_out, tl, tb = (min(t, L) for t in TILES)

    c = jnp.concatenate([c_prompt, c_sample], axis=0)
    mod = _mod(c, w_mod[0], b_mod[0])
    mod_p = [mod[:Bp, None, j * D:(j + 1) * D] for j in range(N_MOD)]
    pad_s = (-Bs) % LANES
    mod_s = [mod[None, Bp:, j * D:(j + 1) * D] for j in range(N_MOD)]
    mod_s_pad = [jnp.pad(m, ((0, 0), (0, pad_s), (0, 0))) for m in mod_s]

    w_cat, padd, alog = _prep_inproj(w_in[0], a_log[0], dt_bias[0], fox_fb[0])
    wg, wf, wq = _prep_outproj(w_out[0], peer_wq[0])
    subkeys = peer_subkeys[0].reshape(PEER_HEADS * 2, N_KEYS, PEER_DK_HALF).astype(BF16)
    wts = (wg, wf, wq, subkeys, _pack_table(peer_u[0]), _pack_table(peer_v[0]), final_norm_w)

    xp = x_prompt.reshape(Bp * L, D)
    gqkv, z, small, fkt, fvt, qa, ka, vat = _inproj(xp, mod_p[0], mod_p[1], w_cat, padd, alog, tm_in, L // tm_in)
    og, delta_p = _gdn_prompt(gqkv, z, small, conv_w[0], gdn_norm_w[0], Bp, L)
    of = _fox_prompt(qa, ka, vat, Bp, L, tq)
    y_p = _post_mixers(og, of, xp, mod_p[2], mod_p[3], mod_p[4], mod_p[5], wts, tm_out, L, tl, tb)

    xs = x_sample.reshape(Bs, D)
    gqkv_s, z_s, small_s, fk_s, fv_s, fq_s = _inproj(xs, mod_s[0], mod_s[1], w_cat, padd, alog, Bs, 1, decode=True)
    og_s, conv_s, delta_s = _gdn_step(gqkv_s, z_s, small_s, state_conv[0], state_delta[0], conv_w[0], gdn_norm_w[0])
    of_s = _fox_decode(page_table, fq_s, fk_s, fv_s, small_s, cache_k[0], cache_v[0], cache_logf[0])
    rows = lambda a: jnp.pad(a.reshape(Bs, -1), ((0, pad_s), (0, 0)))
    ts = Bs + pad_s
    y_s = _post_mixers(rows(og_s), rows(of_s), rows(xs), mod_s_pad[2], mod_s_pad[3], mod_s_pad[4], mod_s_pad[5], wts,
                       ts, ts, ts, min(tb, ts))[:Bs]

    nf = 2 * GDN_HEADS
    cache_rows = lambda t: jnp.transpose(t.reshape(Bp, FOX_HEADS, HEAD_DIM, L), (0, 3, 1, 2))[None]
    return (y_p.reshape(Bp, L, D), y_s.reshape(Bs, 1, D), cache_rows(fkt), cache_rows(fvt),
            small[:, nf:nf + FOX_HEADS].reshape(1, Bp, L, FOX_HEADS),
            gqkv.reshape(Bp, L, -1)[None, :, L - (CONV_WIDTH - 1):], delta_p[None],
            fk_s.reshape(1, Bs, 1, FOX_HEADS, HEAD_DIM), fv_s.reshape(1, Bs, 1, FOX_HEADS, HEAD_DIM),
            small_s[:, nf:nf + FOX_HEADS].reshape(1, Bs, 1, FOX_HEADS), conv_s[None], delta_s[None])
```
